```python
import math
import jax, jax.numpy as jnp
from jax import lax
import numpy as np

D_MODEL = 2048
BATCH = 4
SEQ = 4096
DEPTH = 2

MIX_WIDTH = D_MODEL
DA_HEADS = 8
DA_HEAD_DIM = 64
DA_WIDTH = DA_HEADS * 2 * DA_HEAD_DIM
GDN_HEADS = 8
GDN_HEAD_DIM = 128
GDN_WIDTH = GDN_HEADS * GDN_HEAD_DIM
CONV_K = 4
GDN_CHUNK = 64
Q_BLOCK = 128
REL_BUCKETS = 32
REL_MAX_DIST = 128
RMS_EPS = 1e-6
NEG_INF = -1e30
IN_COLS = 4 * DA_WIDTH + 4 * GDN_WIDTH + 2 * GDN_HEADS

kernel_name = "hymba_diffattn_gdn_hybrid"


def rms_norm(x, g):
    xf = x.astype(jnp.float32)
    y = xf * lax.rsqrt(jnp.mean(xf * xf, axis=-1, keepdims=True) + RMS_EPS)
    return (y * g.astype(jnp.float32)).astype(x.dtype)


def l2_normalize(x):
    return x * lax.rsqrt(jnp.sum(x * x, axis=-1, keepdims=True) + RMS_EPS)


def t5_causal_bucket(dist):
    n = jnp.maximum(dist, 0)
    max_exact = REL_BUCKETS // 2
    large = max_exact + (jnp.log(jnp.maximum(n, max_exact).astype(jnp.float32) / max_exact)
                         / math.log(REL_MAX_DIST / max_exact)
                         * (REL_BUCKETS - max_exact)).astype(jnp.int32)
    large = jnp.minimum(large, REL_BUCKETS - 1)
    return jnp.where(n < max_exact, n, large)


def differential_attention(q1, q2, k1, k2, v, lam, rel_bias):
    B, H, T, _ = q1.shape
    n_blk = T // Q_BLOCK
    scale = DA_HEAD_DIM ** -0.5
    kpos = jnp.arange(T)

    def block(i):
        qs = i * Q_BLOCK
        q1b = lax.dynamic_slice_in_dim(q1, qs, Q_BLOCK, axis=2)
        q2b = lax.dynamic_slice_in_dim(q2, qs, Q_BLOCK, axis=2)
        dist = (qs + jnp.arange(Q_BLOCK))[:, None] - kpos[None, :]
        bias = jnp.transpose(rel_bias.astype(jnp.float32)[t5_causal_bucket(dist)], (2, 0, 1))
        causal = dist >= 0
        s1 = jnp.einsum('bhqd,bhkd->bhqk', q1b, k1).astype(jnp.float32) * scale + bias
        s2 = jnp.einsum('bhqd,bhkd->bhqk', q2b, k2).astype(jnp.float32) * scale + bias
        p1 = jax.nn.softmax(jnp.where(causal, s1, NEG_INF), axis=-1)
        p2 = jax.nn.softmax(jnp.where(causal, s2, NEG_INF), axis=-1)
        attn = p1 - lam * p2
        return jnp.einsum('bhqk,bhkv->bhqv', attn.astype(v.dtype), v)

    out = lax.map(block, jnp.arange(n_blk))
    return jnp.transpose(out, (1, 2, 0, 3, 4)).reshape(B, H, T, v.shape[-1])


def causal_depthwise_conv(x, w):
    T = x.shape[1]
    xp = jnp.pad(x, ((0, 0), (CONV_K - 1, 0), (0, 0)))
    y = xp[:, 0:T] * w[0]
    for j in range(1, CONV_K):
        y = y + xp[:, j:j + T] * w[j]
    return y


def gated_delta_rule_chunked(q, k, v, g, beta):
    B, H, T, dk = q.shape
    dv = v.shape[-1]
    C = GDN_CHUNK
    N = T // C
    q = q.reshape(B, H, N, C, dk)
    k = k.reshape(B, H, N, C, dk)
    v = v.reshape(B, H, N, C, dv)
    beta = beta.reshape(B, H, N, C)
    g = jnp.cumsum(g.reshape(B, H, N, C), axis=-1)
    tril = jnp.tril(jnp.ones((C, C), dtype=bool))
    strict = jnp.tril(jnp.ones((C, C), dtype=bool), -1)
    diff = g[..., :, None] - g[..., None, :]
    decay = jnp.where(tril, jnp.exp(jnp.where(tril, diff, 0.0)), 0.0)
    kb = k * beta[..., None]
    vb = v * beta[..., None]
    L = jnp.where(strict, jnp.einsum('bhnid,bhnjd->bhnij', kb, k) * decay, 0.0)
    A = L + jnp.eye(C, dtype=jnp.float32)
    rhs = jnp.concatenate([vb, kb * jnp.exp(g)[..., None]], axis=-1)
    sol = lax.linalg.triangular_solve(A, rhs, left_side=True, lower=True)
    u, w = sol[..., :dv], sol[..., dv:]
    qk = jnp.where(tril, jnp.einsum('bhnid,bhnjd->bhnij', q, k) * decay, 0.0)
    g_last = g[..., -1]

    def step(S, inp):
        qc, kc, uc, wc, qkc, gc, glc = inp
        v_new = uc - jnp.einsum('bhcd,bhdv->bhcv', wc, S)
        o = (jnp.einsum('bhcd,bhdv->bhcv', qc * jnp.exp(gc)[..., None], S)
             + jnp.einsum('bhij,bhjv->bhiv', qkc, v_new))
        S = (S * jnp.exp(glc)[..., None, None]
             + jnp.einsum('bhcd,bhcv->bhdv', kc * jnp.exp(glc[..., None] - gc)[..., None], v_new))
        return S, o

    mv = lambda t: jnp.moveaxis(t, 2, 0)
    S0 = jnp.zeros((B, H, dk, dv), jnp.float32)
    _, o = lax.scan(step, S0, (mv(q), mv(k), mv(u), mv(w), mv(qk), mv(g), mv(g_last)))
    return jnp.moveaxis(o, 0, 2).reshape(B, H, T, dv)


def hybrid_layer(x, norm_w, w_in, w_out, lq1, lk1, lq2, lk2, subln_w, rel_bias,
                 conv_w, a_log, dt_bias, gdn_norm_w, lambda_init):
    B, T, _ = x.shape
    h = rms_norm(x, norm_w)
    proj = jnp.einsum('btd,dc->btc', h, w_in)

    def heads_pair(t):
        t = t.reshape(B, T, DA_HEADS, 2, DA_HEAD_DIM)
        return jnp.transpose(t[..., 0, :], (0, 2, 1, 3)), jnp.transpose(t[..., 1, :], (0, 2, 1, 3))
    q1, q2 = heads_pair(proj[..., 0:DA_WIDTH])
    k1, k2 = heads_pair(proj[..., DA_WIDTH:2 * DA_WIDTH])
    v_da = jnp.transpose(proj[..., 2 * DA_WIDTH:3 * DA_WIDTH].reshape(B, T, DA_HEADS, 2 * DA_HEAD_DIM), (0, 2, 1, 3))
    gate_da = proj[..., 3 * DA_WIDTH:4 * DA_WIDTH]
    lam = (jnp.exp(jnp.sum(lq1.astype(jnp.float32) * lk1.astype(jnp.float32)))
           - jnp.exp(jnp.sum(lq2.astype(jnp.float32) * lk2.astype(jnp.float32)))
           + lambda_init)
    o_da = differential_attention(q1, q2, k1, k2, v_da, lam, rel_bias)
    o_da = rms_norm(o_da, subln_w) * (1.0 - lambda_init)
    o_da = jnp.transpose(o_da, (0, 2, 1, 3)).reshape(B, T, DA_WIDTH)
    o_da = (o_da * jax.nn.silu(gate_da)).astype(x.dtype)

    off = 4 * DA_WIDTH
    qkv = proj[..., off:off + 3 * GDN_WIDTH]
    z = proj[..., off + 3 * GDN_WIDTH:off + 4 * GDN_WIDTH]
    b_raw = proj[..., off + 4 * GDN_WIDTH:off + 4 * GDN_WIDTH + GDN_HEADS]
    a_raw = proj[..., off + 4 * GDN_WIDTH + GDN_HEADS:off + 4 * GDN_WIDTH + 2 * GDN_HEADS]
    qkv = jax.nn.silu(causal_depthwise_conv(qkv, conv_w)).astype(jnp.float32)
    to_heads = lambda t: jnp.transpose(t.reshape(B, T, GDN_HEADS, GDN_HEAD_DIM), (0, 2, 1, 3))
    q = l2_normalize(to_heads(qkv[..., 0:GDN_WIDTH])) * (GDN_HEAD_DIM ** -0.5)
    k = l2_normalize(to_heads(qkv[..., GDN_WIDTH:2 * GDN_WIDTH]))
    v = to_heads(qkv[..., 2 * GDN_WIDTH:3 * GDN_WIDTH])
    beta = jnp.transpose(jax.nn.sigmoid(b_raw.astype(jnp.float32)), (0, 2, 1))
    g = -jnp.exp(a_log.astype(jnp.float32)) * jax.nn.softplus(a_raw.astype(jnp.float32) + dt_bias.astype(jnp.float32))
    g = jnp.transpose(g, (0, 2, 1))
    o_gdn = gated_delta_rule_chunked(q, k, v, g, beta)
    o_gdn = rms_norm(jnp.transpose(o_gdn, (0, 2, 1, 3)), gdn_norm_w)
    o_gdn = o_gdn * jax.nn.silu(z.astype(jnp.float32)).reshape(B, T, GDN_HEADS, GDN_HEAD_DIM)
    o_gdn = o_gdn.reshape(B, T, GDN_WIDTH).astype(x.dtype)

    mixed = jnp.concatenate([o_da, o_gdn], axis=-1)
    return x + jnp.einsum('btc,cd->btd', mixed, w_out).astype(x.dtype)


def setup_inputs(seed: int = 0) -> dict:
    key = jax.random.key(seed)
    ks = jax.random.split(key, 16)
    f32 = jnp.float32
    x = jax.random.normal(ks[0], (BATCH, SEQ, D_MODEL), f32)
    norm_w = 1.0 + 0.02 * jax.random.normal(ks[1], (DEPTH, D_MODEL), f32)
    w_in = jax.random.normal(ks[2], (DEPTH, D_MODEL, IN_COLS), f32) * (D_MODEL ** -0.5)
    w_out = jax.random.normal(ks[3], (DEPTH, MIX_WIDTH, D_MODEL), f32) * (MIX_WIDTH ** -0.5)
    lambda_q1 = 0.1 * jax.random.normal(ks[4], (DEPTH, DA_HEAD_DIM), f32)
    lambda_k1 = 0.1 * jax.random.normal(ks[5], (DEPTH, DA_HEAD_DIM), f32)
    lambda_q2 = 0.1 * jax.random.normal(ks[6], (DEPTH, DA_HEAD_DIM), f32)
    lambda_k2 = 0.1 * jax.random.normal(ks[7], (DEPTH, DA_HEAD_DIM), f32)
    da_subln_w = 1.0 + 0.02 * jax.random.normal(ks[8], (DEPTH, 2 * DA_HEAD_DIM), f32)
    rel_bias = 0.5 * jax.random.normal(ks[9], (REL_BUCKETS, DA_HEADS), f32)
    conv_w = jax.random.normal(ks[10], (DEPTH, CONV_K, 3 * GDN_WIDTH), f32) * (CONV_K ** -0.5)
    a_log = jnp.log(jax.random.uniform(ks[11], (DEPTH, GDN_HEADS), f32, 1.0, 16.0))
    dt_bias = 0.1 * jax.random.normal(ks[12], (DEPTH, GDN_HEADS), f32)
    gdn_norm_w = 1.0 + 0.02 * jax.random.normal(ks[13], (DEPTH, GDN_HEAD_DIM), f32)
    final_norm_w = 1.0 + 0.02 * jax.random.normal(ks[14], (D_MODEL,), f32)
    return {"x": x, "norm_w": norm_w, "w_in": w_in, "w_out": w_out,
            "lambda_q1": lambda_q1, "lambda_k1": lambda_k1, "lambda_q2": lambda_q2, "lambda_k2": lambda_k2,
            "da_subln_w": da_subln_w, "rel_bias": rel_bias, "conv_w": conv_w, "a_log": a_log,
            "dt_bias": dt_bias, "gdn_norm_w": gdn_norm_w, "final_norm_w": final_norm_w}


def reference(x, norm_w, w_in, w_out, lambda_q1, lambda_k1, lambda_q2, lambda_k2,
              da_subln_w, rel_bias, conv_w, a_log, dt_bias, gdn_norm_w, final_norm_w):
    for l in range(DEPTH):
        lambda_init = 0.8 - 0.6 * math.exp(-0.3 * l)
        x = hybrid_layer(x, norm_w[l], w_in[l], w_out[l],
                         lambda_q1[l], lambda_k1[l], lambda_q2[l], lambda_k2[l],
                         da_subln_w[l], rel_bias, conv_w[l], a_log[l], dt_bias[l], gdn_norm_w[l],
                         lambda_init)
    return rms_norm(x, final_norm_w)
```

```python
import functools
import math

import jax
import jax.numpy as jnp
from jax import lax
from jax.experimental import pallas as pl
from jax.experimental.pallas import tpu as pltpu

F32 = jnp.float32
BF16 = jnp.bfloat16

D_MODEL = 2048
DEPTH = 2
DA_HEADS = 8
DA_HEAD_DIM = 64
DA_WIDTH = DA_HEADS * 2 * DA_HEAD_DIM
GDN_HEADS = 8
GDN_HEAD_DIM = 128
GDN_WIDTH = GDN_HEADS * GDN_HEAD_DIM
CONV_K = 4
GDN_CHUNK = 64
REL_BUCKETS = 32
REL_MAX_DIST = 128
REL_MAX_EXACT = REL_BUCKETS // 2
RMS_EPS = 1e-6
NEG_INF = -1e30
MAIN_COLS = 4 * DA_WIDTH + 4 * GDN_WIDTH
AB_COLS = 2 * GDN_HEADS
HEAD_LANES = 128

INPROJ_TM = 1024
INPROJ_TN = 512
ATTN_TQ = 256
ATTN_TK = 256
GDN_TB = 512
OUTPROJ_TM = 256
VMEM_LIMIT = 48 * 1024 * 1024

NT_DIMS = (((1,), (1,)), ((), ()))


def _sigmoid(x):
    return 1.0 / (1.0 + jnp.exp(-x))


def _softplus(x):
    return jnp.maximum(x, 0.0) + jnp.log(1.0 + jnp.exp(-jnp.abs(x)))


def _dot(a, b):
    return jnp.dot(a, b, preferred_element_type=F32)


def _split_hi_lo(a):
    hi = a.astype(BF16)
    lo = (a - hi.astype(F32)).astype(BF16)
    return hi, lo


def _dot_split(a, b):
    a_hi, a_lo = _split_hi_lo(a)
    b_hi, b_lo = _split_hi_lo(b)
    return _dot(a_hi, b_hi) + _dot(a_lo, b_hi) + _dot(a_hi, b_lo)


def _inproj_kernel(x_ref, nw_ref, w_ref, wab_ref, wabt_ref, proj_ref, ab_ref, abt_ref, h_ref):
    @pl.when(pl.program_id(1) == 0)
    def _():
        x = x_ref[...]
        ms = jnp.mean(x * x, axis=-1, keepdims=True)
        h = (x * lax.rsqrt(ms + RMS_EPS) * nw_ref[...]).astype(BF16)
        h_ref[...] = h
        ab_ref[...] = _dot(h, wab_ref[...])
        abt_ref[...] = lax.dot_general(wabt_ref[...], h, NT_DIMS, preferred_element_type=F32)

    proj_ref[...] = _dot(h_ref[...], w_ref[...]).astype(BF16)


def _inproj(x2, norm_w, w_main, w_ab, w_abt):
    m = x2.shape[0]
    tm, tn = INPROJ_TM, INPROJ_TN
    return pl.pallas_call(
        _inproj_kernel,
        grid=(m // tm, MAIN_COLS // tn),
        in_specs=[
            pl.BlockSpec((tm, D_MODEL), lambda i, j: (i, 0)),
            pl.BlockSpec((1, D_MODEL), lambda i, j: (0, 0)),
            pl.BlockSpec((D_MODEL, tn), lambda i, j: (0, j)),
            pl.BlockSpec((D_MODEL, AB_COLS), lambda i, j: (0, 0)),
            pl.BlockSpec((AB_COLS, D_MODEL), lambda i, j: (0, 0)),
        ],
        out_specs=[
            pl.BlockSpec((tm, tn), lambda i, j: (i, j)),
            pl.BlockSpec((tm, AB_COLS), lambda i, j: (i, 0)),
            pl.BlockSpec((AB_COLS, tm), lambda i, j: (0, i)),
        ],
        out_shape=[
            jax.ShapeDtypeStruct((m, MAIN_COLS), BF16),
            jax.ShapeDtypeStruct((m, AB_COLS), F32),
            jax.ShapeDtypeStruct((AB_COLS, m), F32),
        ],
        scratch_shapes=[pltpu.VMEM((tm, D_MODEL), BF16)],
        compiler_params=pltpu.CompilerParams(
            dimension_semantics=("parallel", "arbitrary"), vmem_limit_bytes=VMEM_LIMIT),
        name="inproj",
    )(x2, norm_w, w_main, w_ab, w_abt)


def _bias_kernel(rb_ref, o_ref, *, tk):
    h = pl.program_id(0)
    shape = o_ref.shape
    r = lax.broadcasted_iota(jnp.int32, shape, 0)
    c = lax.broadcasted_iota(jnp.int32, shape, 1)
    dist = r - c + tk
    n = jnp.maximum(dist, 0)
    nf = jnp.maximum(n, REL_MAX_EXACT).astype(F32)
    large = REL_MAX_EXACT + (jnp.log(nf / REL_MAX_EXACT) / math.log(REL_MAX_DIST / REL_MAX_EXACT)
                             * (REL_BUCKETS - REL_MAX_EXACT)).astype(jnp.int32)
    large = jnp.minimum(large, REL_BUCKETS - 1)
    bucket = jnp.where(n < REL_MAX_EXACT, n, large)
    far = rb_ref[REL_BUCKETS - 1, h]
    val = jnp.zeros(shape, F32)
    for b in range(REL_BUCKETS - 1):
        val = jnp.where(bucket == b, rb_ref[b, h] - far, val)
    o_ref[...] = jnp.where(dist >= 0, val, NEG_INF)


def _near_bias(rel_bias):
    tq, tk = ATTN_TQ, ATTN_TK
    return pl.pallas_call(
        functools.partial(_bias_kernel, tk=tk),
        grid=(DA_HEADS,),
        in_specs=[pl.BlockSpec(memory_space=pltpu.SMEM)],
        out_specs=pl.BlockSpec((None, tq, 2 * tk), lambda h: (h, 0, 0)),
        out_shape=jax.ShapeDtypeStruct((DA_HEADS, tq, 2 * tk), F32),
        name="near_bias",
    )(rel_bias)


def _attn_kernel(q_ref, k_ref, v_ref, g_ref, bias_ref, lam_ref, sw_ref, o_ref,
                 qz_ref, m_ref, l_ref, acc_ref, *, tq, tk, lambda_init):
    i = pl.program_id(2)
    q = q_ref[...] * (DA_HEAD_DIM ** -0.5)
    lane = lax.broadcasted_iota(jnp.int32, q.shape, 1)
    zero = jnp.zeros_like(q)
    qz_ref[0:tq, :] = jnp.where(lane < DA_HEAD_DIM, q, zero)
    qz_ref[tq:2 * tq, :] = jnp.where(lane >= DA_HEAD_DIM, q, zero)
    m_ref[...] = jnp.full(m_ref.shape, NEG_INF, F32)
    l_ref[...] = jnp.zeros(l_ref.shape, F32)
    acc_ref[...] = jnp.zeros(acc_ref.shape, F32)

    def step(start, bias):
        kj = k_ref[pl.ds(start, tk), :]
        vj = v_ref[pl.ds(start, tk), :]
        s = lax.dot_general(qz_ref[...], kj, NT_DIMS, preferred_element_type=F32)
        if bias is not None:
            s = s + jnp.concatenate([bias, bias], axis=0)
        m_prev = m_ref[...]
        m_new = jnp.maximum(m_prev, jnp.max(s, axis=-1, keepdims=True))
        alpha = jnp.exp(m_prev - m_new)
        p = jnp.exp(s - m_new)
        l_ref[...] = alpha * l_ref[...] + jnp.sum(p, axis=-1, keepdims=True)
        acc_ref[...] = alpha * acc_ref[...] + _dot(p.astype(BF16), vj)
        m_ref[...] = m_new

    def far_body(j, carry):
        step(pl.multiple_of(j * tk, tk), None)
        return carry

    lax.fori_loop(0, jnp.maximum(i - 1, 0), far_body, 0)

    @pl.when(i > 0)
    def _():
        step(pl.multiple_of((i - 1) * tk, tk), bias_ref[:, 0:tk])

    step(pl.multiple_of(i * tk, tk), bias_ref[:, tk:2 * tk])

    lp = lam_ref[...]
    s1 = jnp.sum(lp[0:1, :] * lp[1:2, :], axis=-1, keepdims=True)
    s2 = jnp.sum(lp[2:3, :] * lp[3:4, :], axis=-1, keepdims=True)
    lam = jnp.exp(s1) - jnp.exp(s2) + lambda_init
    inv_l = 1.0 / l_ref[...]
    o = acc_ref[0:tq, :] * inv_l[0:tq, :] - lam * (acc_ref[tq:2 * tq, :] * inv_l[tq:2 * tq, :])
    ms = jnp.mean(o * o, axis=-1, keepdims=True)
    y = o * lax.rsqrt(ms + RMS_EPS) * sw_ref[...] * (1.0 - lambda_init)
    gate = g_ref[...].astype(F32)
    o_ref[...] = (y * (gate * _sigmoid(gate))).astype(BF16)


def _attention(proj3, near_bias, lam_params, subln_w, lambda_init):
    b, t, _ = proj3.shape
    tq, tk = ATTN_TQ, ATTN_TK
    assert tq == tk and t % tq == 0
    hl = HEAD_LANES
    q_blk = DA_WIDTH // hl
    return pl.pallas_call(
        functools.partial(_attn_kernel, tq=tq, tk=tk, lambda_init=lambda_init),
        grid=(b, DA_HEADS, t // tq),
        in_specs=[
            pl.BlockSpec((None, tq, hl), lambda bi, h, i: (bi, i, h)),
            pl.BlockSpec((None, t, hl), lambda bi, h, i: (bi, 0, q_blk + h)),
            pl.BlockSpec((None, t, hl), lambda bi, h, i: (bi, 0, 2 * q_blk + h)),
            pl.BlockSpec((None, tq, hl), lambda bi, h, i: (bi, i, 3 * q_blk + h)),
            pl.BlockSpec((None, tq, 2 * tk), lambda bi, h, i: (h, 0, 0)),
            pl.BlockSpec((4, DA_HEAD_DIM), lambda bi, h, i: (0, 0)),
            pl.BlockSpec((1, hl), lambda bi, h, i: (0, 0)),
        ],
        out_specs=pl.BlockSpec((None, tq, hl), lambda bi, h, i: (bi, i, h)),
        out_shape=jax.ShapeDtypeStruct((b, t, DA_WIDTH), BF16),
        scratch_shapes=[
            pltpu.VMEM((2 * tq, hl), BF16),
            pltpu.VMEM((2 * tq, 1), F32),
            pltpu.VMEM((2 * tq, 1), F32),
            pltpu.VMEM((2 * tq, hl), F32),
        ],
        compiler_params=pltpu.CompilerParams(
            dimension_semantics=("parallel", "parallel", "arbitrary"), vmem_limit_bytes=VMEM_LIMIT),
        name="diff_attn",
    )(proj3, proj3, proj3, proj3, near_bias, lam_params, subln_w)


def _unit_lower_inverse(l_strict, eye):
    x = -l_strict
    r = eye + x
    for _ in range(5):
        x = _dot_split(x, x)
        r = r + _dot_split(r, x)
    return r


def _gdn_kernel(qp_ref, kp_ref, vp_ref, z_ref, ab_ref, abt_ref, cw_ref, alog_ref, dtb_ref,
                alogt_ref, dtbt_ref, nw_ref, o_ref,
                s_ref, halo_ref, cbuf_ref, qn_ref, kn_ref, vn_ref, gc_ref, gct_ref, beta_ref, *, tb):
    c_len = GDN_CHUNK
    hl = HEAD_LANES
    n_chunk = tb // c_len
    halo = 8

    @pl.when(pl.program_id(1) == 0)
    def _():
        s_ref[...] = jnp.zeros(s_ref.shape, F32)
        halo_ref[...] = jnp.zeros(halo_ref.shape, F32)

    for g, (src, dst) in enumerate(((qp_ref, qn_ref), (kp_ref, kn_ref), (vp_ref, vn_ref))):
        for h in range(GDN_HEADS):
            cs = slice(h * hl, (h + 1) * hl)
            x = src[:, cs].astype(F32)
            cbuf_ref[0:halo, :] = halo_ref[g, :, cs]
            cbuf_ref[halo:halo + tb, :] = x
            w = cw_ref[:, g * GDN_WIDTH + h * hl:g * GDN_WIDTH + (h + 1) * hl]
            y = w[0:1, :] * cbuf_ref[halo - 3:halo - 3 + tb, :]
            y = y + w[1:2, :] * cbuf_ref[halo - 2:halo - 2 + tb, :]
            y = y + w[2:3, :] * cbuf_ref[halo - 1:halo - 1 + tb, :]
            y = y + w[3:4, :] * x
            halo_ref[g, :, cs] = x[tb - halo:tb, :]
            y = y * _sigmoid(y)
            if g < 2:
                y = y * lax.rsqrt(jnp.sum(y * y, axis=-1, keepdims=True) + RMS_EPS)
            if g == 0:
                y = y * (GDN_HEAD_DIM ** -0.5)
            dst[:, cs] = y

    ab = ab_ref[...]
    beta_ref[...] = _sigmoid(ab[:, 0:GDN_HEADS])
    g_raw = -jnp.exp(alog_ref[...]) * _softplus(ab[:, GDN_HEADS:AB_COLS] + dtb_ref[...])
    abt = abt_ref[...]
    g_raw_t = -jnp.exp(alogt_ref[...]) * _softplus(abt[GDN_HEADS:AB_COLS, :] + dtbt_ref[...])
    row = lax.broadcasted_iota(jnp.int32, (c_len, c_len), 0)
    col = lax.broadcasted_iota(jnp.int32, (c_len, c_len), 1)
    tril = row >= col
    strict = row > col
    eye = (row == col).astype(F32)
    tril_f = tril.astype(F32)
    triu_f = (row <= col).astype(F32)
    for c in range(n_chunk):
        gc_ref[c * c_len:(c + 1) * c_len, :] = _dot_split(tril_f, g_raw[c * c_len:(c + 1) * c_len, :])
        gct_ref[c] = _dot_split(g_raw_t[:, c * c_len:(c + 1) * c_len], triu_f)

    def chunk_body(c, carry):
        r0 = pl.multiple_of(c * c_len, c_len)
        rows = pl.ds(r0, c_len)
        gct = gct_ref[c]
        gcc = gc_ref[rows, :]
        betac = beta_ref[rows, :]
        for h in range(GDN_HEADS):
            cs = slice(h * hl, (h + 1) * hl)
            kc = kn_ref[rows, cs]
            qc = qn_ref[rows, cs]
            vc = vn_ref[rows, cs]
            g_col = gcc[:, h:h + 1]
            g_row = gct[h:h + 1, :]
            b_col = betac[:, h:h + 1]
            g_last = g_col[c_len - 1:c_len, :]
            decay = jnp.where(tril, jnp.exp(jnp.where(tril, g_col - g_row, 0.0)), 0.0)
            eg = jnp.exp(g_col)
            kb = kc * b_col
            kk = lax.dot_general(jnp.concatenate([kb, qc], axis=0).astype(BF16), kc.astype(BF16),
                                 NT_DIMS, preferred_element_type=F32)
            l_strict = jnp.where(strict, kk[0:c_len, :] * decay, 0.0)
            qk = kk[c_len:2 * c_len, :] * decay
            a_inv = _unit_lower_inverse(l_strict, eye)
            rhs = jnp.concatenate([vc * b_col, kb * eg], axis=1).astype(BF16)
            sol = _dot(a_inv.astype(BF16), rhs)
            u = sol[:, 0:hl]
            w = sol[:, hl:2 * hl]
            s_old = s_ref[h]
            ws = _dot(jnp.concatenate([w, qc * eg], axis=0).astype(BF16), s_old.astype(BF16))
            v_new = u - ws[0:c_len, :]
            v_new_b = v_new.astype(BF16)
            o = ws[c_len:2 * c_len, :] + _dot(qk.astype(BF16), v_new_b)
            kd_t = (kc * jnp.exp(g_last - g_col)).T.astype(BF16)
            s_ref[h] = s_old * jnp.exp(g_last) + _dot(kd_t, v_new_b)
            o = o * lax.rsqrt(jnp.mean(o * o, axis=-1, keepdims=True) + RMS_EPS) * nw_ref[...]
            zc = z_ref[rows, cs].astype(F32)
            o_ref[rows, cs] = (o * (zc * _sigmoid(zc))).astype(BF16)
        return carry

    lax.fori_loop(0, n_chunk, chunk_body, 0)


def _gdn(proj3, ab3, abt2, conv_w, a_log, dt_bias, gdn_norm_w):
    b, t, _ = proj3.shape
    tb = GDN_TB
    gw = GDN_WIDTH
    first = 4 * DA_WIDTH // gw
    return pl.pallas_call(
        functools.partial(_gdn_kernel, tb=tb),
        grid=(b, t // tb),
        in_specs=[
            pl.BlockSpec((None, tb, gw), lambda bi, ti: (bi, ti, first)),
            pl.BlockSpec((None, tb, gw), lambda bi, ti: (bi, ti, first + 1)),
            pl.BlockSpec((None, tb, gw), lambda bi, ti: (bi, ti, first + 2)),
            pl.BlockSpec((None, tb, gw), lambda bi, ti: (bi, ti, first + 3)),
            pl.BlockSpec((None, tb, AB_COLS), lambda bi, ti: (bi, ti, 0)),
            pl.BlockSpec((AB_COLS, tb), lambda bi, ti: (0, bi * (t // tb) + ti)),
            pl.BlockSpec((CONV_K, 3 * gw), lambda bi, ti: (0, 0)),
            pl.BlockSpec((1, GDN_HEADS), lambda bi, ti: (0, 0)),
            pl.BlockSpec((1, GDN_HEADS), lambda bi, ti: (0, 0)),
            pl.BlockSpec((GDN_HEADS, 1), lambda bi, ti: (0, 0)),
            pl.BlockSpec((GDN_HEADS, 1), lambda bi, ti: (0, 0)),
            pl.BlockSpec((1, GDN_HEAD_DIM), lambda bi, ti: (0, 0)),
        ],
        out_specs=pl.BlockSpec((None, tb, gw), lambda bi, ti: (bi, ti, 0)),
        out_shape=jax.ShapeDtypeStruct((b, t, gw), BF16),
        scratch_shapes=[
            pltpu.VMEM((GDN_HEADS, GDN_HEAD_DIM, GDN_HEAD_DIM), F32),
            pltpu.VMEM((3, 8, gw), F32),
            pltpu.VMEM((tb + 8, HEAD_LANES), F32),
            pltpu.VMEM((tb, gw), F32),
            pltpu.VMEM((tb, gw), F32),
            pltpu.VMEM((tb, gw), F32),
            pltpu.VMEM((tb, GDN_HEADS), F32),
            pltpu.VMEM((tb // GDN_CHUNK, GDN_HEADS, GDN_CHUNK), F32),
            pltpu.VMEM((tb, GDN_HEADS), F32),
        ],
        compiler_params=pltpu.CompilerParams(
            dimension_semantics=("parallel", "arbitrary"), vmem_limit_bytes=VMEM_LIMIT),
        name="gdn",
    )(proj3, proj3, proj3, proj3, ab3, abt2, conv_w,
      a_log.reshape(1, GDN_HEADS), dt_bias.reshape(1, GDN_HEADS),
      a_log.reshape(GDN_HEADS, 1), dt_bias.reshape(GDN_HEADS, 1),
      gdn_norm_w.reshape(1, GDN_HEAD_DIM))


def _outproj_kernel(oda_ref, ogdn_ref, w_ref, x_ref, fw_ref, o_ref, *, final):
    acc = _dot(oda_ref[...], w_ref[0:DA_WIDTH, :]) + _dot(ogdn_ref[...], w_ref[DA_WIDTH:DA_WIDTH + GDN_WIDTH, :])
    y = x_ref[...] + acc
    if final:
        y = y * lax.rsqrt(jnp.mean(y * y, axis=-1, keepdims=True) + RMS_EPS) * fw_ref[...]
    o_ref[...] = y


def _outproj(o_da2, o_gdn2, w_out, x2, final_w, final):
    m = x2.shape[0]
    tm = OUTPROJ_TM
    return pl.pallas_call(
        functools.partial(_outproj_kernel, final=final),
        grid=(m // tm,),
        in_specs=[
            pl.BlockSpec((tm, DA_WIDTH), lambda i: (i, 0)),
            pl.BlockSpec((tm, GDN_WIDTH), lambda i: (i, 0)),
            pl.BlockSpec((DA_WIDTH + GDN_WIDTH, D_MODEL), lambda i: (0, 0)),
            pl.BlockSpec((tm, D_MODEL), lambda i: (i, 0)),
            pl.BlockSpec((1, D_MODEL), lambda i: (0, 0)),
        ],
        out_specs=pl.BlockSpec((tm, D_MODEL), lambda i: (i, 0)),
        out_shape=jax.ShapeDtypeStruct((m, D_MODEL), F32),
        compiler_params=pltpu.CompilerParams(
            dimension_semantics=("parallel",), vmem_limit_bytes=VMEM_LIMIT),
        name="outproj",
    )(o_da2, o_gdn2, w_out, x2, final_w)


def kernel(x, norm_w, w_in, w_out, lambda_q1, lambda_k1, lambda_q2, lambda_k2, da_subln_w, rel_bias,
           conv_w, a_log, dt_bias, gdn_norm_w, final_norm_w):
    b, t, d = x.shape
    m = b * t
    x2 = x.reshape(m, d)
    near_bias = _near_bias(rel_bias)
    final_w = final_norm_w.reshape(1, d)
    for l in range(DEPTH):
        lambda_init = 0.8 - 0.6 * math.exp(-0.3 * l)
        w_main = w_in[l, :, 0:MAIN_COLS].astype(BF16)
        w_ab = w_in[l, :, MAIN_COLS:MAIN_COLS + AB_COLS].astype(BF16)
        proj, ab, abt = _inproj(x2, norm_w[l].reshape(1, d), w_main, w_ab, w_ab.T)
        proj3 = proj.reshape(b, t, MAIN_COLS)
        lam_params = jnp.stack([lambda_q1[l], lambda_k1[l], lambda_q2[l], lambda_k2[l]])
        o_da = _attention(proj3, near_bias, lam_params, da_subln_w[l].reshape(1, 2 * DA_HEAD_DIM), lambda_init)
        o_gdn = _gdn(proj3, ab.reshape(b, t, AB_COLS), abt, conv_w[l], a_log[l], dt_bias[l], gdn_norm_w[l])
        x2 = _outproj(o_da.reshape(m, DA_WIDTH), o_gdn.reshape(m, GDN_WIDTH), w_out[l].astype(BF16),
                      x2, final_w, final=(l == DEPTH - 1))
    return x2.reshape(b, t, d)
```

```python
import functools
import math

import jax
import jax.numpy as jnp
from jax import lax
from jax.experimental import pallas as pl
from jax.experimental.pallas import tpu as pltpu

F32 = jnp.float32
BF16 = jnp.bfloat16

D_MODEL = 2048
DEPTH = 2
DA_HEADS = 8
DA_HEAD_DIM = 64
DA_WIDTH = DA_HEADS * 2 * DA_HEAD_DIM
GDN_HEADS = 8
GDN_HEAD_DIM = 128
GDN_WIDTH = GDN_HEADS * GDN_HEAD_DIM
CONV_K = 4
GDN_CHUNK = 64
REL_BUCKETS = 32
REL_MAX_DIST = 128
REL_MAX_EXACT = REL_BUCKETS // 2
RMS_EPS = 1e-6
NEG_INF = -1e30
MAIN_COLS = 4 * DA_WIDTH + 4 * GDN_WIDTH
AB_COLS = 2 * GDN_HEADS
HEAD_LANES = 128

INPROJ_TM = 1024
INPROJ_TN = 512
ATTN_TQ = 256
ATTN_TK = 256
GDN_TB = 512
OUTPROJ_TM = 256
VMEM_LIMIT = 48 * 1024 * 1024

NT_DIMS = (((1,), (1,)), ((), ()))


def _sigmoid(x):
    return 1.0 / (1.0 + jnp.exp(-x))


def _softplus(x):
    return jnp.maximum(x, 0.0) + jnp.log(1.0 + jnp.exp(-jnp.abs(x)))


def _dot(a, b):
    return jnp.dot(a, b, preferred_element_type=F32)


def _split_hi_lo(a):
    hi = a.astype(BF16)
    lo = (a - hi.astype(F32)).astype(BF16)
    return hi, lo


def _dot_split(a, b):
    a_hi, a_lo = _split_hi_lo(a)
    b_hi, b_lo = _split_hi_lo(b)
    return _dot(a_hi, b_hi) + _dot(a_lo, b_hi) + _dot(a_hi, b_lo)


def _inproj_kernel(x_ref, nw_ref, w_ref, wab_ref, wabt_ref, proj_ref, ab_ref, abt_ref, h_ref):
    @pl.when(pl.program_id(1) == 0)
    def _():
        x = x_ref[...]
        ms = jnp.mean(x * x, axis=-1, keepdims=True)
        h = (x * lax.rsqrt(ms + RMS_EPS) * nw_ref[...]).astype(BF16)
        h_ref[...] = h
        ab_ref[...] = _dot(h, wab_ref[...])
        abt_ref[...] = lax.dot_general(wabt_ref[...], h, NT_DIMS, preferred_element_type=F32)

    proj_ref[...] = _dot(h_ref[...], w_ref[...]).astype(BF16)


def _inproj(x2, norm_w, w_main, w_ab, w_abt):
    m = x2.shape[0]
    tm, tn = INPROJ_TM, INPROJ_TN
    return pl.pallas_call(
        _inproj_kernel,
        grid=(m // tm, MAIN_COLS // tn),
        in_specs=[
            pl.BlockSpec((tm, D_MODEL), lambda i, j: (i, 0)),
            pl.BlockSpec((1, D_MODEL), lambda i, j: (0, 0)),
            pl.BlockSpec((D_MODEL, tn), lambda i, j: (0, j)),
            pl.BlockSpec((D_MODEL, AB_COLS), lambda i, j: (0, 0)),
            pl.BlockSpec((AB_COLS, D_MODEL), lambda i, j: (0, 0)),
        ],
        out_specs=[
            pl.BlockSpec((tm, tn), lambda i, j: (i, j)),
            pl.BlockSpec((tm, AB_COLS), lambda i, j: (i, 0)),
            pl.BlockSpec((AB_COLS, tm), lambda i, j: (0, i)),
        ],
        out_shape=[
            jax.ShapeDtypeStruct((m, MAIN_COLS), BF16),
            jax.ShapeDtypeStruct((m, AB_COLS), F32),
            jax.ShapeDtypeStruct((AB_COLS, m), F32),
        ],
        scratch_shapes=[pltpu.VMEM((tm, D_MODEL), BF16)],
        compiler_params=pltpu.CompilerParams(
            dimension_semantics=("parallel", "arbitrary"), vmem_limit_bytes=VMEM_LIMIT),
        name="inproj",
    )(x2, norm_w, w_main, w_ab, w_abt)


def _bias_kernel(rb_ref, o_ref, *, tk):
    h = pl.program_id(0)
    shape = o_ref.shape
    c = lax.broadcasted_iota(jnp.int32, shape, 0)
    r = lax.broadcasted_iota(jnp.int32, shape, 1)
    dist = r - c + tk
    n = jnp.maximum(dist, 0)
    nf = jnp.maximum(n, REL_MAX_EXACT).astype(F32)
    large = REL_MAX_EXACT + (jnp.log(nf / REL_MAX_EXACT) / math.log(REL_MAX_DIST / REL_MAX_EXACT)
                             * (REL_BUCKETS - REL_MAX_EXACT)).astype(jnp.int32)
    large = jnp.minimum(large, REL_BUCKETS - 1)
    bucket = jnp.where(n < REL_MAX_EXACT, n, large)
    far = rb_ref[REL_BUCKETS - 1, h]
    val = jnp.zeros(shape, F32)
    for b in range(REL_BUCKETS - 1):
        val = jnp.where(bucket == b, rb_ref[b, h] - far, val)
    o_ref[...] = jnp.where(dist >= 0, val, NEG_INF)


def _near_bias(rel_bias):
    tq, tk = ATTN_TQ, ATTN_TK
    return pl.pallas_call(
        functools.partial(_bias_kernel, tk=tk),
        grid=(DA_HEADS,),
        in_specs=[pl.BlockSpec(memory_space=pltpu.SMEM)],
        out_specs=pl.BlockSpec((None, 2 * tk, tq), lambda h: (h, 0, 0)),
        out_shape=jax.ShapeDtypeStruct((DA_HEADS, 2 * tk, tq), F32),
        name="near_bias",
    )(rel_bias)


def _attn_kernel(q_ref, k_ref, v_ref, g_ref, bias_ref, lam_ref, sw_ref, o_ref,
                 qz_ref, vt_ref, m_ref, l_ref, acc_ref, *, tq, tk, lambda_init):
    i = pl.program_id(2)

    @pl.when(i == 0)
    def _():
        for j in range(vt_ref.shape[0]):
            vt_ref[j] = v_ref[j * tk:(j + 1) * tk, :].astype(F32).T.astype(BF16)

    q = q_ref[...] * (DA_HEAD_DIM ** -0.5)
    lane = lax.broadcasted_iota(jnp.int32, q.shape, 1)
    zero = jnp.zeros_like(q)
    qz_ref[0:tq, :] = jnp.where(lane < DA_HEAD_DIM, q, zero)
    qz_ref[tq:2 * tq, :] = jnp.where(lane >= DA_HEAD_DIM, q, zero)
    m_ref[...] = jnp.full(m_ref.shape, NEG_INF, F32)
    l_ref[...] = jnp.zeros(l_ref.shape, F32)
    acc_ref[...] = jnp.zeros(acc_ref.shape, F32)

    def step(j, bias_t):
        kj = k_ref[pl.ds(pl.multiple_of(j * tk, tk), tk), :]
        s_t = lax.dot_general(kj, qz_ref[...], NT_DIMS, preferred_element_type=F32)
        if bias_t is not None:
            s_t = s_t + jnp.concatenate([bias_t, bias_t], axis=1)
        m_prev = m_ref[...]
        m_new = jnp.maximum(m_prev, jnp.max(s_t, axis=0, keepdims=True))
        alpha = jnp.exp(m_prev - m_new)
        p_t = jnp.exp(s_t - m_new)
        l_ref[...] = alpha * l_ref[...] + jnp.sum(p_t, axis=0, keepdims=True)
        acc_ref[...] = alpha * acc_ref[...] + _dot(vt_ref[j], p_t.astype(BF16))
        m_ref[...] = m_new

    def far_body(j, carry):
        step(j, None)
        return carry

    lax.fori_loop(0, jnp.maximum(i - 1, 0), far_body, 0)

    @pl.when(i > 0)
    def _():
        step(i - 1, bias_ref[0:tk, :])

    step(i, bias_ref[tk:2 * tk, :])

    lp = lam_ref[...]
    s1 = jnp.sum(lp[0:1, :] * lp[1:2, :], axis=-1, keepdims=True)
    s2 = jnp.sum(lp[2:3, :] * lp[3:4, :], axis=-1, keepdims=True)
    lam = jnp.exp(s1) - jnp.exp(s2) + lambda_init
    inv_l = 1.0 / l_ref[...]
    o_t = acc_ref[:, 0:tq] * inv_l[:, 0:tq] - lam * (acc_ref[:, tq:2 * tq] * inv_l[:, tq:2 * tq])
    o = o_t.T
    ms = jnp.mean(o * o, axis=-1, keepdims=True)
    y = o * lax.rsqrt(ms + RMS_EPS) * sw_ref[...] * (1.0 - lambda_init)
    gate = g_ref[...].astype(F32)
    o_ref[...] = (y * (gate * _sigmoid(gate))).astype(BF16)


def _attention(proj3, near_bias, lam_params, subln_w, lambda_init):
    b, t, _ = proj3.shape
    tq, tk = ATTN_TQ, ATTN_TK
    assert tq == tk and t % tq == 0
    hl = HEAD_LANES
    q_blk = DA_WIDTH // hl
    return pl.pallas_call(
        functools.partial(_attn_kernel, tq=tq, tk=tk, lambda_init=lambda_init),
        grid=(b, DA_HEADS, t // tq),
        in_specs=[
            pl.BlockSpec((None, tq, hl), lambda bi, h, i: (bi, i, h)),
            pl.BlockSpec((None, t, hl), lambda bi, h, i: (bi, 0, q_blk + h)),
            pl.BlockSpec((None, t, hl), lambda bi, h, i: (bi, 0, 2 * q_blk + h)),
            pl.BlockSpec((None, tq, hl), lambda bi, h, i: (bi, i, 3 * q_blk + h)),
            pl.BlockSpec((None, 2 * tk, tq), lambda bi, h, i: (h, 0, 0)),
            pl.BlockSpec((4, DA_HEAD_DIM), lambda bi, h, i: (0, 0)),
            pl.BlockSpec((1, hl), lambda bi, h, i: (0, 0)),
        ],
        out_specs=pl.BlockSpec((None, tq, hl), lambda bi, h, i: (bi, i, h)),
        out_shape=jax.ShapeDtypeStruct((b, t, DA_WIDTH), BF16),
        scratch_shapes=[
            pltpu.VMEM((2 * tq, hl), BF16),
            pltpu.VMEM((t // tk, hl, tk), BF16),
            pltpu.VMEM((1, 2 * tq), F32),
            pltpu.VMEM((1, 2 * tq), F32),
            pltpu.VMEM((hl, 2 * tq), F32),
        ],
        compiler_params=pltpu.CompilerParams(
            dimension_semantics=("parallel", "parallel", "arbitrary"), vmem_limit_bytes=VMEM_LIMIT),
        name="diff_attn",
    )(proj3, proj3, proj3, proj3, near_bias, lam_params, subln_w)


def _unit_lower_inverse(l_strict, eye):
    x = -l_strict
    r = eye + x
    for _ in range(5):
        x = _dot_split(x, x)
        r = r + _dot_split(r, x)
    return r


def _gdn_kernel(qp_ref, kp_ref, vp_ref, z_ref, ab_ref, abt_ref, cw_ref, alog_ref, dtb_ref,
                alogt_ref, dtbt_ref, nw_ref, o_ref,
                s_ref, halo_ref, cbuf_ref, qn_ref, kn_ref, vn_ref, gc_ref, gct_ref, beta_ref, *, tb):
    c_len = GDN_CHUNK
    hl = HEAD_LANES
    n_chunk = tb // c_len
    halo = 8

    @pl.when(pl.program_id(1) == 0)
    def _():
        s_ref[...] = jnp.zeros(s_ref.shape, F32)
        halo_ref[...] = jnp.zeros(halo_ref.shape, F32)

    for g, (src, dst) in enumerate(((qp_ref, qn_ref), (kp_ref, kn_ref), (vp_ref, vn_ref))):
        for h in range(GDN_HEADS):
            cs = slice(h * hl, (h + 1) * hl)
            x = src[:, cs].astype(F32)
            cbuf_ref[0:halo, :] = halo_ref[g, :, cs]
            cbuf_ref[halo:halo + tb, :] = x
            w = cw_ref[:, g * GDN_WIDTH + h * hl:g * GDN_WIDTH + (h + 1) * hl]
            y = w[0:1, :] * cbuf_ref[halo - 3:halo - 3 + tb, :]
            y = y + w[1:2, :] * cbuf_ref[halo - 2:halo - 2 + tb, :]
            y = y + w[2:3, :] * cbuf_ref[halo - 1:halo - 1 + tb, :]
            y = y + w[3:4, :] * x
            halo_ref[g, :, cs] = x[tb - halo:tb, :]
            y = y * _sigmoid(y)
            if g < 2:
                y = y * lax.rsqrt(jnp.sum(y * y, axis=-1, keepdims=True) + RMS_EPS)
            if g == 0:
                y = y * (GDN_HEAD_DIM ** -0.5)
            dst[:, cs] = y

    ab = ab_ref[...]
    beta_ref[...] = _sigmoid(ab[:, 0:GDN_HEADS])
    g_raw = -jnp.exp(alog_ref[...]) * _softplus(ab[:, GDN_HEADS:AB_COLS] + dtb_ref[...])
    abt = abt_ref[...]
    g_raw_t = -jnp.exp(alogt_ref[...]) * _softplus(abt[GDN_HEADS:AB_COLS, :] + dtbt_ref[...])
    row = lax.broadcasted_iota(jnp.int32, (c_len, c_len), 0)
    col = lax.broadcasted_iota(jnp.int32, (c_len, c_len), 1)
    tril = row >= col
    strict = row > col
    eye = (row == col).astype(F32)
    tril_f = tril.astype(F32)
    triu_f = (row <= col).astype(F32)
    for c in range(n_chunk):
        gc_ref[c * c_len:(c + 1) * c_len, :] = _dot_split(tril_f, g_raw[c * c_len:(c + 1) * c_len, :])
        gct_ref[c] = _dot_split(g_raw_t[:, c * c_len:(c + 1) * c_len], triu_f)

    def chunk_body(c, carry):
        r0 = pl.multiple_of(c * c_len, c_len)
        rows = pl.ds(r0, c_len)
        gct = gct_ref[c]
        gcc = gc_ref[rows, :]
        betac = beta_ref[rows, :]
        for h in range(GDN_HEADS):
            cs = slice(h * hl, (h + 1) * hl)
            kc = kn_ref[rows, cs]
            qc = qn_ref[rows, cs]
            vc = vn_ref[rows, cs]
            g_col = gcc[:, h:h + 1]
            g_row = gct[h:h + 1, :]
            b_col = betac[:, h:h + 1]
            g_last = g_col[c_len - 1:c_len, :]
            decay = jnp.where(tril, jnp.exp(jnp.where(tril, g_col - g_row, 0.0)), 0.0)
            eg = jnp.exp(g_col)
            kb = kc * b_col
            kk = lax.dot_general(jnp.concatenate([kb, qc], axis=0).astype(BF16), kc.astype(BF16),
                                 NT_DIMS, preferred_element_type=F32)
            l_strict = jnp.where(strict, kk[0:c_len, :] * decay, 0.0)
            qk = kk[c_len:2 * c_len, :] * decay
            a_inv = _unit_lower_inverse(l_strict, eye)
            rhs = jnp.concatenate([vc * b_col, kb * eg], axis=1).astype(BF16)
            sol = _dot(a_inv.astype(BF16), rhs)
            u = sol[:, 0:hl]
            w = sol[:, hl:2 * hl]
            s_old = s_ref[h]
            ws = _dot(jnp.concatenate([w, qc * eg], axis=0).astype(BF16), s_old.astype(BF16))
            v_new = u - ws[0:c_len, :]
            v_new_b = v_new.astype(BF16)
            o = ws[c_len:2 * c_len, :] + _dot(qk.astype(BF16), v_new_b)
            kd_t = (kc * jnp.exp(g_last - g_col)).T.astype(BF16)
            s_ref[h] = s_old * jnp.exp(g_last) + _dot(kd_t, v_new_b)
            o = o * lax.rsqrt(jnp.mean(o * o, axis=-1, keepdims=True) + RMS_EPS) * nw_ref[...]
            zc = z_ref[rows, cs].astype(F32)
            o_ref[rows, cs] = (o * (zc * _sigmoid(zc))).astype(BF16)
        return carry

    lax.fori_loop(0, n_chunk, chunk_body, 0)


def _gdn(proj3, ab3, abt2, conv_w, a_log, dt_bias, gdn_norm_w):
    b, t, _ = proj3.shape
    tb = GDN_TB
    gw = GDN_WIDTH
    first = 4 * DA_WIDTH // gw
    return pl.pallas_call(
        functools.partial(_gdn_kernel, tb=tb),
        grid=(b, t // tb),
        in_specs=[
            pl.BlockSpec((None, tb, gw), lambda bi, ti: (bi, ti, first)),
            pl.BlockSpec((None, tb, gw), lambda bi, ti: (bi, ti, first + 1)),
            pl.BlockSpec((None, tb, gw), lambda bi, ti: (bi, ti, first + 2)),
            pl.BlockSpec((None, tb, gw), lambda bi, ti: (bi, ti, first + 3)),
            pl.BlockSpec((None, tb, AB_COLS), lambda bi, ti: (bi, ti, 0)),
            pl.BlockSpec((AB_COLS, tb), lambda bi, ti: (0, bi * (t // tb) + ti)),
            pl.BlockSpec((CONV_K, 3 * gw), lambda bi, ti: (0, 0)),
            pl.BlockSpec((1, GDN_HEADS), lambda bi, ti: (0, 0)),
            pl.BlockSpec((1, GDN_HEADS), lambda bi, ti: (0, 0)),
            pl.BlockSpec((GDN_HEADS, 1), lambda bi, ti: (0, 0)),
            pl.BlockSpec((GDN_HEADS, 1), lambda bi, ti: (0, 0)),
            pl.BlockSpec((1, GDN_HEAD_DIM), lambda bi, ti: (0, 0)),
        ],
        out_specs=pl.BlockSpec((None, tb, gw), lambda bi, ti: (bi, ti, 0)),
        out_shape=jax.ShapeDtypeStruct((b, t, gw), BF16),
        scratch_shapes=[
            pltpu.VMEM((GDN_HEADS, GDN_HEAD_DIM, GDN_HEAD_DIM), F32),
            pltpu.VMEM((3, 8, gw), F32),
            pltpu.VMEM((tb + 8, HEAD_LANES), F32),
            pltpu.VMEM((tb, gw), F32),
            pltpu.VMEM((tb, gw), F32),
            pltpu.VMEM((tb, gw), F32),
            pltpu.VMEM((tb, GDN_HEADS), F32),
            pltpu.VMEM((tb // GDN_CHUNK, GDN_HEADS, GDN_CHUNK), F32),
            pltpu.VMEM((tb, GDN_HEADS), F32),
        ],
        compiler_params=pltpu.CompilerParams(
            dimension_semantics=("parallel", "arbitrary"), vmem_limit_bytes=VMEM_LIMIT),
        name="gdn",
    )(proj3, proj3, proj3, proj3, ab3, abt2, conv_w,
      a_log.reshape(1, GDN_HEADS), dt_bias.reshape(1, GDN_HEADS),
      a_log.reshape(GDN_HEADS, 1), dt_bias.reshape(GDN_HEADS, 1),
      gdn_norm_w.reshape(1, GDN_HEAD_DIM))


def _outproj_kernel(oda_ref, ogdn_ref, w_ref, x_ref, fw_ref, o_ref, *, final):
    acc = _dot(oda_ref[...], w_ref[0:DA_WIDTH, :]) + _dot(ogdn_ref[...], w_ref[DA_WIDTH:DA_WIDTH + GDN_WIDTH, :])
    y = x_ref[...] + acc
    if final:
        y = y * lax.rsqrt(jnp.mean(y * y, axis=-1, keepdims=True) + RMS_EPS) * fw_ref[...]
    o_ref[...] = y


def _outproj(o_da2, o_gdn2, w_out, x2, final_w, final):
    m = x2.shape[0]
    tm = OUTPROJ_TM
    return pl.pallas_call(
        functools.partial(_outproj_kernel, final=final),
        grid=(m // tm,),
        in_specs=[
            pl.BlockSpec((tm, DA_WIDTH), lambda i: (i, 0)),
            pl.BlockSpec((tm, GDN_WIDTH), lambda i: (i, 0)),
            pl.BlockSpec((DA_WIDTH + GDN_WIDTH, D_MODEL), lambda i: (0, 0)),
            pl.BlockSpec((tm, D_MODEL), lambda i: (i, 0)),
            pl.BlockSpec((1, D_MODEL), lambda i: (0, 0)),
        ],
        out_specs=pl.BlockSpec((tm, D_MODEL), lambda i: (i, 0)),
        out_shape=jax.ShapeDtypeStruct((m, D_MODEL), F32),
        compiler_params=pltpu.CompilerParams(
            dimension_semantics=("parallel",), vmem_limit_bytes=VMEM_LIMIT),
        name="outproj",
    )(o_da2, o_gdn2, w_out, x2, final_w)


def kernel(x, norm_w, w_in, w_out, lambda_q1, lambda_k1, lambda_q2, lambda_k2, da_subln_w, rel_bias,
           conv_w, a_log, dt_bias, gdn_norm_w, final_norm_w):
    b, t, d = x.shape
    m = b * t
    x2 = x.reshape(m, d)
    near_bias = _near_bias(rel_bias)
    final_w = final_norm_w.reshape(1, d)
    for l in range(DEPTH):
        lambda_init = 0.8 - 0.6 * math.exp(-0.3 * l)
        w_main = w_in[l, :, 0:MAIN_COLS].astype(BF16)
        w_ab = w_in[l, :, MAIN_COLS:MAIN_COLS + AB_COLS].astype(BF16)
        proj, ab, abt = _inproj(x2, norm_w[l].reshape(1, d), w_main, w_ab, w_ab.T)
        proj3 = proj.reshape(b, t, MAIN_COLS)
        lam_params = jnp.stack([lambda_q1[l], lambda_k1[l], lambda_q2[l], lambda_k2[l]])
        o_da = _attention(proj3, near_bias, lam_params, da_subln_w[l].reshape(1, 2 * DA_HEAD_DIM), lambda_init)
        o_gdn = _gdn(proj3, ab.reshape(b, t, AB_COLS), abt, conv_w[l], a_log[l], dt_bias[l], gdn_norm_w[l])
        x2 = _outproj(o_da.reshape(m, DA_WIDTH), o_gdn.reshape(m, GDN_WIDTH), w_out[l].astype(BF16),
                      x2, final_w, final=(l == DEPTH - 1))
    return x2.reshape(b, t, d)
```

```python
import functools
import math

import jax
import jax.numpy as jnp
from jax import lax
from jax.experimental import pallas as pl
from jax.experimental.pallas import tpu as pltpu

F32 = jnp.float32
BF16 = jnp.bfloat16

D_MODEL = 2048
DEPTH = 2
DA_HEADS = 8
DA_HEAD_DIM = 64
DA_WIDTH = DA_HEADS * 2 * DA_HEAD_DIM
GDN_HEADS = 8
GDN_HEAD_DIM = 128
GDN_WIDTH = GDN_HEADS * GDN_HEAD_DIM
CONV_K = 4
GDN_CHUNK = 64
REL_BUCKETS = 32
REL_MAX_DIST = 128
REL_MAX_EXACT = REL_BUCKETS // 2
RMS_EPS = 1e-6
NEG_INF = -1e30
MAIN_COLS = 4 * DA_WIDTH + 4 * GDN_WIDTH
AB_COLS = 2 * GDN_HEADS
HEAD_LANES = 128

INPROJ_TM = 1024
INPROJ_TN = 512
ATTN_TQ = 256
ATTN_TK = 256
GDN_TB = 1024
OUTPROJ_TM = 256
VMEM_LIMIT = 48 * 1024 * 1024
GDN_VMEM_LIMIT = 56 * 1024 * 1024

NT_DIMS = (((1,), (1,)), ((), ()))


def _sigmoid(x):
    return 1.0 / (1.0 + jnp.exp(-x))


def _softplus(x):
    return jnp.maximum(x, 0.0) + jnp.log(1.0 + jnp.exp(-jnp.abs(x)))


def _dot(a, b):
    return jnp.dot(a, b, preferred_element_type=F32)


def _split_hi_lo(a):
    hi = a.astype(BF16)
    lo = (a - hi.astype(F32)).astype(BF16)
    return hi, lo


def _dot_split(a, b):
    a_hi, a_lo = _split_hi_lo(a)
    b_hi, b_lo = _split_hi_lo(b)
    return _dot(a_hi, b_hi) + _dot(a_lo, b_hi) + _dot(a_hi, b_lo)


def _inproj_kernel(x_ref, nw_ref, w_ref, wab_ref, wabt_ref, proj_ref, ab_ref, abt_ref, h_ref):
    @pl.when(pl.program_id(1) == 0)
    def _():
        x = x_ref[...]
        ms = jnp.mean(x * x, axis=-1, keepdims=True)
        h = (x * lax.rsqrt(ms + RMS_EPS) * nw_ref[...]).astype(BF16)
        h_ref[...] = h
        ab_ref[...] = _dot(h, wab_ref[...])
        abt_ref[...] = lax.dot_general(wabt_ref[...], h, NT_DIMS, preferred_element_type=F32)

    proj_ref[...] = _dot(h_ref[...], w_ref[...]).astype(BF16)


def _inproj(x2, norm_w, w_main, w_ab, w_abt):
    m = x2.shape[0]
    tm, tn = INPROJ_TM, INPROJ_TN
    return pl.pallas_call(
        _inproj_kernel,
        grid=(m // tm, MAIN_COLS // tn),
        in_specs=[
            pl.BlockSpec((tm, D_MODEL), lambda i, j: (i, 0)),
            pl.BlockSpec((1, D_MODEL), lambda i, j: (0, 0)),
            pl.BlockSpec((D_MODEL, tn), lambda i, j: (0, j)),
            pl.BlockSpec((D_MODEL, AB_COLS), lambda i, j: (0, 0)),
            pl.BlockSpec((AB_COLS, D_MODEL), lambda i, j: (0, 0)),
        ],
        out_specs=[
            pl.BlockSpec((tm, tn), lambda i, j: (i, j)),
            pl.BlockSpec((tm, AB_COLS), lambda i, j: (i, 0)),
            pl.BlockSpec((AB_COLS, tm), lambda i, j: (0, i)),
        ],
        out_shape=[
            jax.ShapeDtypeStruct((m, MAIN_COLS), BF16),
            jax.ShapeDtypeStruct((m, AB_COLS), F32),
            jax.ShapeDtypeStruct((AB_COLS, m), F32),
        ],
        scratch_shapes=[pltpu.VMEM((tm, D_MODEL), BF16)],
        compiler_params=pltpu.CompilerParams(
            dimension_semantics=("parallel", "arbitrary"), vmem_limit_bytes=VMEM_LIMIT),
        name="inproj",
    )(x2, norm_w, w_main, w_ab, w_abt)


def _bias_kernel(rb_ref, o_ref, *, tk):
    h = pl.program_id(0)
    shape = o_ref.shape
    c = lax.broadcasted_iota(jnp.int32, shape, 0)
    r = lax.broadcasted_iota(jnp.int32, shape, 1)
    dist = r - c + tk
    n = jnp.maximum(dist, 0)
    nf = jnp.maximum(n, REL_MAX_EXACT).astype(F32)
    large = REL_MAX_EXACT + (jnp.log(nf / REL_MAX_EXACT) / math.log(REL_MAX_DIST / REL_MAX_EXACT)
                             * (REL_BUCKETS - REL_MAX_EXACT)).astype(jnp.int32)
    large = jnp.minimum(large, REL_BUCKETS - 1)
    bucket = jnp.where(n < REL_MAX_EXACT, n, large)
    far = rb_ref[REL_BUCKETS - 1, h]
    val = jnp.zeros(shape, F32)
    for b in range(REL_BUCKETS - 1):
        val = jnp.where(bucket == b, rb_ref[b, h] - far, val)
    o_ref[...] = jnp.where(dist >= 0, val, NEG_INF)


def _near_bias(rel_bias):
    tq, tk = ATTN_TQ, ATTN_TK
    return pl.pallas_call(
        functools.partial(_bias_kernel, tk=tk),
        grid=(DA_HEADS,),
        in_specs=[pl.BlockSpec(memory_space=pltpu.SMEM)],
        out_specs=pl.BlockSpec((None, 2 * tk, tq), lambda h: (h, 0, 0)),
        out_shape=jax.ShapeDtypeStruct((DA_HEADS, 2 * tk, tq), F32),
        name="near_bias",
    )(rel_bias)


def _attn_kernel(q_ref, k_ref, v_ref, g_ref, bias_ref, lam_ref, sw_ref, o_ref,
                 qz_ref, vt_ref, m_ref, l_ref, acc_ref, *, tq, tk, lambda_init):
    i = pl.program_id(2)

    @pl.when(i == 0)
    def _():
        for j in range(vt_ref.shape[0]):
            vt_ref[j] = v_ref[j * tk:(j + 1) * tk, :].astype(F32).T.astype(BF16)

    q = q_ref[...] * (DA_HEAD_DIM ** -0.5)
    lane = lax.broadcasted_iota(jnp.int32, q.shape, 1)
    zero = jnp.zeros_like(q)
    qz_ref[0:tq, :] = jnp.where(lane < DA_HEAD_DIM, q, zero)
    qz_ref[tq:2 * tq, :] = jnp.where(lane >= DA_HEAD_DIM, q, zero)
    m_ref[...] = jnp.full(m_ref.shape, NEG_INF, F32)
    l_ref[...] = jnp.zeros(l_ref.shape, F32)
    acc_ref[...] = jnp.zeros(acc_ref.shape, F32)

    def step(j, bias_t):
        kj = k_ref[pl.ds(pl.multiple_of(j * tk, tk), tk), :]
        s_t = lax.dot_general(kj, qz_ref[...], NT_DIMS, preferred_element_type=F32)
        if bias_t is not None:
            s_t = s_t + jnp.concatenate([bias_t, bias_t], axis=1)
        m_prev = m_ref[...]
        m_new = jnp.maximum(m_prev, jnp.max(s_t, axis=0, keepdims=True))
        alpha = jnp.exp(m_prev - m_new)
        p_t = jnp.exp(s_t - m_new)
        l_ref[...] = alpha * l_ref[...] + jnp.sum(p_t, axis=0, keepdims=True)
        acc_ref[...] = alpha * acc_ref[...] + _dot(vt_ref[j], p_t.astype(BF16))
        m_ref[...] = m_new

    def far_body(j, carry):
        step(j, None)
        return carry

    lax.fori_loop(0, jnp.maximum(i - 1, 0), far_body, 0)

    @pl.when(i > 0)
    def _():
        step(i - 1, bias_ref[0:tk, :])

    step(i, bias_ref[tk:2 * tk, :])

    lp = lam_ref[...]
    s1 = jnp.sum(lp[0:1, :] * lp[1:2, :], axis=-1, keepdims=True)
    s2 = jnp.sum(lp[2:3, :] * lp[3:4, :], axis=-1, keepdims=True)
    lam = jnp.exp(s1) - jnp.exp(s2) + lambda_init
    inv_l = 1.0 / l_ref[...]
    o_t = acc_ref[:, 0:tq] * inv_l[:, 0:tq] - lam * (acc_ref[:, tq:2 * tq] * inv_l[:, tq:2 * tq])
    o = o_t.T
    ms = jnp.mean(o * o, axis=-1, keepdims=True)
    y = o * lax.rsqrt(ms + RMS_EPS) * sw_ref[...] * (1.0 - lambda_init)
    gate = g_ref[...].astype(F32)
    o_ref[...] = (y * (gate * _sigmoid(gate))).astype(BF16)


def _attention(proj3, near_bias, lam_params, subln_w, lambda_init):
    b, t, _ = proj3.shape
    tq, tk = ATTN_TQ, ATTN_TK
    assert tq == tk and t % tq == 0
    hl = HEAD_LANES
    q_blk = DA_WIDTH // hl
    return pl.pallas_call(
        functools.partial(_attn_kernel, tq=tq, tk=tk, lambda_init=lambda_init),
        grid=(b, DA_HEADS, t // tq),
        in_specs=[
            pl.BlockSpec((None, tq, hl), lambda bi, h, i: (bi, i, h)),
            pl.BlockSpec((None, t, hl), lambda bi, h, i: (bi, 0, q_blk + h)),
            pl.BlockSpec((None, t, hl), lambda bi, h, i: (bi, 0, 2 * q_blk + h)),
            pl.BlockSpec((None, tq, hl), lambda bi, h, i: (bi, i, 3 * q_blk + h)),
            pl.BlockSpec((None, 2 * tk, tq), lambda bi, h, i: (h, 0, 0)),
            pl.BlockSpec((4, DA_HEAD_DIM), lambda bi, h, i: (0, 0)),
            pl.BlockSpec((1, hl), lambda bi, h, i: (0, 0)),
        ],
        out_specs=pl.BlockSpec((None, tq, hl), lambda bi, h, i: (bi, i, h)),
        out_shape=jax.ShapeDtypeStruct((b, t, DA_WIDTH), BF16),
        scratch_shapes=[
            pltpu.VMEM((2 * tq, hl), BF16),
            pltpu.VMEM((t // tk, hl, tk), BF16),
            pltpu.VMEM((1, 2 * tq), F32),
            pltpu.VMEM((1, 2 * tq), F32),
            pltpu.VMEM((hl, 2 * tq), F32),
        ],
        compiler_params=pltpu.CompilerParams(
            dimension_semantics=("parallel", "parallel", "arbitrary"), vmem_limit_bytes=VMEM_LIMIT),
        name="diff_attn",
    )(proj3, proj3, proj3, proj3, near_bias, lam_params, subln_w)


LST_PITCH = 72
SUBLANES = 8


def _unit_lower_inverse_on_lanes(lt_ref, tt_ref, n_inst):
    c_len = GDN_CHUNK
    sub = lax.broadcasted_iota(jnp.int32, (SUBLANES, n_inst), 0)
    zero = jnp.zeros((SUBLANES, n_inst), F32)
    for i in range(c_len):
        n_blk = (i - 1) // SUBLANES + 1 if i > 0 else 0
        acc = [None] * n_blk
        for k in range(i):
            lik = jnp.broadcast_to(lt_ref[i, k:k + 1, :], (SUBLANES, n_inst))
            for jb in range(k // SUBLANES + 1):
                term = lik * tt_ref[k, jb * SUBLANES:(jb + 1) * SUBLANES, :]
                acc[jb] = term if acc[jb] is None else acc[jb] + term
        for jb in range(c_len // SUBLANES):
            val = -acc[jb] if jb < n_blk else zero
            if jb == i // SUBLANES:
                val = jnp.where(sub == i % SUBLANES, 1.0, val)
            tt_ref[i, jb * SUBLANES:(jb + 1) * SUBLANES, :] = val


def _gdn_kernel(qp_ref, kp_ref, vp_ref, z_ref, ab_ref, abt_ref, cw_ref, alog_ref, dtb_ref,
                alogt_ref, dtbt_ref, nw_ref, o_ref,
                s_ref, cbuf_ref, qn_ref, kn_ref, vn_ref, gc_ref, gct_ref, beta_ref, bt_ref,
                lst_ref, lt_ref, tt_ref, qk_ref, *, tb):
    c_len = GDN_CHUNK
    hl = HEAD_LANES
    n_chunk = tb // c_len
    n_inst = n_chunk * GDN_HEADS
    halo = 8

    @pl.when(pl.program_id(1) == 0)
    def _():
        s_ref[...] = jnp.zeros(s_ref.shape, F32)
        cbuf_ref[:, 0:halo, :] = jnp.zeros((3, halo, GDN_WIDTH), F32)

    def conv_chunk(c, carry):
        rows = pl.ds(pl.multiple_of(c * c_len, c_len), c_len)
        for g, (src, dst) in enumerate(((qp_ref, qn_ref), (kp_ref, kn_ref), (vp_ref, vn_ref))):
            x = src[rows, :].astype(F32)
            cbuf_ref[g, halo:halo + c_len, :] = x
            w = cw_ref[:, g * GDN_WIDTH:(g + 1) * GDN_WIDTH]
            y = w[0:1, :] * cbuf_ref[g, halo - 3:halo - 3 + c_len, :]
            y = y + w[1:2, :] * cbuf_ref[g, halo - 2:halo - 2 + c_len, :]
            y = y + w[2:3, :] * cbuf_ref[g, halo - 1:halo - 1 + c_len, :]
            y = y + w[3:4, :] * x
            cbuf_ref[g, 0:halo, :] = x[c_len - halo:c_len, :]
            y = y * _sigmoid(y)
            for h in range(GDN_HEADS):
                cs = slice(h * hl, (h + 1) * hl)
                yh = y[:, cs]
                if g < 2:
                    yh = yh * lax.rsqrt(jnp.sum(yh * yh, axis=-1, keepdims=True) + RMS_EPS)
                if g == 0:
                    yh = yh * (GDN_HEAD_DIM ** -0.5)
                dst[rows, cs] = yh.astype(BF16)
        return carry

    lax.fori_loop(0, n_chunk, conv_chunk, 0)

    ab = ab_ref[...]
    beta_ref[...] = _sigmoid(ab[:, 0:GDN_HEADS])
    g_raw = -jnp.exp(alog_ref[...]) * _softplus(ab[:, GDN_HEADS:AB_COLS] + dtb_ref[...])
    abt = abt_ref[...]
    beta_t = _sigmoid(abt[0:GDN_HEADS, :])
    g_raw_t = -jnp.exp(alogt_ref[...]) * _softplus(abt[GDN_HEADS:AB_COLS, :] + dtbt_ref[...])
    row = lax.broadcasted_iota(jnp.int32, (c_len, c_len), 0)
    col = lax.broadcasted_iota(jnp.int32, (c_len, c_len), 1)
    tril = row >= col
    strict = row > col
    tril_f = tril.astype(F32)
    triu_f = (row <= col).astype(F32)
    for c in range(n_chunk):
        gc_ref[c * c_len:(c + 1) * c_len, :] = _dot_split(tril_f, g_raw[c * c_len:(c + 1) * c_len, :])
        gct_ref[c] = _dot_split(g_raw_t[:, c * c_len:(c + 1) * c_len], triu_f)
        bt_ref[c] = beta_t[:, c * c_len:(c + 1) * c_len]

    heads = range(GDN_HEADS)
    lanes = [slice(h * hl, (h + 1) * hl) for h in heads]

    def intra_chunk(c, carry):
        rows = pl.ds(pl.multiple_of(c * c_len, c_len), c_len)
        gct = gct_ref[c]
        gcc = gc_ref[rows, :]
        betac = beta_ref[rows, :]
        ks = [kn_ref[rows, lanes[h]] for h in heads]
        kq = [jnp.concatenate([ks[h], qn_ref[rows, lanes[h]]], axis=0) for h in heads]
        kk = [lax.dot_general(kq[h], ks[h], NT_DIMS, preferred_element_type=F32) for h in heads]
        decay = [jnp.where(tril, jnp.exp(jnp.where(tril, gcc[:, h:h + 1] - gct[h:h + 1, :], 0.0)), 0.0)
                 for h in heads]
        pad = jnp.zeros((c_len, hl - c_len), F32)
        for h in heads:
            l_strict = jnp.where(strict, kk[h][0:c_len, :] * decay[h] * betac[:, h:h + 1], 0.0)
            base = pl.multiple_of((c * GDN_HEADS + h) * LST_PITCH, SUBLANES)
            lst_ref[pl.ds(base, c_len), :] = jnp.concatenate([l_strict, pad], axis=1)
            qk_ref[c * GDN_HEADS + h] = (kk[h][c_len:2 * c_len, :] * decay[h]).astype(BF16)
        return carry

    lax.fori_loop(0, n_chunk, intra_chunk, 0)

    def to_lanes(i, carry):
        x = lst_ref[pl.ds(i, n_inst, stride=LST_PITCH), :]
        lt_ref[i] = x.T[0:c_len, :]
        return carry

    lax.fori_loop(0, c_len, to_lanes, 0)
    _unit_lower_inverse_on_lanes(lt_ref, tt_ref, n_inst)

    def from_lanes(i, carry):
        t = tt_ref[i]
        lst_ref[pl.ds(i, n_inst, stride=LST_PITCH), :] = jnp.concatenate([t, jnp.zeros_like(t)], axis=0).T
        return carry

    lax.fori_loop(0, c_len, from_lanes, 0)

    def scan_chunk(c, carry):
        rows = pl.ds(pl.multiple_of(c * c_len, c_len), c_len)
        gct = gct_ref[c]
        gcc = gc_ref[rows, :]
        bct = bt_ref[c]
        g_row = [gct[h:h + 1, :] for h in heads]
        g_last = [g_row[h][:, c_len - 1:c_len] for h in heads]
        ks = [kn_ref[rows, lanes[h]] for h in heads]
        a_inv = [lst_ref[pl.ds(pl.multiple_of((c * GDN_HEADS + h) * LST_PITCH, SUBLANES), c_len), 0:c_len]
                 for h in heads]
        b_row = [bct[h:h + 1, :] for h in heads]
        u = [_dot((a_inv[h] * b_row[h]).astype(BF16), vn_ref[rows, lanes[h]]) for h in heads]
        w = [_dot((a_inv[h] * (b_row[h] * jnp.exp(g_row[h]))).astype(BF16), ks[h]) for h in heads]
        s_old = [s_ref[h] for h in heads]
        ws = [_dot(jnp.concatenate([w[h].astype(BF16), qn_ref[rows, lanes[h]]], axis=0), s_old[h].astype(BF16))
              for h in heads]
        v_new = [(u[h] - ws[h][0:c_len, :]).astype(BF16) for h in heads]
        kd_t = [(ks[h].astype(F32).T * jnp.exp(g_last[h] - g_row[h])).astype(BF16) for h in heads]
        for h in heads:
            s_ref[h] = s_old[h] * jnp.exp(g_last[h]) + _dot(kd_t[h], v_new[h])
        o = [jnp.exp(gcc[:, h:h + 1]) * ws[h][c_len:2 * c_len, :]
             + _dot(qk_ref[c * GDN_HEADS + h], v_new[h]) for h in heads]
        for h in heads:
            on = o[h] * lax.rsqrt(jnp.mean(o[h] * o[h], axis=-1, keepdims=True) + RMS_EPS) * nw_ref[...]
            zc = z_ref[rows, lanes[h]].astype(F32)
            o_ref[rows, lanes[h]] = (on * (zc * _sigmoid(zc))).astype(BF16)
        return carry

    lax.fori_loop(0, n_chunk, scan_chunk, 0)


def _gdn(proj3, ab3, abt2, conv_w, a_log, dt_bias, gdn_norm_w):
    b, t, _ = proj3.shape
    tb = GDN_TB
    gw = GDN_WIDTH
    first = 4 * DA_WIDTH // gw
    n_chunk = tb // GDN_CHUNK
    n_inst = n_chunk * GDN_HEADS
    assert n_inst == HEAD_LANES, "one (chunk, head) matrix per lane"
    return pl.pallas_call(
        functools.partial(_gdn_kernel, tb=tb),
        grid=(b, t // tb),
        in_specs=[
            pl.BlockSpec((None, tb, gw), lambda bi, ti: (bi, ti, first)),
            pl.BlockSpec((None, tb, gw), lambda bi, ti: (bi, ti, first + 1)),
            pl.BlockSpec((None, tb, gw), lambda bi, ti: (bi, ti, first + 2)),
            pl.BlockSpec((None, tb, gw), lambda bi, ti: (bi, ti, first + 3)),
            pl.BlockSpec((None, tb, AB_COLS), lambda bi, ti: (bi, ti, 0)),
            pl.BlockSpec((AB_COLS, tb), lambda bi, ti: (0, bi * (t // tb) + ti)),
            pl.BlockSpec((CONV_K, 3 * gw), lambda bi, ti: (0, 0)),
            pl.BlockSpec((1, GDN_HEADS), lambda bi, ti: (0, 0)),
            pl.BlockSpec((1, GDN_HEADS), lambda bi, ti: (0, 0)),
            pl.BlockSpec((GDN_HEADS, 1), lambda bi, ti: (0, 0)),
            pl.BlockSpec((GDN_HEADS, 1), lambda bi, ti: (0, 0)),
            pl.BlockSpec((1, GDN_HEAD_DIM), lambda bi, ti: (0, 0)),
        ],
        out_specs=pl.BlockSpec((None, tb, gw), lambda bi, ti: (bi, ti, 0)),
        out_shape=jax.ShapeDtypeStruct((b, t, gw), BF16),
        scratch_shapes=[
            pltpu.VMEM((GDN_HEADS, GDN_HEAD_DIM, GDN_HEAD_DIM), F32),
            pltpu.VMEM((3, 8 + GDN_CHUNK, gw), F32),
            pltpu.VMEM((tb, gw), BF16),
            pltpu.VMEM((tb, gw), BF16),
            pltpu.VMEM((tb, gw), BF16),
            pltpu.VMEM((tb, GDN_HEADS), F32),
            pltpu.VMEM((n_chunk, GDN_HEADS, GDN_CHUNK), F32),
            pltpu.VMEM((tb, GDN_HEADS), F32),
            pltpu.VMEM((n_chunk, GDN_HEADS, GDN_CHUNK), F32),
            pltpu.VMEM((n_inst * LST_PITCH, HEAD_LANES), F32),
            pltpu.VMEM((GDN_CHUNK, GDN_CHUNK, n_inst), F32),
            pltpu.VMEM((GDN_CHUNK, GDN_CHUNK, n_inst), F32),
            pltpu.VMEM((n_inst, GDN_CHUNK, GDN_CHUNK), BF16),
        ],
        compiler_params=pltpu.CompilerParams(
            dimension_semantics=("parallel", "arbitrary"), vmem_limit_bytes=GDN_VMEM_LIMIT),
        name="gdn",
    )(proj3, proj3, proj3, proj3, ab3, abt2, conv_w,
      a_log.reshape(1, GDN_HEADS), dt_bias.reshape(1, GDN_HEADS),
      a_log.reshape(GDN_HEADS, 1), dt_bias.reshape(GDN_HEADS, 1),
      gdn_norm_w.reshape(1, GDN_HEAD_DIM))


def _outproj_kernel(oda_ref, ogdn_ref, w_ref, x_ref, fw_ref, o_ref, *, final):
    acc = _dot(oda_ref[...], w_ref[0:DA_WIDTH, :]) + _dot(ogdn_ref[...], w_ref[DA_WIDTH:DA_WIDTH + GDN_WIDTH, :])
    y = x_ref[...] + acc
    if final:
        y = y * lax.rsqrt(jnp.mean(y * y, axis=-1, keepdims=True) + RMS_EPS) * fw_ref[...]
    o_ref[...] = y


def _outproj(o_da2, o_gdn2, w_out, x2, final_w, final):
    m = x2.shape[0]
    tm = OUTPROJ_TM
    return pl.pallas_call(
        functools.partial(_outproj_kernel, final=final),
        grid=(m // tm,),
        in_specs=[
            pl.BlockSpec((tm, DA_WIDTH), lambda i: (i, 0)),
            pl.BlockSpec((tm, GDN_WIDTH), lambda i: (i, 0)),
            pl.BlockSpec((DA_WIDTH + GDN_WIDTH, D_MODEL), lambda i: (0, 0)),
            pl.BlockSpec((tm, D_MODEL), lambda i: (i, 0)),
            pl.BlockSpec((1, D_MODEL), lambda i: (0, 0)),
        ],
        out_specs=pl.BlockSpec((tm, D_MODEL), lambda i: (i, 0)),
        out_shape=jax.ShapeDtypeStruct((m, D_MODEL), F32),
        compiler_params=pltpu.CompilerParams(
            dimension_semantics=("parallel",), vmem_limit_bytes=VMEM_LIMIT),
        name="outproj",
    )(o_da2, o_gdn2, w_out, x2, final_w)


def kernel(x, norm_w, w_in, w_out, lambda_q1, lambda_k1, lambda_q2, lambda_k2, da_subln_w, rel_bias,
           conv_w, a_log, dt_bias, gdn_norm_w, final_norm_w):
    b, t, d = x.shape
    m = b * t
    x2 = x.reshape(m, d)
    near_bias = _near_bias(rel_bias)
    final_w = final_norm_w.reshape(1, d)
    for l in range(DEPTH):
        lambda_init = 0.8 - 0.6 * math.exp(-0.3 * l)
        w_main = w_in[l, :, 0:MAIN_COLS].astype(BF16)
        w_ab = w_in[l, :, MAIN_COLS:MAIN_COLS + AB_COLS].astype(BF16)
        proj, ab, abt = _inproj(x2, norm_w[l].reshape(1, d), w_main, w_ab, w_ab.T)
        proj3 = proj.reshape(b, t, MAIN_COLS)
        lam_params = jnp.stack([lambda_q1[l], lambda_k1[l], lambda_q2[l], lambda_k2[l]])
        o_da = _attention(proj3, near_bias, lam_params, da_subln_w[l].reshape(1, 2 * DA_HEAD_DIM), lambda_init)
        o_gdn = _gdn(proj3, ab.reshape(b, t, AB_COLS), abt, conv_w[l], a_log[l], dt_bias[l], gdn_norm_w[l])
        x2 = _outproj(o_da.reshape(m, DA_WIDTH), o_gdn.reshape(m, GDN_WIDTH), w_out[l].astype(BF16),
                      x2, final_w, final=(l == DEPTH - 1))
    return x2.reshape(b, t, d)
```

```python
import functools
import math

import jax
import jax.numpy as jnp
from jax import lax
from jax.experimental import pallas as pl
from jax.experimental.pallas import tpu as pltpu

F32 = jnp.float32
BF16 = jnp.bfloat16

D_MODEL = 2048
DEPTH = 2
DA_HEADS = 8
DA_HEAD_DIM = 64
DA_WIDTH = DA_HEADS * 2 * DA_HEAD_DIM
GDN_HEADS = 8
GDN_HEAD_DIM = 128
GDN_WIDTH = GDN_HEADS * GDN_HEAD_DIM
CONV_K = 4
GDN_CHUNK = 64
REL_BUCKETS = 32
REL_MAX_DIST = 128
REL_MAX_EXACT = REL_BUCKETS // 2
RMS_EPS = 1e-6
NEG_INF = -1e30
MAIN_COLS = 4 * DA_WIDTH + 4 * GDN_WIDTH
AB_COLS = 2 * GDN_HEADS
HEAD_LANES = 128
BF16_ROWS = 16
LOG2_E = math.log2(math.e)

INPROJ_TM = 1024
INPROJ_TN = 512
ATTN_TQ = 512
ATTN_TK = 512
ATTN_LANE_TILE = 256
GDN_TB = 1024
OUTPROJ_TM = 256
VMEM_LIMIT = 48 * 1024 * 1024
GDN_VMEM_LIMIT = 56 * 1024 * 1024

NT_DIMS = (((1,), (1,)), ((), ()))


def _sigmoid(x):
    return 1.0 / (1.0 + jnp.exp(-x))


def _softplus(x):
    return jnp.maximum(x, 0.0) + jnp.log(1.0 + jnp.exp(-jnp.abs(x)))


def _dot(a, b):
    return jnp.dot(a, b, preferred_element_type=F32)


def _split_hi_lo(a):
    hi = a.astype(BF16)
    lo = (a - hi.astype(F32)).astype(BF16)
    return hi, lo


def _dot_split(a, b):
    a_hi, a_lo = _split_hi_lo(a)
    b_hi, b_lo = _split_hi_lo(b)
    return _dot(a_hi, b_hi) + _dot(a_lo, b_hi) + _dot(a_hi, b_lo)


def _inproj_kernel(x_ref, nw_ref, w_ref, wab_ref, wabt_ref, proj_ref, ab_ref, abt_ref, h_ref):
    @pl.when(pl.program_id(1) == 0)
    def _():
        x = x_ref[...]
        ms = jnp.mean(x * x, axis=-1, keepdims=True)
        h = (x * lax.rsqrt(ms + RMS_EPS) * nw_ref[...]).astype(BF16)
        h_ref[...] = h
        ab_ref[...] = _dot(h, wab_ref[...])
        abt_ref[...] = lax.dot_general(wabt_ref[...], h, NT_DIMS, preferred_element_type=F32)

    proj_ref[...] = _dot(h_ref[...], w_ref[...]).astype(BF16)


def _inproj(x2, norm_w, w_main, w_ab, w_abt):
    m = x2.shape[0]
    tm, tn = INPROJ_TM, INPROJ_TN
    return pl.pallas_call(
        _inproj_kernel,
        grid=(m // tm, MAIN_COLS // tn),
        in_specs=[
            pl.BlockSpec((tm, D_MODEL), lambda i, j: (i, 0)),
            pl.BlockSpec((1, D_MODEL), lambda i, j: (0, 0)),
            pl.BlockSpec((D_MODEL, tn), lambda i, j: (0, j)),
            pl.BlockSpec((D_MODEL, AB_COLS), lambda i, j: (0, 0)),
            pl.BlockSpec((AB_COLS, D_MODEL), lambda i, j: (0, 0)),
        ],
        out_specs=[
            pl.BlockSpec((tm, tn), lambda i, j: (i, j)),
            pl.BlockSpec((tm, AB_COLS), lambda i, j: (i, 0)),
            pl.BlockSpec((AB_COLS, tm), lambda i, j: (0, i)),
        ],
        out_shape=[
            jax.ShapeDtypeStruct((m, MAIN_COLS), BF16),
            jax.ShapeDtypeStruct((m, AB_COLS), F32),
            jax.ShapeDtypeStruct((AB_COLS, m), F32),
        ],
        scratch_shapes=[pltpu.VMEM((tm, D_MODEL), BF16)],
        compiler_params=pltpu.CompilerParams(
            dimension_semantics=("parallel", "arbitrary"), vmem_limit_bytes=VMEM_LIMIT),
        name="inproj",
    )(x2, norm_w, w_main, w_ab, w_abt)


def _bias_kernel(rb_ref, o_ref, *, tk):
    h = pl.program_id(0)
    shape = o_ref.shape
    c = lax.broadcasted_iota(jnp.int32, shape, 0)
    r = lax.broadcasted_iota(jnp.int32, shape, 1)
    dist = r - c + tk
    n = jnp.maximum(dist, 0)
    nf = jnp.maximum(n, REL_MAX_EXACT).astype(F32)
    large = REL_MAX_EXACT + (jnp.log(nf / REL_MAX_EXACT) / math.log(REL_MAX_DIST / REL_MAX_EXACT)
                             * (REL_BUCKETS - REL_MAX_EXACT)).astype(jnp.int32)
    large = jnp.minimum(large, REL_BUCKETS - 1)
    bucket = jnp.where(n < REL_MAX_EXACT, n, large)
    far = rb_ref[REL_BUCKETS - 1, h]
    val = jnp.zeros(shape, F32)
    for b in range(REL_BUCKETS - 1):
        val = jnp.where(bucket == b, rb_ref[b, h] - far, val)
    o_ref[...] = jnp.where(dist >= 0, val * LOG2_E, NEG_INF)


def _near_bias(rel_bias):
    tq, tk = ATTN_TQ, ATTN_TK
    return pl.pallas_call(
        functools.partial(_bias_kernel, tk=tk),
        grid=(DA_HEADS,),
        in_specs=[pl.BlockSpec(memory_space=pltpu.SMEM)],
        out_specs=pl.BlockSpec((None, 2 * tk, tq), lambda h: (h, 0, 0)),
        out_shape=jax.ShapeDtypeStruct((DA_HEADS, 2 * tk, tq), F32),
        name="near_bias",
    )(rel_bias)


def _attn_kernel(q_ref, k_ref, v_ref, g_ref, bias_ref, lam_ref, sw_ref, o_ref,
                 qz_ref, vt_ref, m_ref, acc_ref, sa_ref, sb_ref, *, tq, tk, lambda_init):
    i = pl.program_id(2)
    hl = HEAD_LANES
    lt = ATTN_LANE_TILE
    n_tiles = 2 * tq // lt

    @pl.when(i == 0)
    def _():
        ones = jnp.ones((vt_ref.shape[1] - hl, tk), BF16)
        for j in range(vt_ref.shape[0]):
            vt_ref[j, 0:hl, :] = v_ref[j * tk:(j + 1) * tk, :].astype(F32).T.astype(BF16)
            vt_ref[j, hl:, :] = ones

    q = q_ref[...] * (DA_HEAD_DIM ** -0.5 * LOG2_E)
    lane = lax.broadcasted_iota(jnp.int32, q.shape, 1)
    zero = jnp.zeros_like(q)
    qz_ref[0:tq, :] = jnp.where(lane < DA_HEAD_DIM, q, zero)
    qz_ref[tq:2 * tq, :] = jnp.where(lane >= DA_HEAD_DIM, q, zero)
    m_ref[...] = jnp.full(m_ref.shape, NEG_INF, F32)
    acc_ref[...] = jnp.zeros(acc_ref.shape, F32)

    tiles = range(n_tiles)
    cols = [slice(t * lt, (t + 1) * lt) for t in tiles]

    def scores(j, s_ref):
        kj = k_ref[pl.ds(pl.multiple_of(j * tk, tk), tk), :]
        for t in tiles:
            s_ref[t] = lax.dot_general(kj, qz_ref[cols[t], :], NT_DIMS, preferred_element_type=F32)

    def softmax_pv(j, s_ref, bias_row0, bias_valid=None):
        vt = vt_ref[j]
        s = [s_ref[t] for t in tiles]
        if bias_row0 is not None:
            bias = [bias_ref[bias_row0:bias_row0 + tk, cols[t]] for t in range(tq // lt)]
            if bias_valid is not None:
                bias = [jnp.where(bias_valid, b, NEG_INF) for b in bias]
            s = [s[t] + bias[t % (tq // lt)] for t in tiles]
        m_prev = [m_ref[:, cols[t]] for t in tiles]
        m_new = [jnp.maximum(m_prev[t], jnp.max(s[t], axis=0, keepdims=True)) for t in tiles]
        p = [jnp.exp2(s[t] - m_new[t]).astype(BF16) for t in tiles]
        pv = [_dot(vt, p[t]) for t in tiles]
        for t in tiles:
            acc_ref[:, cols[t]] = jnp.exp2(m_prev[t] - m_new[t]) * acc_ref[:, cols[t]] + pv[t]
            m_ref[:, cols[t]] = m_new[t]

    n_far = jnp.maximum(i - 1, 0)
    odd = n_far % 2

    @pl.when(odd == 1)
    def _():
        scores(0, sa_ref)
        softmax_pv(0, sa_ref, None)

    scores(odd, sa_ref)

    def far_pair(jj, carry):
        j = odd + 2 * jj
        scores(j + 1, sb_ref)
        softmax_pv(j, sa_ref, None)
        scores(j + 2, sa_ref)
        softmax_pv(j + 1, sb_ref, None)
        return carry

    lax.fori_loop(0, n_far // 2, far_pair, 0)
    scores(i, sb_ref)
    softmax_pv(n_far, sa_ref, 0, bias_valid=i > 0)
    softmax_pv(i, sb_ref, tk)

    lp = lam_ref[...]
    s1 = jnp.sum(lp[0:1, :] * lp[1:2, :], axis=-1, keepdims=True)
    s2 = jnp.sum(lp[2:3, :] * lp[3:4, :], axis=-1, keepdims=True)
    lam = jnp.exp(s1) - jnp.exp(s2) + lambda_init
    inv_l = 1.0 / acc_ref[hl:hl + 1, :]
    o_t = acc_ref[0:hl, 0:tq] * inv_l[:, 0:tq] - lam * (acc_ref[0:hl, tq:2 * tq] * inv_l[:, tq:2 * tq])
    o = o_t.T
    ms = jnp.mean(o * o, axis=-1, keepdims=True)
    y = o * lax.rsqrt(ms + RMS_EPS) * sw_ref[...] * (1.0 - lambda_init)
    gate = g_ref[...].astype(F32)
    o_ref[...] = (y * (gate * _sigmoid(gate))).astype(BF16)


def _attention(proj3, near_bias, lam_params, subln_w, lambda_init):
    b, t, _ = proj3.shape
    tq, tk = ATTN_TQ, ATTN_TK
    assert tq == tk and t % tq == 0
    hl = HEAD_LANES
    q_blk = DA_WIDTH // hl
    return pl.pallas_call(
        functools.partial(_attn_kernel, tq=tq, tk=tk, lambda_init=lambda_init),
        grid=(b, DA_HEADS, t // tq),
        in_specs=[
            pl.BlockSpec((None, tq, hl), lambda bi, h, i: (bi, i, h)),
            pl.BlockSpec((None, t, hl), lambda bi, h, i: (bi, 0, q_blk + h)),
            pl.BlockSpec((None, t, hl), lambda bi, h, i: (bi, 0, 2 * q_blk + h)),
            pl.BlockSpec((None, tq, hl), lambda bi, h, i: (bi, i, 3 * q_blk + h)),
            pl.BlockSpec((None, 2 * tk, tq), lambda bi, h, i: (h, 0, 0)),
            pl.BlockSpec((4, DA_HEAD_DIM), lambda bi, h, i: (0, 0)),
            pl.BlockSpec((1, hl), lambda bi, h, i: (0, 0)),
        ],
        out_specs=pl.BlockSpec((None, tq, hl), lambda bi, h, i: (bi, i, h)),
        out_shape=jax.ShapeDtypeStruct((b, t, DA_WIDTH), BF16),
        scratch_shapes=[
            pltpu.VMEM((2 * tq, hl), BF16),
            pltpu.VMEM((t // tk, hl + BF16_ROWS, tk), BF16),
            pltpu.VMEM((1, 2 * tq), F32),
            pltpu.VMEM((hl + BF16_ROWS, 2 * tq), F32),
            pltpu.VMEM((2 * tq // ATTN_LANE_TILE, tk, ATTN_LANE_TILE), F32),
            pltpu.VMEM((2 * tq // ATTN_LANE_TILE, tk, ATTN_LANE_TILE), F32),
        ],
        compiler_params=pltpu.CompilerParams(
            dimension_semantics=("parallel", "parallel", "arbitrary"), vmem_limit_bytes=VMEM_LIMIT),
        name="diff_attn",
    )(proj3, proj3, proj3, proj3, near_bias, lam_params, subln_w)


LST_PITCH = 72
SUBLANES = 8


def _unit_lower_inverse_on_lanes(lt_ref, tt_ref, n_inst):
    c_len = GDN_CHUNK
    sub = lax.broadcasted_iota(jnp.int32, (SUBLANES, n_inst), 0)
    zero = jnp.zeros((SUBLANES, n_inst), F32)
    for i in range(c_len):
        n_blk = (i - 1) // SUBLANES + 1 if i > 0 else 0
        acc = [None] * n_blk
        for k in range(i):
            lik = jnp.broadcast_to(lt_ref[i, k:k + 1, :], (SUBLANES, n_inst))
            for jb in range(k // SUBLANES + 1):
                term = lik * tt_ref[k, jb * SUBLANES:(jb + 1) * SUBLANES, :]
                acc[jb] = term if acc[jb] is None else acc[jb] + term
        for jb in range(c_len // SUBLANES):
            val = -acc[jb] if jb < n_blk else zero
            if jb == i // SUBLANES:
                val = jnp.where(sub == i % SUBLANES, 1.0, val)
            tt_ref[i, jb * SUBLANES:(jb + 1) * SUBLANES, :] = val


def _gdn_kernel(qp_ref, kp_ref, vp_ref, z_ref, ab_ref, abt_ref, cw_ref, alog_ref, dtb_ref,
                alogt_ref, dtbt_ref, nw_ref, o_ref,
                s_ref, cbuf_ref, qn_ref, kn_ref, vn_ref, gc_ref, gct_ref, beta_ref, bt_ref,
                lst_ref, lt_ref, tt_ref, qk_ref, *, tb):
    c_len = GDN_CHUNK
    hl = HEAD_LANES
    n_chunk = tb // c_len
    n_inst = n_chunk * GDN_HEADS
    halo = 8

    @pl.when(pl.program_id(1) == 0)
    def _():
        s_ref[...] = jnp.zeros(s_ref.shape, F32)
        cbuf_ref[:, 0:halo, :] = jnp.zeros((3, halo, GDN_WIDTH), F32)

    def conv_chunk(c, carry):
        rows = pl.ds(pl.multiple_of(c * c_len, c_len), c_len)
        for g, (src, dst) in enumerate(((qp_ref, qn_ref), (kp_ref, kn_ref), (vp_ref, vn_ref))):
            x = src[rows, :].astype(F32)
            cbuf_ref[g, halo:halo + c_len, :] = x
            w = cw_ref[:, g * GDN_WIDTH:(g + 1) * GDN_WIDTH]
            y = w[0:1, :] * cbuf_ref[g, halo - 3:halo - 3 + c_len, :]
            y = y + w[1:2, :] * cbuf_ref[g, halo - 2:halo - 2 + c_len, :]
            y = y + w[2:3, :] * cbuf_ref[g, halo - 1:halo - 1 + c_len, :]
            y = y + w[3:4, :] * x
            cbuf_ref[g, 0:halo, :] = x[c_len - halo:c_len, :]
            y = y * _sigmoid(y)
            for h in range(GDN_HEADS):
                cs = slice(h * hl, (h + 1) * hl)
                yh = y[:, cs]
                if g < 2:
                    yh = yh * lax.rsqrt(jnp.sum(yh * yh, axis=-1, keepdims=True) + RMS_EPS)
                if g == 0:
                    yh = yh * (GDN_HEAD_DIM ** -0.5)
                dst[rows, cs] = yh.astype(BF16)
        return carry

    lax.fori_loop(0, n_chunk, conv_chunk, 0)

    ab = ab_ref[...]
    beta_ref[...] = _sigmoid(ab[:, 0:GDN_HEADS])
    g_raw = -jnp.exp(alog_ref[...]) * _softplus(ab[:, GDN_HEADS:AB_COLS] + dtb_ref[...])
    abt = abt_ref[...]
    beta_t = _sigmoid(abt[0:GDN_HEADS, :])
    g_raw_t = -jnp.exp(alogt_ref[...]) * _softplus(abt[GDN_HEADS:AB_COLS, :] + dtbt_ref[...])
    row = lax.broadcasted_iota(jnp.int32, (c_len, c_len), 0)
    col = lax.broadcasted_iota(jnp.int32, (c_len, c_len), 1)
    tril = row >= col
    strict = row > col
    tril_f = tril.astype(F32)
    triu_f = (row <= col).astype(F32)
    for c in range(n_chunk):
        gc_ref[c * c_len:(c + 1) * c_len, :] = _dot_split(tril_f, g_raw[c * c_len:(c + 1) * c_len, :])
        gct_ref[c] = _dot_split(g_raw_t[:, c * c_len:(c + 1) * c_len], triu_f)
        bt_ref[c] = beta_t[:, c * c_len:(c + 1) * c_len]

    heads = range(GDN_HEADS)
    lanes = [slice(h * hl, (h + 1) * hl) for h in heads]

    def intra_chunk(c, carry):
        rows = pl.ds(pl.multiple_of(c * c_len, c_len), c_len)
        gct = gct_ref[c]
        gcc = gc_ref[rows, :]
        betac = beta_ref[rows, :]
        ks = [kn_ref[rows, lanes[h]] for h in heads]
        kq = [jnp.concatenate([ks[h], qn_ref[rows, lanes[h]]], axis=0) for h in heads]
        kk = [lax.dot_general(kq[h], ks[h], NT_DIMS, preferred_element_type=F32) for h in heads]
        decay = [jnp.where(tril, jnp.exp(jnp.where(tril, gcc[:, h:h + 1] - gct[h:h + 1, :], 0.0)), 0.0)
                 for h in heads]
        pad = jnp.zeros((c_len, hl - c_len), F32)
        for h in heads:
            l_strict = jnp.where(strict, kk[h][0:c_len, :] * decay[h] * betac[:, h:h + 1], 0.0)
            base = pl.multiple_of((c * GDN_HEADS + h) * LST_PITCH, SUBLANES)
            lst_ref[pl.ds(base, c_len), :] = jnp.concatenate([l_strict, pad], axis=1)
            qk_ref[c * GDN_HEADS + h] = (kk[h][c_len:2 * c_len, :] * decay[h]).astype(BF16)
        return carry

    lax.fori_loop(0, n_chunk, intra_chunk, 0)

    def to_lanes(i, carry):
        x = lst_ref[pl.ds(i, n_inst, stride=LST_PITCH), :]
        lt_ref[i] = x.T[0:c_len, :]
        return carry

    lax.fori_loop(0, c_len, to_lanes, 0)
    _unit_lower_inverse_on_lanes(lt_ref, tt_ref, n_inst)

    def from_lanes(i, carry):
        t = tt_ref[i]
        lst_ref[pl.ds(i, n_inst, stride=LST_PITCH), :] = jnp.concatenate([t, jnp.zeros_like(t)], axis=0).T
        return carry

    lax.fori_loop(0, c_len, from_lanes, 0)

    def scan_chunk(c, carry):
        rows = pl.ds(pl.multiple_of(c * c_len, c_len), c_len)
        gct = gct_ref[c]
        gcc = gc_ref[rows, :]
        bct = bt_ref[c]
        g_row = [gct[h:h + 1, :] for h in heads]
        g_last = [g_row[h][:, c_len - 1:c_len] for h in heads]
        ks = [kn_ref[rows, lanes[h]] for h in heads]
        a_inv = [lst_ref[pl.ds(pl.multiple_of((c * GDN_HEADS + h) * LST_PITCH, SUBLANES), c_len), 0:c_len]
                 for h in heads]
        b_row = [bct[h:h + 1, :] for h in heads]
        u = [_dot((a_inv[h] * b_row[h]).astype(BF16), vn_ref[rows, lanes[h]]) for h in heads]
        w = [_dot((a_inv[h] * (b_row[h] * jnp.exp(g_row[h]))).astype(BF16), ks[h]) for h in heads]
        s_old = [s_ref[h] for h in heads]
        ws = [_dot(jnp.concatenate([w[h].astype(BF16), qn_ref[rows, lanes[h]]], axis=0), s_old[h].astype(BF16))
              for h in heads]
        v_new = [(u[h] - ws[h][0:c_len, :]).astype(BF16) for h in heads]
        kd_t = [(ks[h].astype(F32).T * jnp.exp(g_last[h] - g_row[h])).astype(BF16) for h in heads]
        for h in heads:
            s_ref[h] = s_old[h] * jnp.exp(g_last[h]) + _dot(kd_t[h], v_new[h])
        o = [jnp.exp(gcc[:, h:h + 1]) * ws[h][c_len:2 * c_len, :]
             + _dot(qk_ref[c * GDN_HEADS + h], v_new[h]) for h in heads]
        for h in heads:
            on = o[h] * lax.rsqrt(jnp.mean(o[h] * o[h], axis=-1, keepdims=True) + RMS_EPS) * nw_ref[...]
            zc = z_ref[rows, lanes[h]].astype(F32)
            o_ref[rows, lanes[h]] = (on * (zc * _sigmoid(zc))).astype(BF16)
        return carry

    lax.fori_loop(0, n_chunk, scan_chunk, 0)


def _gdn(proj3, ab3, abt2, conv_w, a_log, dt_bias, gdn_norm_w):
    b, t, _ = proj3.shape
    tb = GDN_TB
    gw = GDN_WIDTH
    first = 4 * DA_WIDTH // gw
    n_chunk = tb // GDN_CHUNK
    n_inst = n_chunk * GDN_HEADS
    assert n_inst == HEAD_LANES, "one (chunk, head) matrix per lane"
    return pl.pallas_call(
        functools.partial(_gdn_kernel, tb=tb),
        grid=(b, t // tb),
        in_specs=[
            pl.BlockSpec((None, tb, gw), lambda bi, ti: (bi, ti, first)),
            pl.BlockSpec((None, tb, gw), lambda bi, ti: (bi, ti, first + 1)),
            pl.BlockSpec((None, tb, gw), lambda bi, ti: (bi, ti, first + 2)),
            pl.BlockSpec((None, tb, gw), lambda bi, ti: (bi, ti, first + 3)),
            pl.BlockSpec((None, tb, AB_COLS), lambda bi, ti: (bi, ti, 0)),
            pl.BlockSpec((AB_COLS, tb), lambda bi, ti: (0, bi * (t // tb) + ti)),
            pl.BlockSpec((CONV_K, 3 * gw), lambda bi, ti: (0, 0)),
            pl.BlockSpec((1, GDN_HEADS), lambda bi, ti: (0, 0)),
            pl.BlockSpec((1, GDN_HEADS), lambda bi, ti: (0, 0)),
            pl.BlockSpec((GDN_HEADS, 1), lambda bi, ti: (0, 0)),
            pl.BlockSpec((GDN_HEADS, 1), lambda bi, ti: (0, 0)),
            pl.BlockSpec((1, GDN_HEAD_DIM), lambda bi, ti: (0, 0)),
        ],
        out_specs=pl.BlockSpec((None, tb, gw), lambda bi, ti: (bi, ti, 0)),
        out_shape=jax.ShapeDtypeStruct((b, t, gw), BF16),
        scratch_shapes=[
            pltpu.VMEM((GDN_HEADS, GDN_HEAD_DIM, GDN_HEAD_DIM), F32),
            pltpu.VMEM((3, 8 + GDN_CHUNK, gw), F32),
            pltpu.VMEM((tb, gw), BF16),
            pltpu.VMEM((tb, gw), BF16),
            pltpu.VMEM((tb, gw), BF16),
            pltpu.VMEM((tb, GDN_HEADS), F32),
            pltpu.VMEM((n_chunk, GDN_HEADS, GDN_CHUNK), F32),
            pltpu.VMEM((tb, GDN_HEADS), F32),
            pltpu.VMEM((n_chunk, GDN_HEADS, GDN_CHUNK), F32),
            pltpu.VMEM((n_inst * LST_PITCH, HEAD_LANES), F32),
            pltpu.VMEM((GDN_CHUNK, GDN_CHUNK, n_inst), F32),
            pltpu.VMEM((GDN_CHUNK, GDN_CHUNK, n_inst), F32),
            pltpu.VMEM((n_inst, GDN_CHUNK, GDN_CHUNK), BF16),
        ],
        compiler_params=pltpu.CompilerParams(
            dimension_semantics=("parallel", "arbitrary"), vmem_limit_bytes=GDN_VMEM_LIMIT),
        name="gdn",
    )(proj3, proj3, proj3, proj3, ab3, abt2, conv_w,
      a_log.reshape(1, GDN_HEADS), dt_bias.reshape(1, GDN_HEADS),
      a_log.reshape(GDN_HEADS, 1), dt_bias.reshape(GDN_HEADS, 1),
      gdn_norm_w.reshape(1, GDN_HEAD_DIM))


def _outproj_kernel(oda_ref, ogdn_ref, w_ref, x_ref, fw_ref, o_ref, *, final):
    acc = _dot(oda_ref[...], w_ref[0:DA_WIDTH, :]) + _dot(ogdn_ref[...], w_ref[DA_WIDTH:DA_WIDTH + GDN_WIDTH, :])
    y = x_ref[...] + acc
    if final:
        y = y * lax.rsqrt(jnp.mean(y * y, axis=-1, keepdims=True) + RMS_EPS) * fw_ref[...]
    o_ref[...] = y


def _outproj(o_da2, o_gdn2, w_out, x2, final_w, final):
    m = x2.shape[0]
    tm = OUTPROJ_TM
    return pl.pallas_call(
        functools.partial(_outproj_kernel, final=final),
        grid=(m // tm,),
        in_specs=[
            pl.BlockSpec((tm, DA_WIDTH), lambda i: (i, 0)),
            pl.BlockSpec((tm, GDN_WIDTH), lambda i: (i, 0)),
            pl.BlockSpec((DA_WIDTH + GDN_WIDTH, D_MODEL), lambda i: (0, 0)),
            pl.BlockSpec((tm, D_MODEL), lambda i: (i, 0)),
            pl.BlockSpec((1, D_MODEL), lambda i: (0, 0)),
        ],
        out_specs=pl.BlockSpec((tm, D_MODEL), lambda i: (i, 0)),
        out_shape=jax.ShapeDtypeStruct((m, D_MODEL), F32),
        compiler_params=pltpu.CompilerParams(
            dimension_semantics=("parallel",), vmem_limit_bytes=VMEM_LIMIT),
        name="outproj",
    )(o_da2, o_gdn2, w_out, x2, final_w)


def kernel(x, norm_w, w_in, w_out, lambda_q1, lambda_k1, lambda_q2, lambda_k2, da_subln_w, rel_bias,
           conv_w, a_log, dt_bias, gdn_norm_w, final_norm_w):
    b, t, d = x.shape
    m = b * t
    x2 = x.reshape(m, d)
    near_bias = _near_bias(rel_bias)
    final_w = final_norm_w.reshape(1, d)
    for l in range(DEPTH):
        lambda_init = 0.8 - 0.6 * math.exp(-0.3 * l)
        w_main = w_in[l, :, 0:MAIN_COLS].astype(BF16)
        w_ab = w_in[l, :, MAIN_COLS:MAIN_COLS + AB_COLS].astype(BF16)
        proj, ab, abt = _inproj(x2, norm_w[l].reshape(1, d), w_main, w_ab, w_ab.T)
        proj3 = proj.reshape(b, t, MAIN_COLS)
        lam_params = jnp.stack([lambda_q1[l], lambda_k1[l], lambda_q2[l], lambda_k2[l]])
        o_da = _attention(proj3, near_bias, lam_params, da_subln_w[l].reshape(1, 2 * DA_HEAD_DIM), lambda_init)
        o_gdn = _gdn(proj3, ab.reshape(b, t, AB_COLS), abt, conv_w[l], a_log[l], dt_bias[l], gdn_norm_w[l])
        x2 = _outproj(o_da.reshape(m, DA_WIDTH), o_gdn.reshape(m, GDN_WIDTH), w_out[l].astype(BF16),
                      x2, final_w, final=(l == DEPTH - 1))
    return x2.reshape(b, t, d)
```

```python
import functools
import math

import jax
import jax.numpy as jnp
from jax import lax
from jax.experimental import pallas as pl
from jax.experimental.pallas import tpu as pltpu

F32 = jnp.float32
BF16 = jnp.bfloat16

D_MODEL = 2048
DEPTH = 2
DA_HEADS = 8
DA_HEAD_DIM = 64
DA_WIDTH = DA_HEADS * 2 * DA_HEAD_DIM
GDN_HEADS = 8
GDN_HEAD_DIM = 128
GDN_WIDTH = GDN_HEADS * GDN_HEAD_DIM
CONV_K = 4
GDN_CHUNK = 64
REL_BUCKETS = 32
REL_MAX_DIST = 128
REL_MAX_EXACT = REL_BUCKETS // 2
RMS_EPS = 1e-6
NEG_INF = -1e30
MAIN_COLS = 4 * DA_WIDTH + 4 * GDN_WIDTH
AB_COLS = 2 * GDN_HEADS
HEAD_LANES = 128
BF16_ROWS = 16
LOG2_E = math.log2(math.e)

INPROJ_TM = 1024
INPROJ_TN = 512
ATTN_TQ = 512
ATTN_TK = 512
ATTN_LANE_TILE = 256
GDN_TB = 1024
OUTPROJ_TM = 256
VMEM_LIMIT = 48 * 1024 * 1024
GDN_VMEM_LIMIT = 56 * 1024 * 1024

NT_DIMS = (((1,), (1,)), ((), ()))


def _sigmoid(x):
    return 1.0 / (1.0 + jnp.exp(-x))


def _silu(x):
    h = 0.5 * x
    return h + h * jnp.tanh(h)


def _softplus(x):
    return jnp.maximum(x, 0.0) + jnp.log(1.0 + jnp.exp(-jnp.abs(x)))


def _dot(a, b):
    return jnp.dot(a, b, preferred_element_type=F32)


def _split_hi_lo(a):
    hi = a.astype(BF16)
    lo = (a - hi.astype(F32)).astype(BF16)
    return hi, lo


def _dot_split(a, b):
    a_hi, a_lo = _split_hi_lo(a)
    b_hi, b_lo = _split_hi_lo(b)
    return _dot(a_hi, b_hi) + _dot(a_lo, b_hi) + _dot(a_hi, b_lo)


def _inproj_kernel(x_ref, nw_ref, w_ref, wab_ref, proj_ref, ab_ref, abt_ref, h_ref):
    @pl.when(pl.program_id(1) == 0)
    def _():
        x = x_ref[...]
        ms = jnp.mean(x * x, axis=-1, keepdims=True)
        h = (x * lax.rsqrt(ms + RMS_EPS) * nw_ref[...]).astype(BF16)
        h_ref[...] = h
        ab = _dot(h, wab_ref[...])
        ab_ref[...] = ab[:, 0:AB_COLS]
        abt_ref[...] = ab.T[0:AB_COLS, :]

    proj_ref[...] = _dot(h_ref[...], w_ref[...]).astype(BF16)


def _inproj(x2, norm_w, w_in_bf16, layer, w_ab_padded):
    m = x2.shape[0]
    tm, tn = INPROJ_TM, INPROJ_TN
    return pl.pallas_call(
        _inproj_kernel,
        grid=(m // tm, MAIN_COLS // tn),
        in_specs=[
            pl.BlockSpec((tm, D_MODEL), lambda i, j: (i, 0)),
            pl.BlockSpec((1, D_MODEL), lambda i, j: (0, 0)),
            pl.BlockSpec((None, D_MODEL, tn), lambda i, j: (layer, 0, j)),
            pl.BlockSpec((D_MODEL, HEAD_LANES), lambda i, j: (0, 0)),
        ],
        out_specs=[
            pl.BlockSpec((tm, tn), lambda i, j: (i, j)),
            pl.BlockSpec((tm, AB_COLS), lambda i, j: (i, 0)),
            pl.BlockSpec((AB_COLS, tm), lambda i, j: (0, i)),
        ],
        out_shape=[
            jax.ShapeDtypeStruct((m, MAIN_COLS), BF16),
            jax.ShapeDtypeStruct((m, AB_COLS), F32),
            jax.ShapeDtypeStruct((AB_COLS, m), F32),
        ],
        scratch_shapes=[pltpu.VMEM((tm, D_MODEL), BF16)],
        compiler_params=pltpu.CompilerParams(
            dimension_semantics=("parallel", "arbitrary"), vmem_limit_bytes=VMEM_LIMIT),
        name="inproj",
    )(x2, norm_w, w_in_bf16, w_ab_padded)


def _bias_kernel(rb_ref, o_ref, *, tk):
    h = pl.program_id(0)
    shape = o_ref.shape
    c = lax.broadcasted_iota(jnp.int32, shape, 0)
    r = lax.broadcasted_iota(jnp.int32, shape, 1)
    dist = r - c + tk
    n = jnp.maximum(dist, 0)
    nf = jnp.maximum(n, REL_MAX_EXACT).astype(F32)
    large = REL_MAX_EXACT + (jnp.log(nf / REL_MAX_EXACT) / math.log(REL_MAX_DIST / REL_MAX_EXACT)
                             * (REL_BUCKETS - REL_MAX_EXACT)).astype(jnp.int32)
    large = jnp.minimum(large, REL_BUCKETS - 1)
    bucket = jnp.where(n < REL_MAX_EXACT, n, large)
    far = rb_ref[REL_BUCKETS - 1, h]
    val = jnp.zeros(shape, F32)
    for b in range(REL_BUCKETS - 1):
        val = jnp.where(bucket == b, rb_ref[b, h] - far, val)
    o_ref[...] = jnp.where(dist >= 0, val * LOG2_E, NEG_INF)


def _near_bias(rel_bias):
    tq, tk = ATTN_TQ, ATTN_TK
    return pl.pallas_call(
        functools.partial(_bias_kernel, tk=tk),
        grid=(DA_HEADS,),
        in_specs=[pl.BlockSpec(memory_space=pltpu.SMEM)],
        out_specs=pl.BlockSpec((None, 2 * tk, tq), lambda h: (h, 0, 0)),
        out_shape=jax.ShapeDtypeStruct((DA_HEADS, 2 * tk, tq), F32),
        name="near_bias",
    )(rel_bias)


def _attn_kernel(q_ref, k_ref, v_ref, g_ref, bias_ref, lam_ref, sw_ref, o_ref,
                 qz_ref, vt_ref, m_ref, acc_ref, sa_ref, sb_ref, *, tq, tk, lambda_init):
    i = pl.program_id(2)
    hl = HEAD_LANES
    lt = ATTN_LANE_TILE
    n_tiles = 2 * tq // lt

    @pl.when(i == 0)
    def _():
        ones = jnp.ones((vt_ref.shape[1] - hl, tk), BF16)
        for j in range(vt_ref.shape[0]):
            vt_ref[j, 0:hl, :] = v_ref[j * tk:(j + 1) * tk, :].astype(F32).T.astype(BF16)
            vt_ref[j, hl:, :] = ones

    q = q_ref[...] * (DA_HEAD_DIM ** -0.5 * LOG2_E)
    lane = lax.broadcasted_iota(jnp.int32, q.shape, 1)
    zero = jnp.zeros_like(q)
    qz_ref[0:tq, :] = jnp.where(lane < DA_HEAD_DIM, q, zero)
    qz_ref[tq:2 * tq, :] = jnp.where(lane >= DA_HEAD_DIM, q, zero)
    m_ref[...] = jnp.full(m_ref.shape, NEG_INF, F32)
    acc_ref[...] = jnp.zeros(acc_ref.shape, F32)

    tiles = range(n_tiles)
    cols = [slice(t * lt, (t + 1) * lt) for t in tiles]

    def scores(j, s_ref):
        kj = k_ref[pl.ds(pl.multiple_of(j * tk, tk), tk), :]
        for t in tiles:
            s_ref[t] = lax.dot_general(kj, qz_ref[cols[t], :], NT_DIMS, preferred_element_type=F32)

    def softmax_pv(j, s_ref, bias_row0, bias_valid=None):
        vt = vt_ref[j]
        s = [s_ref[t] for t in tiles]
        if bias_row0 is not None:
            bias = [bias_ref[bias_row0:bias_row0 + tk, cols[t]] for t in range(tq // lt)]
            if bias_valid is not None:
                bias = [jnp.where(bias_valid, b, NEG_INF) for b in bias]
            s = [s[t] + bias[t % (tq // lt)] for t in tiles]
        m_prev = [m_ref[:, cols[t]] for t in tiles]
        m_new = [jnp.maximum(m_prev[t], jnp.max(s[t], axis=0, keepdims=True)) for t in tiles]
        p = [jnp.exp2(s[t] - m_new[t]).astype(BF16) for t in tiles]
        pv = [_dot(vt, p[t]) for t in tiles]
        for t in tiles:
            acc_ref[:, cols[t]] = jnp.exp2(m_prev[t] - m_new[t]) * acc_ref[:, cols[t]] + pv[t]
            m_ref[:, cols[t]] = m_new[t]

    n_far = jnp.maximum(i - 1, 0)
    odd = n_far % 2

    @pl.when(odd == 1)
    def _():
        scores(0, sa_ref)
        softmax_pv(0, sa_ref, None)

    scores(odd, sa_ref)

    def far_pair(jj, carry):
        j = odd + 2 * jj
        scores(j + 1, sb_ref)
        softmax_pv(j, sa_ref, None)
        scores(j + 2, sa_ref)
        softmax_pv(j + 1, sb_ref, None)
        return carry

    lax.fori_loop(0, n_far // 2, far_pair, 0)
    scores(i, sb_ref)
    softmax_pv(n_far, sa_ref, 0, bias_valid=i > 0)
    softmax_pv(i, sb_ref, tk)

    lp = lam_ref[...]
    s1 = jnp.sum(lp[0:1, :] * lp[1:2, :], axis=-1, keepdims=True)
    s2 = jnp.sum(lp[2:3, :] * lp[3:4, :], axis=-1, keepdims=True)
    lam = jnp.exp(s1) - jnp.exp(s2) + lambda_init
    inv_l = 1.0 / acc_ref[hl:hl + 1, :]
    o_t = acc_ref[0:hl, 0:tq] * inv_l[:, 0:tq] - lam * (acc_ref[0:hl, tq:2 * tq] * inv_l[:, tq:2 * tq])
    o = o_t.T
    ms = jnp.mean(o * o, axis=-1, keepdims=True)
    y = o * lax.rsqrt(ms + RMS_EPS) * sw_ref[...] * (1.0 - lambda_init)
    o_ref[...] = (y * _silu(g_ref[...].astype(F32))).astype(BF16)


def _attention(proj3, near_bias, lam_params, subln_w, lambda_init):
    b, t, _ = proj3.shape
    tq, tk = ATTN_TQ, ATTN_TK
    assert tq == tk and t % tq == 0
    hl = HEAD_LANES
    q_blk = DA_WIDTH // hl
    return pl.pallas_call(
        functools.partial(_attn_kernel, tq=tq, tk=tk, lambda_init=lambda_init),
        grid=(b, DA_HEADS, t // tq),
        in_specs=[
            pl.BlockSpec((None, tq, hl), lambda bi, h, i: (bi, i, h)),
            pl.BlockSpec((None, t, hl), lambda bi, h, i: (bi, 0, q_blk + h)),
            pl.BlockSpec((None, t, hl), lambda bi, h, i: (bi, 0, 2 * q_blk + h)),
            pl.BlockSpec((None, tq, hl), lambda bi, h, i: (bi, i, 3 * q_blk + h)),
            pl.BlockSpec((None, 2 * tk, tq), lambda bi, h, i: (h, 0, 0)),
            pl.BlockSpec((4, DA_HEAD_DIM), lambda bi, h, i: (0, 0)),
            pl.BlockSpec((1, hl), lambda bi, h, i: (0, 0)),
        ],
        out_specs=pl.BlockSpec((None, tq, hl), lambda bi, h, i: (bi, i, h)),
        out_shape=jax.ShapeDtypeStruct((b, t, DA_WIDTH), BF16),
        scratch_shapes=[
            pltpu.VMEM((2 * tq, hl), BF16),
            pltpu.VMEM((t // tk, hl + BF16_ROWS, tk), BF16),
            pltpu.VMEM((1, 2 * tq), F32),
            pltpu.VMEM((hl + BF16_ROWS, 2 * tq), F32),
            pltpu.VMEM((2 * tq // ATTN_LANE_TILE, tk, ATTN_LANE_TILE), F32),
            pltpu.VMEM((2 * tq // ATTN_LANE_TILE, tk, ATTN_LANE_TILE), F32),
        ],
        compiler_params=pltpu.CompilerParams(
            dimension_semantics=("parallel", "parallel", "arbitrary"), vmem_limit_bytes=VMEM_LIMIT),
        name="diff_attn",
    )(proj3, proj3, proj3, proj3, near_bias, lam_params, subln_w)


LST_PITCH = 72
SUBLANES = 8
XPOSE_UNROLL = 4


def _unit_lower_inverse_on_lanes(lt_ref, tt_ref, n_inst):
    c_len = GDN_CHUNK
    sub = lax.broadcasted_iota(jnp.int32, (SUBLANES, n_inst), 0)
    zero = jnp.zeros((SUBLANES, n_inst), F32)
    for i in range(c_len):
        n_blk = (i - 1) // SUBLANES + 1 if i > 0 else 0
        acc = [None] * n_blk
        for k in range(i):
            lik = jnp.broadcast_to(lt_ref[i, k:k + 1, :], (SUBLANES, n_inst))
            for jb in range(k // SUBLANES + 1):
                term = lik * tt_ref[k, jb * SUBLANES:(jb + 1) * SUBLANES, :]
                acc[jb] = term if acc[jb] is None else acc[jb] + term
        for jb in range(c_len // SUBLANES):
            val = -acc[jb] if jb < n_blk else zero
            if jb == i // SUBLANES:
                val = jnp.where(sub == i % SUBLANES, 1.0, val)
            tt_ref[i, jb * SUBLANES:(jb + 1) * SUBLANES, :] = val


def _gdn_kernel(qp_ref, kp_ref, vp_ref, z_ref, ab_ref, abt_ref, cw_ref, alog_ref, dtb_ref,
                alogt_ref, dtbt_ref, nw_ref, o_ref,
                s_ref, cbuf_ref, qn_ref, kn_ref, vn_ref, gc_ref, gct_ref, beta_ref, bt_ref,
                lst_ref, lt_ref, tt_ref, qk_ref, *, tb):
    c_len = GDN_CHUNK
    hl = HEAD_LANES
    n_chunk = tb // c_len
    n_inst = n_chunk * GDN_HEADS
    halo = BF16_ROWS

    @pl.when(pl.program_id(1) == 0)
    def _():
        s_ref[...] = jnp.zeros(s_ref.shape, F32)
        cbuf_ref[:, 0:halo, :] = jnp.zeros((3, halo, GDN_WIDTH), BF16)

    sel_r = lax.broadcasted_iota(jnp.int32, ((CONV_K - 1) * c_len, halo + c_len), 0)
    sel_c = lax.broadcasted_iota(jnp.int32, ((CONV_K - 1) * c_len, halo + c_len), 1)
    tap = sel_r // c_len
    shift_mat = (sel_c == sel_r - tap * c_len + halo - (CONV_K - 1) + tap).astype(BF16)

    def conv_chunk(c, carry):
        rows = pl.ds(pl.multiple_of(c * c_len, c_len), c_len)
        for g, (src, dst) in enumerate(((qp_ref, qn_ref), (kp_ref, kn_ref), (vp_ref, vn_ref))):
            xb = src[rows, :]
            cbuf_ref[g, halo:halo + c_len, :] = xb
            w = cw_ref[:, g * GDN_WIDTH:(g + 1) * GDN_WIDTH]
            taps = _dot(shift_mat, cbuf_ref[g])
            y = w[0:1, :] * taps[0:c_len, :]
            y = y + w[1:2, :] * taps[c_len:2 * c_len, :]
            y = y + w[2:3, :] * taps[2 * c_len:3 * c_len, :]
            y = y + w[3:4, :] * xb.astype(F32)
            cbuf_ref[g, 0:halo, :] = xb[c_len - halo:c_len, :]
            y = _silu(y)
            for h in range(GDN_HEADS):
                cs = slice(h * hl, (h + 1) * hl)
                yh = y[:, cs]
                if g < 2:
                    yh = yh * lax.rsqrt(jnp.sum(yh * yh, axis=-1, keepdims=True) + RMS_EPS)
                if g == 0:
                    yh = yh * (GDN_HEAD_DIM ** -0.5)
                dst[rows, cs] = yh.astype(BF16)
        return carry

    lax.fori_loop(0, n_chunk, conv_chunk, 0)

    ab = ab_ref[...]
    beta_ref[...] = _sigmoid(ab[:, 0:GDN_HEADS])
    g_raw = -jnp.exp(alog_ref[...]) * _softplus(ab[:, GDN_HEADS:AB_COLS] + dtb_ref[...])
    abt = abt_ref[...]
    beta_t = _sigmoid(abt[0:GDN_HEADS, :])
    g_raw_t = -jnp.exp(alogt_ref[...]) * _softplus(abt[GDN_HEADS:AB_COLS, :] + dtbt_ref[...])
    row = lax.broadcasted_iota(jnp.int32, (c_len, c_len), 0)
    col = lax.broadcasted_iota(jnp.int32, (c_len, c_len), 1)
    tril = row >= col
    strict = row > col
    tril_f = tril.astype(F32)
    triu_f = (row <= col).astype(F32)
    for c in range(n_chunk):
        gc_ref[c * c_len:(c + 1) * c_len, :] = _dot_split(tril_f, g_raw[c * c_len:(c + 1) * c_len, :])
        gct_ref[c] = _dot_split(g_raw_t[:, c * c_len:(c + 1) * c_len], triu_f)
        bt_ref[c] = beta_t[:, c * c_len:(c + 1) * c_len]

    heads = range(GDN_HEADS)
    lanes = [slice(h * hl, (h + 1) * hl) for h in heads]

    def intra_chunk(c, carry):
        rows = pl.ds(pl.multiple_of(c * c_len, c_len), c_len)
        gct = gct_ref[c]
        gcc = gc_ref[rows, :]
        betac = beta_ref[rows, :]
        ks = [kn_ref[rows, lanes[h]] for h in heads]
        kq = [jnp.concatenate([ks[h], qn_ref[rows, lanes[h]]], axis=0) for h in heads]
        kk = [lax.dot_general(kq[h], ks[h], NT_DIMS, preferred_element_type=F32) for h in heads]
        decay = [jnp.where(tril, jnp.exp(jnp.where(tril, gcc[:, h:h + 1] - gct[h:h + 1, :], 0.0)), 0.0)
                 for h in heads]
        pad = jnp.zeros((c_len, hl - c_len), F32)
        for h in heads:
            l_strict = jnp.where(strict, kk[h][0:c_len, :] * decay[h] * betac[:, h:h + 1], 0.0)
            base = pl.multiple_of((c * GDN_HEADS + h) * LST_PITCH, SUBLANES)
            lst_ref[pl.ds(base, c_len), :] = jnp.concatenate([l_strict, pad], axis=1)
            qk_ref[c * GDN_HEADS + h] = (kk[h][c_len:2 * c_len, :] * decay[h]).astype(BF16)
        return carry

    lax.fori_loop(0, n_chunk, intra_chunk, 0)

    def to_lanes(ii, carry):
        for i in [ii * XPOSE_UNROLL + u for u in range(XPOSE_UNROLL)]:
            x = lst_ref[pl.ds(i, n_inst, stride=LST_PITCH), :]
            lt_ref[i] = x.T[0:c_len, :]
        return carry

    lax.fori_loop(0, c_len // XPOSE_UNROLL, to_lanes, 0)
    _unit_lower_inverse_on_lanes(lt_ref, tt_ref, n_inst)

    def from_lanes(ii, carry):
        for i in [ii * XPOSE_UNROLL + u for u in range(XPOSE_UNROLL)]:
            t = tt_ref[i]
            lst_ref[pl.ds(i, n_inst, stride=LST_PITCH), :] = jnp.concatenate([t, jnp.zeros_like(t)], axis=0).T
        return carry

    lax.fori_loop(0, c_len // XPOSE_UNROLL, from_lanes, 0)

    def scan_chunk(c, carry):
        rows = pl.ds(pl.multiple_of(c * c_len, c_len), c_len)
        gct = gct_ref[c]
        gcc = gc_ref[rows, :]
        bct = bt_ref[c]
        g_row = [gct[h:h + 1, :] for h in heads]
        g_last = [g_row[h][:, c_len - 1:c_len] for h in heads]
        ks = [kn_ref[rows, lanes[h]] for h in heads]
        a_inv = [lst_ref[pl.ds(pl.multiple_of((c * GDN_HEADS + h) * LST_PITCH, SUBLANES), c_len), 0:c_len]
                 for h in heads]
        b_row = [bct[h:h + 1, :] for h in heads]
        u = [_dot((a_inv[h] * b_row[h]).astype(BF16), vn_ref[rows, lanes[h]]) for h in heads]
        w = [_dot((a_inv[h] * (b_row[h] * jnp.exp(g_row[h]))).astype(BF16), ks[h]) for h in heads]
        s_old = [s_ref[h] for h in heads]
        ws = [_dot(jnp.concatenate([w[h].astype(BF16), qn_ref[rows, lanes[h]]], axis=0), s_old[h].astype(BF16))
              for h in heads]
        v_new = [(u[h] - ws[h][0:c_len, :]).astype(BF16) for h in heads]
        kd_t = [(ks[h].astype(F32).T * jnp.exp(g_last[h] - g_row[h])).astype(BF16) for h in heads]
        for h in heads:
            s_ref[h] = s_old[h] * jnp.exp(g_last[h]) + _dot(kd_t[h], v_new[h])
        o = [jnp.exp(gcc[:, h:h + 1]) * ws[h][c_len:2 * c_len, :]
             + _dot(qk_ref[c * GDN_HEADS + h], v_new[h]) for h in heads]
        for h in heads:
            on = o[h] * lax.rsqrt(jnp.mean(o[h] * o[h], axis=-1, keepdims=True) + RMS_EPS) * nw_ref[...]
            zc = z_ref[rows, lanes[h]].astype(F32)
            o_ref[rows, lanes[h]] = (on * _silu(zc)).astype(BF16)
        return carry

    lax.fori_loop(0, n_chunk, scan_chunk, 0)


def _gdn(proj3, ab3, abt2, conv_w, a_log, dt_bias, gdn_norm_w):
    b, t, _ = proj3.shape
    tb = GDN_TB
    gw = GDN_WIDTH
    first = 4 * DA_WIDTH // gw
    n_chunk = tb // GDN_CHUNK
    n_inst = n_chunk * GDN_HEADS
    assert n_inst == HEAD_LANES, "one (chunk, head) matrix per lane"
    return pl.pallas_call(
        functools.partial(_gdn_kernel, tb=tb),
        grid=(b, t // tb),
        in_specs=[
            pl.BlockSpec((None, tb, gw), lambda bi, ti: (bi, ti, first)),
            pl.BlockSpec((None, tb, gw), lambda bi, ti: (bi, ti, first + 1)),
            pl.BlockSpec((None, tb, gw), lambda bi, ti: (bi, ti, first + 2)),
            pl.BlockSpec((None, tb, gw), lambda bi, ti: (bi, ti, first + 3)),
            pl.BlockSpec((None, tb, AB_COLS), lambda bi, ti: (bi, ti, 0)),
            pl.BlockSpec((AB_COLS, tb), lambda bi, ti: (0, bi * (t // tb) + ti)),
            pl.BlockSpec((CONV_K, 3 * gw), lambda bi, ti: (0, 0)),
            pl.BlockSpec((1, GDN_HEADS), lambda bi, ti: (0, 0)),
            pl.BlockSpec((1, GDN_HEADS), lambda bi, ti: (0, 0)),
            pl.BlockSpec((GDN_HEADS, 1), lambda bi, ti: (0, 0)),
            pl.BlockSpec((GDN_HEADS, 1), lambda bi, ti: (0, 0)),
            pl.BlockSpec((1, GDN_HEAD_DIM), lambda bi, ti: (0, 0)),
        ],
        out_specs=pl.BlockSpec((None, tb, gw), lambda bi, ti: (bi, ti, 0)),
        out_shape=jax.ShapeDtypeStruct((b, t, gw), BF16),
        scratch_shapes=[
            pltpu.VMEM((GDN_HEADS, GDN_HEAD_DIM, GDN_HEAD_DIM), F32),
            pltpu.VMEM((3, BF16_ROWS + GDN_CHUNK, gw), BF16),
            pltpu.VMEM((tb, gw), BF16),
            pltpu.VMEM((tb, gw), BF16),
            pltpu.VMEM((tb, gw), BF16),
            pltpu.VMEM((tb, GDN_HEADS), F32),
            pltpu.VMEM((n_chunk, GDN_HEADS, GDN_CHUNK), F32),
            pltpu.VMEM((tb, GDN_HEADS), F32),
            pltpu.VMEM((n_chunk, GDN_HEADS, GDN_CHUNK), F32),
            pltpu.VMEM((n_inst * LST_PITCH, HEAD_LANES), F32),
            pltpu.VMEM((GDN_CHUNK, GDN_CHUNK, n_inst), F32),
            pltpu.VMEM((GDN_CHUNK, GDN_CHUNK, n_inst), F32),
            pltpu.VMEM((n_inst, GDN_CHUNK, GDN_CHUNK), BF16),
        ],
        compiler_params=pltpu.CompilerParams(
            dimension_semantics=("parallel", "arbitrary"), vmem_limit_bytes=GDN_VMEM_LIMIT),
        name="gdn",
    )(proj3, proj3, proj3, proj3, ab3, abt2, conv_w,
      a_log.reshape(1, GDN_HEADS), dt_bias.reshape(1, GDN_HEADS),
      a_log.reshape(GDN_HEADS, 1), dt_bias.reshape(GDN_HEADS, 1),
      gdn_norm_w.reshape(1, GDN_HEAD_DIM))


def _outproj_kernel(oda_ref, ogdn_ref, w_ref, x_ref, fw_ref, o_ref, *, final):
    acc = _dot(oda_ref[...], w_ref[0:DA_WIDTH, :]) + _dot(ogdn_ref[...], w_ref[DA_WIDTH:DA_WIDTH + GDN_WIDTH, :])
    y = x_ref[...] + acc
    if final:
        y = y * lax.rsqrt(jnp.mean(y * y, axis=-1, keepdims=True) + RMS_EPS) * fw_ref[...]
    o_ref[...] = y


def _outproj(o_da2, o_gdn2, w_out_bf16, layer, x2, final_w, final):
    m = x2.shape[0]
    tm = OUTPROJ_TM
    return pl.pallas_call(
        functools.partial(_outproj_kernel, final=final),
        grid=(m // tm,),
        in_specs=[
            pl.BlockSpec((tm, DA_WIDTH), lambda i: (i, 0)),
            pl.BlockSpec((tm, GDN_WIDTH), lambda i: (i, 0)),
            pl.BlockSpec((None, DA_WIDTH + GDN_WIDTH, D_MODEL), lambda i: (layer, 0, 0)),
            pl.BlockSpec((tm, D_MODEL), lambda i: (i, 0)),
            pl.BlockSpec((1, D_MODEL), lambda i: (0, 0)),
        ],
        out_specs=pl.BlockSpec((tm, D_MODEL), lambda i: (i, 0)),
        out_shape=jax.ShapeDtypeStruct((m, D_MODEL), F32),
        compiler_params=pltpu.CompilerParams(
            dimension_semantics=("parallel",), vmem_limit_bytes=VMEM_LIMIT),
        name="outproj",
    )(o_da2, o_gdn2, w_out_bf16, x2, final_w)


def kernel(x, norm_w, w_in, w_out, lambda_q1, lambda_k1, lambda_q2, lambda_k2, da_subln_w, rel_bias,
           conv_w, a_log, dt_bias, gdn_norm_w, final_norm_w):
    b, t, d = x.shape
    m = b * t
    x2 = x.reshape(m, d)
    near_bias = _near_bias(rel_bias)
    final_w = final_norm_w.reshape(1, d)
    w_in_bf16 = w_in.astype(BF16)
    w_out_bf16 = w_out.astype(BF16)
    w_ab_padded = jnp.pad(w_in[:, :, MAIN_COLS:MAIN_COLS + AB_COLS].astype(BF16),
                          ((0, 0), (0, 0), (0, HEAD_LANES - AB_COLS)))
    for l in range(DEPTH):
        lambda_init = 0.8 - 0.6 * math.exp(-0.3 * l)
        proj, ab, abt = _inproj(x2, norm_w[l].reshape(1, d), w_in_bf16, l, w_ab_padded[l])
        proj3 = proj.reshape(b, t, MAIN_COLS)
        lam_params = jnp.stack([lambda_q1[l], lambda_k1[l], lambda_q2[l], lambda_k2[l]])
        o_da = _attention(proj3, near_bias, lam_params, da_subln_w[l].reshape(1, 2 * DA_HEAD_DIM), lambda_init)
        o_gdn = _gdn(proj3, ab.reshape(b, t, AB_COLS), abt, conv_w[l], a_log[l], dt_bias[l], gdn_norm_w[l])
        x2 = _outproj(o_da.reshape(m, DA_WIDTH), o_gdn.reshape(m, GDN_WIDTH), w_out_bf16, l,
                      x2, final_w, final=(l == DEPTH - 1))
    return x2.reshape(b, t, d)
```

```python
import functools
import math

import jax
import jax.numpy as jnp
from jax import lax
from jax.experimental import pallas as pl
from jax.experimental.pallas import tpu as pltpu

F32 = jnp.float32
BF16 = jnp.bfloat16

D_MODEL = 2048
DEPTH = 2
DA_HEADS = 8
DA_HEAD_DIM = 64
DA_WIDTH = DA_HEADS * 2 * DA_HEAD_DIM
GDN_HEADS = 8
GDN_HEAD_DIM = 128
GDN_WIDTH = GDN_HEADS * GDN_HEAD_DIM
CONV_K = 4
GDN_CHUNK = 64
REL_BUCKETS = 32
REL_MAX_DIST = 128
REL_MAX_EXACT = REL_BUCKETS // 2
RMS_EPS = 1e-6
NEG_INF = -1e30
MAIN_COLS = 4 * DA_WIDTH + 4 * GDN_WIDTH
AB_COLS = 2 * GDN_HEADS
HEAD_LANES = 128
BF16_ROWS = 16
LOG2_E = math.log2(math.e)

INPROJ_TM = 1024
INPROJ_TN = 512
ATTN_TQ = 512
ATTN_TK = 512
ATTN_LANE_TILE = 256
ATTN_HEADS_PER_STEP = 2
GDN_TB = 1024
OUTPROJ_TM = 256
VMEM_LIMIT = 48 * 1024 * 1024
GDN_VMEM_LIMIT = 56 * 1024 * 1024

NT_DIMS = (((1,), (1,)), ((), ()))


def _sigmoid(x):
    return 1.0 / (1.0 + jnp.exp(-x))


def _silu(x):
    h = 0.5 * x
    return h + h * jnp.tanh(h)


def _softplus(x):
    return jnp.maximum(x, 0.0) + jnp.log(1.0 + jnp.exp(-jnp.abs(x)))


def _dot(a, b):
    return jnp.dot(a, b, preferred_element_type=F32)


def _split_hi_lo(a):
    hi = a.astype(BF16)
    lo = (a - hi.astype(F32)).astype(BF16)
    return hi, lo


def _dot_split(a, b):
    a_hi, a_lo = _split_hi_lo(a)
    b_hi, b_lo = _split_hi_lo(b)
    return _dot(a_hi, b_hi) + _dot(a_lo, b_hi) + _dot(a_hi, b_lo)


def _inproj_kernel(x_ref, nw_ref, w_ref, wab_ref, proj_ref, ab_ref, abt_ref, h_ref):
    @pl.when(pl.program_id(1) == 0)
    def _():
        x = x_ref[...]
        ms = jnp.mean(x * x, axis=-1, keepdims=True)
        h = (x * lax.rsqrt(ms + RMS_EPS) * nw_ref[...]).astype(BF16)
        h_ref[...] = h
        ab = _dot(h, wab_ref[...])
        ab_ref[...] = ab[:, 0:AB_COLS]
        abt_ref[...] = ab.T[0:AB_COLS, :]

    proj_ref[...] = _dot(h_ref[...], w_ref[...].astype(BF16)).astype(BF16)


def _inproj(x2, norm_w, w_in, layer, w_ab_padded):
    m = x2.shape[0]
    tm, tn = INPROJ_TM, INPROJ_TN
    return pl.pallas_call(
        _inproj_kernel,
        grid=(m // tm, MAIN_COLS // tn),
        in_specs=[
            pl.BlockSpec((tm, D_MODEL), lambda i, j: (i, 0)),
            pl.BlockSpec((1, D_MODEL), lambda i, j: (0, 0)),
            pl.BlockSpec((None, D_MODEL, tn), lambda i, j: (layer, 0, j)),
            pl.BlockSpec((D_MODEL, HEAD_LANES), lambda i, j: (0, 0)),
        ],
        out_specs=[
            pl.BlockSpec((tm, tn), lambda i, j: (i, j)),
            pl.BlockSpec((tm, AB_COLS), lambda i, j: (i, 0)),
            pl.BlockSpec((AB_COLS, tm), lambda i, j: (0, i)),
        ],
        out_shape=[
            jax.ShapeDtypeStruct((m, MAIN_COLS), BF16),
            jax.ShapeDtypeStruct((m, AB_COLS), F32),
            jax.ShapeDtypeStruct((AB_COLS, m), F32),
        ],
        scratch_shapes=[pltpu.VMEM((tm, D_MODEL), BF16)],
        compiler_params=pltpu.CompilerParams(
            dimension_semantics=("parallel", "arbitrary"), vmem_limit_bytes=VMEM_LIMIT),
        name="inproj",
    )(x2, norm_w, w_in, w_ab_padded)


def _bias_kernel(rb_ref, o_ref, *, tk):
    h = pl.program_id(0)
    shape = o_ref.shape
    c = lax.broadcasted_iota(jnp.int32, shape, 0)
    r = lax.broadcasted_iota(jnp.int32, shape, 1)
    dist = r - c + tk
    n = jnp.maximum(dist, 0)
    nf = jnp.maximum(n, REL_MAX_EXACT).astype(F32)
    large = REL_MAX_EXACT + (jnp.log(nf / REL_MAX_EXACT) / math.log(REL_MAX_DIST / REL_MAX_EXACT)
                             * (REL_BUCKETS - REL_MAX_EXACT)).astype(jnp.int32)
    large = jnp.minimum(large, REL_BUCKETS - 1)
    bucket = jnp.where(n < REL_MAX_EXACT, n, large)
    far = rb_ref[REL_BUCKETS - 1, h]
    val = jnp.zeros(shape, F32)
    for b in range(REL_BUCKETS - 1):
        val = jnp.where(bucket == b, rb_ref[b, h] - far, val)
    o_ref[...] = jnp.where(dist >= 0, val * LOG2_E, NEG_INF)


def _near_bias(rel_bias):
    tq, tk = ATTN_TQ, ATTN_TK
    return pl.pallas_call(
        functools.partial(_bias_kernel, tk=tk),
        grid=(DA_HEADS,),
        in_specs=[pl.BlockSpec(memory_space=pltpu.SMEM)],
        out_specs=pl.BlockSpec((None, 2 * tk, tq), lambda h: (h, 0, 0)),
        out_shape=jax.ShapeDtypeStruct((DA_HEADS, 2 * tk, tq), F32),
        name="near_bias",
    )(rel_bias)


def _attn_kernel(q_ref, k_ref, v_ref, g_ref, bias_ref, lam_ref, sw_ref, o_ref,
                 qz_ref, vt_ref, m_ref, acc_ref, sa_ref, sb_ref, *, tq, tk, lambda_init):
    i = pl.program_id(2)
    hl = HEAD_LANES
    lt = ATTN_LANE_TILE
    hp = ATTN_HEADS_PER_STEP
    n_tiles = 2 * tq // lt
    q_tiles = tq // lt
    head_cols = [slice(hd * hl, (hd + 1) * hl) for hd in range(hp)]

    @pl.when(i == 0)
    def _():
        ones = jnp.ones((vt_ref.shape[2] - hl, tk), BF16)
        for hd in range(hp):
            for j in range(vt_ref.shape[1]):
                vt_ref[hd, j, 0:hl, :] = v_ref[j * tk:(j + 1) * tk, head_cols[hd]].astype(F32).T.astype(BF16)
                vt_ref[hd, j, hl:, :] = ones

    for hd in range(hp):
        q = q_ref[:, head_cols[hd]] * (DA_HEAD_DIM ** -0.5 * LOG2_E)
        lane = lax.broadcasted_iota(jnp.int32, q.shape, 1)
        zero = jnp.zeros_like(q)
        qz_ref[hd, 0:tq, :] = jnp.where(lane < DA_HEAD_DIM, q, zero)
        qz_ref[hd, tq:2 * tq, :] = jnp.where(lane >= DA_HEAD_DIM, q, zero)
    m_ref[...] = jnp.full(m_ref.shape, NEG_INF, F32)
    acc_ref[...] = jnp.zeros(acc_ref.shape, F32)

    units = [(hd, t) for hd in range(hp) for t in range(n_tiles)]
    cols = [slice(t * lt, (t + 1) * lt) for t in range(n_tiles)]

    def keys_used(t, diag):
        return (t % q_tiles + 1) * lt if diag else tk

    def scores(j, s_ref, diag=False):
        row0 = pl.multiple_of(j * tk, tk)
        for u, (hd, t) in enumerate(units):
            nk = keys_used(t, diag)
            kj = k_ref[pl.ds(row0, nk), head_cols[hd]]
            s_ref[u, 0:nk, :] = lax.dot_general(kj, qz_ref[hd, cols[t], :], NT_DIMS, preferred_element_type=F32)

    def softmax_pv(j, s_ref, bias_row0, bias_valid=None, diag=False):
        nk = [keys_used(t, diag) for _, t in units]
        s = [s_ref[u, 0:nk[u], :] for u in range(len(units))]
        if bias_row0 is not None:
            bias = [bias_ref[hd, bias_row0:bias_row0 + nk[u], cols[t % q_tiles]] for u, (hd, t) in enumerate(units)]
            if bias_valid is not None:
                bias = [jnp.where(bias_valid, b, NEG_INF) for b in bias]
            s = [a + b for a, b in zip(s, bias)]
        m_prev = [m_ref[hd, :, cols[t]] for hd, t in units]
        m_new = [jnp.maximum(mp, jnp.max(a, axis=0, keepdims=True)) for mp, a in zip(m_prev, s)]
        p = [jnp.exp2(a - mn).astype(BF16) for a, mn in zip(s, m_new)]
        pv = [_dot(vt_ref[hd, j, :, 0:nk[u]], p[u]) for u, (hd, t) in enumerate(units)]
        for u, (hd, t) in enumerate(units):
            acc_ref[hd, :, cols[t]] = jnp.exp2(m_prev[u] - m_new[u]) * acc_ref[hd, :, cols[t]] + pv[u]
            m_ref[hd, :, cols[t]] = m_new[u]

    n_far = jnp.maximum(i - 1, 0)
    odd = n_far % 2

    @pl.when(odd == 1)
    def _():
        scores(0, sa_ref)
        softmax_pv(0, sa_ref, None)

    scores(odd, sa_ref)

    def far_pair(jj, carry):
        j = odd + 2 * jj
        scores(j + 1, sb_ref)
        softmax_pv(j, sa_ref, None)
        scores(j + 2, sa_ref)
        softmax_pv(j + 1, sb_ref, None)
        return carry

    lax.fori_loop(0, n_far // 2, far_pair, 0)
    scores(i, sb_ref, diag=True)
    softmax_pv(n_far, sa_ref, 0, bias_valid=i > 0)
    softmax_pv(i, sb_ref, tk, diag=True)

    lp = lam_ref[...]
    s1 = jnp.sum(lp[0:1, :] * lp[1:2, :], axis=-1, keepdims=True)
    s2 = jnp.sum(lp[2:3, :] * lp[3:4, :], axis=-1, keepdims=True)
    lam = jnp.exp(s1) - jnp.exp(s2) + lambda_init
    for hd in range(hp):
        inv_l = 1.0 / acc_ref[hd, hl:hl + 1, :]
        o_t = (acc_ref[hd, 0:hl, 0:tq] * inv_l[:, 0:tq]
               - lam * (acc_ref[hd, 0:hl, tq:2 * tq] * inv_l[:, tq:2 * tq]))
        o = o_t.T
        ms = jnp.mean(o * o, axis=-1, keepdims=True)
        y = o * lax.rsqrt(ms + RMS_EPS) * sw_ref[...] * (1.0 - lambda_init)
        o_ref[:, head_cols[hd]] = (y * _silu(g_ref[:, head_cols[hd]].astype(F32))).astype(BF16)


def _attention(proj3, near_bias, lam_params, subln_w, lambda_init):
    b, t, _ = proj3.shape
    tq, tk = ATTN_TQ, ATTN_TK
    assert tq == tk and t % tq == 0
    hl = HEAD_LANES
    hp = ATTN_HEADS_PER_STEP
    wide = hp * hl
    sect = DA_WIDTH // wide
    n_units = hp * (2 * tq // ATTN_LANE_TILE)
    return pl.pallas_call(
        functools.partial(_attn_kernel, tq=tq, tk=tk, lambda_init=lambda_init),
        grid=(b, DA_HEADS // hp, t // tq),
        in_specs=[
            pl.BlockSpec((None, tq, wide), lambda bi, h, i: (bi, i, h)),
            pl.BlockSpec((None, t, wide), lambda bi, h, i: (bi, 0, sect + h)),
            pl.BlockSpec((None, t, wide), lambda bi, h, i: (bi, 0, 2 * sect + h)),
            pl.BlockSpec((None, tq, wide), lambda bi, h, i: (bi, i, 3 * sect + h)),
            pl.BlockSpec((hp, 2 * tk, tq), lambda bi, h, i: (h, 0, 0)),
            pl.BlockSpec((4, DA_HEAD_DIM), lambda bi, h, i: (0, 0)),
            pl.BlockSpec((1, hl), lambda bi, h, i: (0, 0)),
        ],
        out_specs=pl.BlockSpec((None, tq, wide), lambda bi, h, i: (bi, i, h)),
        out_shape=jax.ShapeDtypeStruct((b, t, DA_WIDTH), BF16),
        scratch_shapes=[
            pltpu.VMEM((hp, 2 * tq, hl), BF16),
            pltpu.VMEM((hp, t // tk, hl + BF16_ROWS, tk), BF16),
            pltpu.VMEM((hp, 1, 2 * tq), F32),
            pltpu.VMEM((hp, hl + BF16_ROWS, 2 * tq), F32),
            pltpu.VMEM((n_units, tk, ATTN_LANE_TILE), F32),
            pltpu.VMEM((n_units, tk, ATTN_LANE_TILE), F32),
        ],
        compiler_params=pltpu.CompilerParams(
            dimension_semantics=("parallel", "parallel", "arbitrary"), vmem_limit_bytes=VMEM_LIMIT),
        name="diff_attn",
    )(proj3, proj3, proj3, proj3, near_bias, lam_params, subln_w)


LST_PITCH = 72
SUBLANES = 8
XPOSE_UNROLL = 4


def _unit_lower_inverse_on_lanes(lt_ref, tt_ref, n_inst):
    c_len = GDN_CHUNK
    sub = lax.broadcasted_iota(jnp.int32, (SUBLANES, n_inst), 0)
    zero = jnp.zeros((SUBLANES, n_inst), F32)
    for i in range(c_len):
        n_blk = (i - 1) // SUBLANES + 1 if i > 0 else 0
        acc = [None] * n_blk
        for k in range(i):
            lik = jnp.broadcast_to(lt_ref[i, k:k + 1, :], (SUBLANES, n_inst))
            for jb in range(k // SUBLANES + 1):
                term = lik * tt_ref[k, jb * SUBLANES:(jb + 1) * SUBLANES, :]
                acc[jb] = term if acc[jb] is None else acc[jb] + term
        for jb in range(c_len // SUBLANES):
            val = -acc[jb] if jb < n_blk else zero
            if jb == i // SUBLANES:
                val = jnp.where(sub == i % SUBLANES, 1.0, val)
            tt_ref[i, jb * SUBLANES:(jb + 1) * SUBLANES, :] = val


def _gdn_kernel(qp_ref, kp_ref, vp_ref, z_ref, ab_ref, abt_ref, cw_ref, alog_ref, dtb_ref,
                alogt_ref, dtbt_ref, nw_ref, o_ref,
                s_ref, cbuf_ref, qn_ref, kn_ref, vn_ref, gc_ref, gct_ref, beta_ref, bt_ref,
                lst_ref, lt_ref, tt_ref, qk_ref, *, tb):
    c_len = GDN_CHUNK
    hl = HEAD_LANES
    n_chunk = tb // c_len
    n_inst = n_chunk * GDN_HEADS
    halo = BF16_ROWS

    @pl.when(pl.program_id(1) == 0)
    def _():
        s_ref[...] = jnp.zeros(s_ref.shape, F32)
        cbuf_ref[:, 0:halo, :] = jnp.zeros((3, halo, GDN_WIDTH), BF16)

    sel_r = lax.broadcasted_iota(jnp.int32, ((CONV_K - 1) * c_len, halo + c_len), 0)
    sel_c = lax.broadcasted_iota(jnp.int32, ((CONV_K - 1) * c_len, halo + c_len), 1)
    tap = sel_r // c_len
    shift_mat = (sel_c == sel_r - tap * c_len + halo - (CONV_K - 1) + tap).astype(BF16)

    def conv_chunk(c, carry):
        rows = pl.ds(pl.multiple_of(c * c_len, c_len), c_len)
        for g, (src, dst) in enumerate(((qp_ref, qn_ref), (kp_ref, kn_ref), (vp_ref, vn_ref))):
            xb = src[rows, :]
            cbuf_ref[g, halo:halo + c_len, :] = xb
            w = cw_ref[:, g * GDN_WIDTH:(g + 1) * GDN_WIDTH]
            taps = _dot(shift_mat, cbuf_ref[g])
            y = w[0:1, :] * taps[0:c_len, :]
            y = y + w[1:2, :] * taps[c_len:2 * c_len, :]
            y = y + w[2:3, :] * taps[2 * c_len:3 * c_len, :]
            y = y + w[3:4, :] * xb.astype(F32)
            cbuf_ref[g, 0:halo, :] = xb[c_len - halo:c_len, :]
            y = _silu(y)
            for h in range(GDN_HEADS):
                cs = slice(h * hl, (h + 1) * hl)
                yh = y[:, cs]
                if g < 2:
                    yh = yh * lax.rsqrt(jnp.sum(yh * yh, axis=-1, keepdims=True) + RMS_EPS)
                if g == 0:
                    yh = yh * (GDN_HEAD_DIM ** -0.5)
                dst[rows, cs] = yh.astype(BF16)
        return carry

    lax.fori_loop(0, n_chunk, conv_chunk, 0)

    ab = ab_ref[...]
    beta_ref[...] = _sigmoid(ab[:, 0:GDN_HEADS])
    g_raw = -jnp.exp(alog_ref[...]) * _softplus(ab[:, GDN_HEADS:AB_COLS] + dtb_ref[...])
    abt = abt_ref[...]
    beta_t = _sigmoid(abt[0:GDN_HEADS, :])
    g_raw_t = -jnp.exp(alogt_ref[...]) * _softplus(abt[GDN_HEADS:AB_COLS, :] + dtbt_ref[...])
    row = lax.broadcasted_iota(jnp.int32, (c_len, c_len), 0)
    col = lax.broadcasted_iota(jnp.int32, (c_len, c_len), 1)
    tril = row >= col
    strict = row > col
    tril_f = tril.astype(F32)
    triu_f = (row <= col).astype(F32)
    for c in range(n_chunk):
        gc_ref[c * c_len:(c + 1) * c_len, :] = _dot_split(tril_f, g_raw[c * c_len:(c + 1) * c_len, :])
        gct_ref[c] = _dot_split(g_raw_t[:, c * c_len:(c + 1) * c_len], triu_f)
        bt_ref[c] = beta_t[:, c * c_len:(c + 1) * c_len]

    heads = range(GDN_HEADS)
    lanes = [slice(h * hl, (h + 1) * hl) for h in heads]

    def intra_chunk(c, carry):
        rows = pl.ds(pl.multiple_of(c * c_len, c_len), c_len)
        gct = gct_ref[c]
        gcc = gc_ref[rows, :]
        betac = beta_ref[rows, :]
        ks = [kn_ref[rows, lanes[h]] for h in heads]
        kq = [jnp.concatenate([ks[h], qn_ref[rows, lanes[h]]], axis=0) for h in heads]
        kk = [lax.dot_general(kq[h], ks[h], NT_DIMS, preferred_element_type=F32) for h in heads]
        decay = [jnp.where(tril, jnp.exp(jnp.where(tril, gcc[:, h:h + 1] - gct[h:h + 1, :], 0.0)), 0.0)
                 for h in heads]
        pad = jnp.zeros((c_len, hl - c_len), F32)
        for h in heads:
            l_strict = jnp.where(strict, kk[h][0:c_len, :] * decay[h] * betac[:, h:h + 1], 0.0)
            base = pl.multiple_of((c * GDN_HEADS + h) * LST_PITCH, SUBLANES)
            lst_ref[pl.ds(base, c_len), :] = jnp.concatenate([l_strict, pad], axis=1)
            qk_ref[c * GDN_HEADS + h] = (kk[h][c_len:2 * c_len, :] * decay[h]).astype(BF16)
        return carry

    lax.fori_loop(0, n_chunk, intra_chunk, 0)

    def to_lanes(ii, carry):
        for i in [ii * XPOSE_UNROLL + u for u in range(XPOSE_UNROLL)]:
            x = lst_ref[pl.ds(i, n_inst, stride=LST_PITCH), :]
            lt_ref[i] = x.T[0:c_len, :]
        return carry

    lax.fori_loop(0, c_len // XPOSE_UNROLL, to_lanes, 0)
    _unit_lower_inverse_on_lanes(lt_ref, tt_ref, n_inst)

    def from_lanes(ii, carry):
        for i in [ii * XPOSE_UNROLL + u for u in range(XPOSE_UNROLL)]:
            t = tt_ref[i]
            lst_ref[pl.ds(i, n_inst, stride=LST_PITCH), :] = jnp.concatenate([t, jnp.zeros_like(t)], axis=0).T
        return carry

    lax.fori_loop(0, c_len // XPOSE_UNROLL, from_lanes, 0)

    def scan_chunk(c, carry):
        rows = pl.ds(pl.multiple_of(c * c_len, c_len), c_len)
        gct = gct_ref[c]
        gcc = gc_ref[rows, :]
        bct = bt_ref[c]
        g_row = [gct[h:h + 1, :] for h in heads]
        g_last = [g_row[h][:, c_len - 1:c_len] for h in heads]
        ks = [kn_ref[rows, lanes[h]] for h in heads]
        a_inv = [lst_ref[pl.ds(pl.multiple_of((c * GDN_HEADS + h) * LST_PITCH, SUBLANES), c_len), 0:c_len]
                 for h in heads]
        b_row = [bct[h:h + 1, :] for h in heads]
        u = [_dot((a_inv[h] * b_row[h]).astype(BF16), vn_ref[rows, lanes[h]]) for h in heads]
        w = [_dot((a_inv[h] * (b_row[h] * jnp.exp(g_row[h]))).astype(BF16), ks[h]) for h in heads]
        s_old = [s_ref[h] for h in heads]
        ws = [_dot(jnp.concatenate([w[h].astype(BF16), qn_ref[rows, lanes[h]]], axis=0), s_old[h].astype(BF16))
              for h in heads]
        v_new = [(u[h] - ws[h][0:c_len, :]).astype(BF16) for h in heads]
        kd_t = [(ks[h].astype(F32).T * jnp.exp(g_last[h] - g_row[h])).astype(BF16) for h in heads]
        for h in heads:
            s_ref[h] = s_old[h] * jnp.exp(g_last[h]) + _dot(kd_t[h], v_new[h])
        o = [jnp.exp(gcc[:, h:h + 1]) * ws[h][c_len:2 * c_len, :]
             + _dot(qk_ref[c * GDN_HEADS + h], v_new[h]) for h in heads]
        for h in heads:
            on = o[h] * lax.rsqrt(jnp.mean(o[h] * o[h], axis=-1, keepdims=True) + RMS_EPS) * nw_ref[...]
            zc = z_ref[rows, lanes[h]].astype(F32)
            o_ref[rows, lanes[h]] = (on * _silu(zc)).astype(BF16)
        return carry

    lax.fori_loop(0, n_chunk, scan_chunk, 0)


def _gdn(proj3, ab3, abt2, conv_w, a_log, dt_bias, gdn_norm_w):
    b, t, _ = proj3.shape
    tb = GDN_TB
    gw = GDN_WIDTH
    first = 4 * DA_WIDTH // gw
    n_chunk = tb // GDN_CHUNK
    n_inst = n_chunk * GDN_HEADS
    assert n_inst == HEAD_LANES, "one (chunk, head) matrix per lane"
    return pl.pallas_call(
        functools.partial(_gdn_kernel, tb=tb),
        grid=(b, t // tb),
        in_specs=[
            pl.BlockSpec((None, tb, gw), lambda bi, ti: (bi, ti, first)),
            pl.BlockSpec((None, tb, gw), lambda bi, ti: (bi, ti, first + 1)),
            pl.BlockSpec((None, tb, gw), lambda bi, ti: (bi, ti, first + 2)),
            pl.BlockSpec((None, tb, gw), lambda bi, ti: (bi, ti, first + 3)),
            pl.BlockSpec((None, tb, AB_COLS), lambda bi, ti: (bi, ti, 0)),
            pl.BlockSpec((AB_COLS, tb), lambda bi, ti: (0, bi * (t // tb) + ti)),
            pl.BlockSpec((CONV_K, 3 * gw), lambda bi, ti: (0, 0)),
            pl.BlockSpec((1, GDN_HEADS), lambda bi, ti: (0, 0)),
            pl.BlockSpec((1, GDN_HEADS), lambda bi, ti: (0, 0)),
            pl.BlockSpec((GDN_HEADS, 1), lambda bi, ti: (0, 0)),
            pl.BlockSpec((GDN_HEADS, 1), lambda bi, ti: (0, 0)),
            pl.BlockSpec((1, GDN_HEAD_DIM), lambda bi, ti: (0, 0)),
        ],
        out_specs=pl.BlockSpec((None, tb, gw), lambda bi, ti: (bi, ti, 0)),
        out_shape=jax.ShapeDtypeStruct((b, t, gw), BF16),
        scratch_shapes=[
            pltpu.VMEM((GDN_HEADS, GDN_HEAD_DIM, GDN_HEAD_DIM), F32),
            pltpu.VMEM((3, BF16_ROWS + GDN_CHUNK, gw), BF16),
            pltpu.VMEM((tb, gw), BF16),
            pltpu.VMEM((tb, gw), BF16),
            pltpu.VMEM((tb, gw), BF16),
            pltpu.VMEM((tb, GDN_HEADS), F32),
            pltpu.VMEM((n_chunk, GDN_HEADS, GDN_CHUNK), F32),
            pltpu.VMEM((tb, GDN_HEADS), F32),
            pltpu.VMEM((n_chunk, GDN_HEADS, GDN_CHUNK), F32),
            pltpu.VMEM((n_inst * LST_PITCH, HEAD_LANES), F32),
            pltpu.VMEM((GDN_CHUNK, GDN_CHUNK, n_inst), F32),
            pltpu.VMEM((GDN_CHUNK, GDN_CHUNK, n_inst), F32),
            pltpu.VMEM((n_inst, GDN_CHUNK, GDN_CHUNK), BF16),
        ],
        compiler_params=pltpu.CompilerParams(
            dimension_semantics=("parallel", "arbitrary"), vmem_limit_bytes=GDN_VMEM_LIMIT),
        name="gdn",
    )(proj3, proj3, proj3, proj3, ab3, abt2, conv_w,
      a_log.reshape(1, GDN_HEADS), dt_bias.reshape(1, GDN_HEADS),
      a_log.reshape(GDN_HEADS, 1), dt_bias.reshape(GDN_HEADS, 1),
      gdn_norm_w.reshape(1, GDN_HEAD_DIM))


def _outproj_kernel(oda_ref, ogdn_ref, w_ref, x_ref, fw_ref, o_ref, *, final):
    acc = _dot(oda_ref[...], w_ref[0:DA_WIDTH, :]) + _dot(ogdn_ref[...], w_ref[DA_WIDTH:DA_WIDTH + GDN_WIDTH, :])
    y = x_ref[...] + acc
    if final:
        y = y * lax.rsqrt(jnp.mean(y * y, axis=-1, keepdims=True) + RMS_EPS) * fw_ref[...]
    o_ref[...] = y


def _outproj(o_da2, o_gdn2, w_out_bf16, layer, x2, final_w, final):
    m = x2.shape[0]
    tm = OUTPROJ_TM
    return pl.pallas_call(
        functools.partial(_outproj_kernel, final=final),
        grid=(m // tm,),
        in_specs=[
            pl.BlockSpec((tm, DA_WIDTH), lambda i: (i, 0)),
            pl.BlockSpec((tm, GDN_WIDTH), lambda i: (i, 0)),
            pl.BlockSpec((None, DA_WIDTH + GDN_WIDTH, D_MODEL), lambda i: (layer, 0, 0)),
            pl.BlockSpec((tm, D_MODEL), lambda i: (i, 0)),
            pl.BlockSpec((1, D_MODEL), lambda i: (0, 0)),
        ],
        out_specs=pl.BlockSpec((tm, D_MODEL), lambda i: (i, 0)),
        out_shape=jax.ShapeDtypeStruct((m, D_MODEL), F32),
        compiler_params=pltpu.CompilerParams(
            dimension_semantics=("parallel",), vmem_limit_bytes=VMEM_LIMIT),
        name="outproj",
    )(o_da2, o_gdn2, w_out_bf16, x2, final_w)


def kernel(x, norm_w, w_in, w_out, lambda_q1, lambda_k1, lambda_q2, lambda_k2, da_subln_w, rel_bias,
           conv_w, a_log, dt_bias, gdn_norm_w, final_norm_w):
    b, t, d = x.shape
    m = b * t
    x2 = x.reshape(m, d)
    near_bias = _near_bias(rel_bias)
    final_w = final_norm_w.reshape(1, d)
    w_out_bf16 = w_out.astype(BF16)
    w_ab_padded = jnp.pad(w_in[:, :, MAIN_COLS:MAIN_COLS + AB_COLS].astype(BF16),
                          ((0, 0), (0, 0), (0, HEAD_LANES - AB_COLS)))
    for l in range(DEPTH):
        lambda_init = 0.8 - 0.6 * math.exp(-0.3 * l)
        proj, ab, abt = _inproj(x2, norm_w[l].reshape(1, d), w_in, l, w_ab_padded[l])
        proj3 = proj.reshape(b, t, MAIN_COLS)
        lam_params = jnp.stack([lambda_q1[l], lambda_k1[l], lambda_q2[l], lambda_k2[l]])
        o_da = _attention(proj3, near_bias, lam_params, da_subln_w[l].reshape(1, 2 * DA_HEAD_DIM), lambda_init)
        o_gdn = _gdn(proj3, ab.reshape(b, t, AB_COLS), abt, conv_w[l], a_log[l], dt_bias[l], gdn_norm_w[l])
        x2 = _outproj(o_da.reshape(m, DA_WIDTH), o_gdn.reshape(m, GDN_WIDTH), w_out_bf16, l,
                      x2, final_w, final=(l == DEPTH - 1))
    return x2.reshape(b, t, d)
```

```python
import functools
import math

import jax
import jax.numpy as jnp
from jax import lax
from jax.experimental import pallas as pl
from jax.experimental.pallas import tpu as pltpu

F32 = jnp.float32
BF16 = jnp.bfloat16

D_MODEL = 2048
DEPTH = 2
DA_HEADS = 8
DA_HEAD_DIM = 64
DA_WIDTH = DA_HEADS * 2 * DA_HEAD_DIM
GDN_HEADS = 8
GDN_HEAD_DIM = 128
GDN_WIDTH = GDN_HEADS * GDN_HEAD_DIM
CONV_K = 4
GDN_CHUNK = 64
REL_BUCKETS = 32
REL_MAX_DIST = 128
REL_MAX_EXACT = REL_BUCKETS // 2
RMS_EPS = 1e-6
NEG_INF = -1e30
MAIN_COLS = 4 * DA_WIDTH + 4 * GDN_WIDTH
AB_COLS = 2 * GDN_HEADS
HEAD_LANES = 128
BF16_ROWS = 16
LOG2_E = math.log2(math.e)

INPROJ_TM = 1024
INPROJ_TN = 512
ATTN_TQ = 512
ATTN_TK = 512
ATTN_LANE_TILE = 256
ATTN_HEADS_PER_STEP = 2
GDN_TB = 1024
OUTPROJ_TM = 256
VMEM_LIMIT = 48 * 1024 * 1024
GDN_VMEM_LIMIT = 56 * 1024 * 1024

NT_DIMS = (((1,), (1,)), ((), ()))


def _sigmoid(x):
    return 1.0 / (1.0 + jnp.exp(-x))


def _silu(x):
    h = 0.5 * x
    return h + h * jnp.tanh(h)


def _softplus(x):
    return jnp.maximum(x, 0.0) + jnp.log(1.0 + jnp.exp(-jnp.abs(x)))


def _dot(a, b):
    return jnp.dot(a, b, preferred_element_type=F32)


def _split_hi_lo(a):
    hi = a.astype(BF16)
    lo = (a - hi.astype(F32)).astype(BF16)
    return hi, lo


def _dot_split(a, b):
    a_hi, a_lo = _split_hi_lo(a)
    b_hi, b_lo = _split_hi_lo(b)
    return _dot(a_hi, b_hi) + _dot(a_lo, b_hi) + _dot(a_hi, b_lo)


def _inproj_kernel(x_ref, nw_ref, w_ref, wab_ref, proj_ref, ab_ref, abt_ref, h_ref):
    @pl.when(pl.program_id(1) == 0)
    def _():
        x = x_ref[...]
        ms = jnp.mean(x * x, axis=-1, keepdims=True)
        h = (x * lax.rsqrt(ms + RMS_EPS) * nw_ref[...]).astype(BF16)
        h_ref[...] = h
        w_ab = jnp.concatenate(
            [wab_ref[...].astype(BF16), jnp.zeros((HEAD_LANES - AB_COLS, D_MODEL), BF16)], axis=0)
        ab = lax.dot_general(h, w_ab, NT_DIMS, preferred_element_type=F32)
        ab_ref[...] = ab[:, 0:AB_COLS]
        abt_ref[...] = ab.T[0:AB_COLS, :]

    proj_ref[...] = lax.dot_general(h_ref[...], w_ref[...].astype(BF16), NT_DIMS,
                                    preferred_element_type=F32).astype(BF16)


def _inproj(x2, norm_w, w_in_t, layer):
    m = x2.shape[0]
    tm, tn = INPROJ_TM, INPROJ_TN
    return pl.pallas_call(
        _inproj_kernel,
        grid=(m // tm, MAIN_COLS // tn),
        in_specs=[
            pl.BlockSpec((tm, D_MODEL), lambda i, j: (i, 0)),
            pl.BlockSpec((1, D_MODEL), lambda i, j: (0, 0)),
            pl.BlockSpec((None, tn, D_MODEL), lambda i, j: (layer, j, 0)),
            pl.BlockSpec((None, AB_COLS, D_MODEL), lambda i, j: (layer, MAIN_COLS // AB_COLS, 0)),
        ],
        out_specs=[
            pl.BlockSpec((tm, tn), lambda i, j: (i, j)),
            pl.BlockSpec((tm, AB_COLS), lambda i, j: (i, 0)),
            pl.BlockSpec((AB_COLS, tm), lambda i, j: (0, i)),
        ],
        out_shape=[
            jax.ShapeDtypeStruct((m, MAIN_COLS), BF16),
            jax.ShapeDtypeStruct((m, AB_COLS), F32),
            jax.ShapeDtypeStruct((AB_COLS, m), F32),
        ],
        scratch_shapes=[pltpu.VMEM((tm, D_MODEL), BF16)],
        compiler_params=pltpu.CompilerParams(
            dimension_semantics=("parallel", "arbitrary"), vmem_limit_bytes=VMEM_LIMIT),
        name="inproj",
    )(x2, norm_w, w_in_t, w_in_t)


def _bias_kernel(rb_ref, o_ref, *, tk):
    h = pl.program_id(0)
    shape = o_ref.shape
    c = lax.broadcasted_iota(jnp.int32, shape, 0)
    r = lax.broadcasted_iota(jnp.int32, shape, 1)
    dist = r - c + tk
    n = jnp.maximum(dist, 0)
    nf = jnp.maximum(n, REL_MAX_EXACT).astype(F32)
    large = REL_MAX_EXACT + (jnp.log(nf / REL_MAX_EXACT) / math.log(REL_MAX_DIST / REL_MAX_EXACT)
                             * (REL_BUCKETS - REL_MAX_EXACT)).astype(jnp.int32)
    large = jnp.minimum(large, REL_BUCKETS - 1)
    bucket = jnp.where(n < REL_MAX_EXACT, n, large)
    far = rb_ref[REL_BUCKETS - 1, h]
    val = jnp.zeros(shape, F32)
    for b in range(REL_BUCKETS - 1):
        val = jnp.where(bucket == b, rb_ref[b, h] - far, val)
    o_ref[...] = jnp.where(dist >= 0, val * LOG2_E, NEG_INF)


def _near_bias(rel_bias):
    tq, tk = ATTN_TQ, ATTN_TK
    return pl.pallas_call(
        functools.partial(_bias_kernel, tk=tk),
        grid=(DA_HEADS,),
        in_specs=[pl.BlockSpec(memory_space=pltpu.SMEM)],
        out_specs=pl.BlockSpec((None, 2 * tk, tq), lambda h: (h, 0, 0)),
        out_shape=jax.ShapeDtypeStruct((DA_HEADS, 2 * tk, tq), F32),
        name="near_bias",
    )(rel_bias)


def _attn_kernel(q_ref, k_ref, v_ref, g_ref, bias_ref, lam_ref, sw_ref, o_ref,
                 qz_ref, vt_ref, m_ref, acc_ref, sa_ref, sb_ref, *, tq, tk, lambda_init):
    i = pl.program_id(2)
    hl = HEAD_LANES
    lt = ATTN_LANE_TILE
    hp = ATTN_HEADS_PER_STEP
    n_tiles = 2 * tq // lt
    q_tiles = tq // lt
    head_cols = [slice(hd * hl, (hd + 1) * hl) for hd in range(hp)]

    @pl.when(i == 0)
    def _():
        ones = jnp.ones((vt_ref.shape[2] - hl, tk), BF16)
        for hd in range(hp):
            for j in range(vt_ref.shape[1]):
                vt_ref[hd, j, 0:hl, :] = v_ref[j * tk:(j + 1) * tk, head_cols[hd]].astype(F32).T.astype(BF16)
                vt_ref[hd, j, hl:, :] = ones

    for hd in range(hp):
        q = q_ref[:, head_cols[hd]] * (DA_HEAD_DIM ** -0.5 * LOG2_E)
        lane = lax.broadcasted_iota(jnp.int32, q.shape, 1)
        zero = jnp.zeros_like(q)
        qz_ref[hd, 0:tq, :] = jnp.where(lane < DA_HEAD_DIM, q, zero)
        qz_ref[hd, tq:2 * tq, :] = jnp.where(lane >= DA_HEAD_DIM, q, zero)
    m_ref[...] = jnp.full(m_ref.shape, NEG_INF, F32)
    acc_ref[...] = jnp.zeros(acc_ref.shape, F32)

    units = [(hd, t) for hd in range(hp) for t in range(n_tiles)]
    cols = [slice(t * lt, (t + 1) * lt) for t in range(n_tiles)]

    def keys_used(t, diag):
        return (t % q_tiles + 1) * lt if diag else tk

    def scores(j, s_ref, diag=False):
        row0 = pl.multiple_of(j * tk, tk)
        for u, (hd, t) in enumerate(units):
            nk = keys_used(t, diag)
            kj = k_ref[pl.ds(row0, nk), head_cols[hd]]
            s_ref[u, 0:nk, :] = lax.dot_general(kj, qz_ref[hd, cols[t], :], NT_DIMS, preferred_element_type=F32)

    def softmax_pv(j, s_ref, bias_row0, bias_valid=None, diag=False):
        nk = [keys_used(t, diag) for _, t in units]
        s = [s_ref[u, 0:nk[u], :] for u in range(len(units))]
        if bias_row0 is not None:
            bias = [bias_ref[hd, bias_row0:bias_row0 + nk[u], cols[t % q_tiles]] for u, (hd, t) in enumerate(units)]
            if bias_valid is not None:
                bias = [jnp.where(bias_valid, b, NEG_INF) for b in bias]
            s = [a + b for a, b in zip(s, bias)]
        m_prev = [m_ref[hd, :, cols[t]] for hd, t in units]
        m_new = [jnp.maximum(mp, jnp.max(a, axis=0, keepdims=True)) for mp, a in zip(m_prev, s)]
        p = [jnp.exp2(a - mn).astype(BF16) for a, mn in zip(s, m_new)]
        pv = [_dot(vt_ref[hd, j, :, 0:nk[u]], p[u]) for u, (hd, t) in enumerate(units)]
        for u, (hd, t) in enumerate(units):
            acc_ref[hd, :, cols[t]] = jnp.exp2(m_prev[u] - m_new[u]) * acc_ref[hd, :, cols[t]] + pv[u]
            m_ref[hd, :, cols[t]] = m_new[u]

    n_far = jnp.maximum(i - 1, 0)
    odd = n_far % 2

    @pl.when(odd == 1)
    def _():
        scores(0, sa_ref)
        softmax_pv(0, sa_ref, None)

    scores(odd, sa_ref)

    def far_pair(jj, carry):
        j = odd + 2 * jj
        scores(j + 1, sb_ref)
        softmax_pv(j, sa_ref, None)
        scores(j + 2, sa_ref)
        softmax_pv(j + 1, sb_ref, None)
        return carry

    lax.fori_loop(0, n_far // 2, far_pair, 0)
    scores(i, sb_ref, diag=True)
    softmax_pv(n_far, sa_ref, 0, bias_valid=i > 0)
    softmax_pv(i, sb_ref, tk, diag=True)

    lp = lam_ref[...]
    s1 = jnp.sum(lp[0:1, :] * lp[1:2, :], axis=-1, keepdims=True)
    s2 = jnp.sum(lp[2:3, :] * lp[3:4, :], axis=-1, keepdims=True)
    lam = jnp.exp(s1) - jnp.exp(s2) + lambda_init
    for hd in range(hp):
        inv_l = 1.0 / acc_ref[hd, hl:hl + 1, :]
        o_t = (acc_ref[hd, 0:hl, 0:tq] * inv_l[:, 0:tq]
               - lam * (acc_ref[hd, 0:hl, tq:2 * tq] * inv_l[:, tq:2 * tq]))
        o = o_t.T
        ms = jnp.mean(o * o, axis=-1, keepdims=True)
        y = o * lax.rsqrt(ms + RMS_EPS) * sw_ref[...] * (1.0 - lambda_init)
        o_ref[:, head_cols[hd]] = (y * _silu(g_ref[:, head_cols[hd]].astype(F32))).astype(BF16)


def _attention(proj3, near_bias, lam_params, subln_w, lambda_init):
    b, t, _ = proj3.shape
    tq, tk = ATTN_TQ, ATTN_TK
    assert tq == tk and t % tq == 0
    hl = HEAD_LANES
    hp = ATTN_HEADS_PER_STEP
    wide = hp * hl
    sect = DA_WIDTH // wide
    n_units = hp * (2 * tq // ATTN_LANE_TILE)
    return pl.pallas_call(
        functools.partial(_attn_kernel, tq=tq, tk=tk, lambda_init=lambda_init),
        grid=(b, DA_HEADS // hp, t // tq),
        in_specs=[
            pl.BlockSpec((None, tq, wide), lambda bi, h, i: (bi, i, h)),
            pl.BlockSpec((None, t, wide), lambda bi, h, i: (bi, 0, sect + h)),
            pl.BlockSpec((None, t, wide), lambda bi, h, i: (bi, 0, 2 * sect + h)),
            pl.BlockSpec((None, tq, wide), lambda bi, h, i: (bi, i, 3 * sect + h)),
            pl.BlockSpec((hp, 2 * tk, tq), lambda bi, h, i: (h, 0, 0)),
            pl.BlockSpec((4, DA_HEAD_DIM), lambda bi, h, i: (0, 0)),
            pl.BlockSpec((1, hl), lambda bi, h, i: (0, 0)),
        ],
        out_specs=pl.BlockSpec((None, tq, wide), lambda bi, h, i: (bi, i, h)),
        out_shape=jax.ShapeDtypeStruct((b, t, DA_WIDTH), BF16),
        scratch_shapes=[
            pltpu.VMEM((hp, 2 * tq, hl), BF16),
            pltpu.VMEM((hp, t // tk, hl + BF16_ROWS, tk), BF16),
            pltpu.VMEM((hp, 1, 2 * tq), F32),
            pltpu.VMEM((hp, hl + BF16_ROWS, 2 * tq), F32),
            pltpu.VMEM((n_units, tk, ATTN_LANE_TILE), F32),
            pltpu.VMEM((n_units, tk, ATTN_LANE_TILE), F32),
        ],
        compiler_params=pltpu.CompilerParams(
            dimension_semantics=("parallel", "parallel", "arbitrary"), vmem_limit_bytes=VMEM_LIMIT),
        name="diff_attn",
    )(proj3, proj3, proj3, proj3, near_bias, lam_params, subln_w)


LST_PITCH = 72
SUBLANES = 8
XPOSE_UNROLL = 4


def _unit_lower_inverse_on_lanes(lt_ref, tt_ref, n_inst):
    c_len = GDN_CHUNK
    sub = lax.broadcasted_iota(jnp.int32, (SUBLANES, n_inst), 0)
    zero = jnp.zeros((SUBLANES, n_inst), F32)
    for i in range(c_len):
        n_blk = (i - 1) // SUBLANES + 1 if i > 0 else 0
        acc = [None] * n_blk
        for k in range(i):
            lik = jnp.broadcast_to(lt_ref[i, k:k + 1, :], (SUBLANES, n_inst))
            for jb in range(k // SUBLANES + 1):
                term = lik * tt_ref[k, jb * SUBLANES:(jb + 1) * SUBLANES, :]
                acc[jb] = term if acc[jb] is None else acc[jb] + term
        for jb in range(c_len // SUBLANES):
            val = -acc[jb] if jb < n_blk else zero
            if jb == i // SUBLANES:
                val = jnp.where(sub == i % SUBLANES, 1.0, val)
            tt_ref[i, jb * SUBLANES:(jb + 1) * SUBLANES, :] = val


def _gdn_kernel(qp_ref, kp_ref, vp_ref, z_ref, ab_ref, abt_ref, cw_ref, alog_ref, dtb_ref,
                alogt_ref, dtbt_ref, nw_ref, o_ref,
                s_ref, cbuf_ref, qn_ref, kn_ref, vn_ref, gc_ref, gct_ref, beta_ref, bt_ref,
                lst_ref, lt_ref, tt_ref, qk_ref, *, tb):
    c_len = GDN_CHUNK
    hl = HEAD_LANES
    n_chunk = tb // c_len
    n_inst = n_chunk * GDN_HEADS
    halo = BF16_ROWS

    @pl.when(pl.program_id(1) == 0)
    def _():
        s_ref[...] = jnp.zeros(s_ref.shape, F32)
        cbuf_ref[:, 0:halo, :] = jnp.zeros((3, halo, GDN_WIDTH), BF16)

    sel_r = lax.broadcasted_iota(jnp.int32, ((CONV_K - 1) * c_len, halo + c_len), 0)
    sel_c = lax.broadcasted_iota(jnp.int32, ((CONV_K - 1) * c_len, halo + c_len), 1)
    tap = sel_r // c_len
    shift_mat = (sel_c == sel_r - tap * c_len + halo - (CONV_K - 1) + tap).astype(BF16)

    def conv_chunk(c, carry):
        rows = pl.ds(pl.multiple_of(c * c_len, c_len), c_len)
        for g, (src, dst) in enumerate(((qp_ref, qn_ref), (kp_ref, kn_ref), (vp_ref, vn_ref))):
            xb = src[rows, :]
            cbuf_ref[g, halo:halo + c_len, :] = xb
            w = cw_ref[:, g * GDN_WIDTH:(g + 1) * GDN_WIDTH]
            taps = _dot(shift_mat, cbuf_ref[g])
            y = w[0:1, :] * taps[0:c_len, :]
            y = y + w[1:2, :] * taps[c_len:2 * c_len, :]
            y = y + w[2:3, :] * taps[2 * c_len:3 * c_len, :]
            y = y + w[3:4, :] * xb.astype(F32)
            cbuf_ref[g, 0:halo, :] = xb[c_len - halo:c_len, :]
            y = _silu(y)
            for h in range(GDN_HEADS):
                cs = slice(h * hl, (h + 1) * hl)
                yh = y[:, cs]
                if g < 2:
                    yh = yh * lax.rsqrt(jnp.sum(yh * yh, axis=-1, keepdims=True) + RMS_EPS)
                if g == 0:
                    yh = yh * (GDN_HEAD_DIM ** -0.5)
                dst[rows, cs] = yh.astype(BF16)
        return carry

    lax.fori_loop(0, n_chunk, conv_chunk, 0)

    ab = ab_ref[...]
    beta_ref[...] = _sigmoid(ab[:, 0:GDN_HEADS])
    g_raw = -jnp.exp(alog_ref[...]) * _softplus(ab[:, GDN_HEADS:AB_COLS] + dtb_ref[...])
    abt = abt_ref[...]
    beta_t = _sigmoid(abt[0:GDN_HEADS, :])
    g_raw_t = -jnp.exp(alogt_ref[...]) * _softplus(abt[GDN_HEADS:AB_COLS, :] + dtbt_ref[...])
    row = lax.broadcasted_iota(jnp.int32, (c_len, c_len), 0)
    col = lax.broadcasted_iota(jnp.int32, (c_len, c_len), 1)
    tril = row >= col
    strict = row > col
    tril_f = tril.astype(F32)
    triu_f = (row <= col).astype(F32)
    for c in range(n_chunk):
        gc_ref[c * c_len:(c + 1) * c_len, :] = _dot_split(tril_f, g_raw[c * c_len:(c + 1) * c_len, :])
        gct_ref[c] = _dot_split(g_raw_t[:, c * c_len:(c + 1) * c_len], triu_f)
        bt_ref[c] = beta_t[:, c * c_len:(c + 1) * c_len]

    heads = range(GDN_HEADS)
    lanes = [slice(h * hl, (h + 1) * hl) for h in heads]

    def intra_chunk(c, carry):
        rows = pl.ds(pl.multiple_of(c * c_len, c_len), c_len)
        gct = gct_ref[c]
        gcc = gc_ref[rows, :]
        betac = beta_ref[rows, :]
        ks = [kn_ref[rows, lanes[h]] for h in heads]
        kq = [jnp.concatenate([ks[h], qn_ref[rows, lanes[h]]], axis=0) for h in heads]
        kk = [lax.dot_general(kq[h], ks[h], NT_DIMS, preferred_element_type=F32) for h in heads]
        decay = [jnp.where(tril, jnp.exp(jnp.where(tril, gcc[:, h:h + 1] - gct[h:h + 1, :], 0.0)), 0.0)
                 for h in heads]
        pad = jnp.zeros((c_len, hl - c_len), F32)
        for h in heads:
            l_strict = jnp.where(strict, kk[h][0:c_len, :] * decay[h] * betac[:, h:h + 1], 0.0)
            base = pl.multiple_of((c * GDN_HEADS + h) * LST_PITCH, SUBLANES)
            lst_ref[pl.ds(base, c_len), :] = jnp.concatenate([l_strict, pad], axis=1)
            qk_ref[c * GDN_HEADS + h] = (kk[h][c_len:2 * c_len, :] * decay[h]).astype(BF16)
        return carry

    lax.fori_loop(0, n_chunk, intra_chunk, 0)

    def to_lanes(ii, carry):
        for i in [ii * XPOSE_UNROLL + u for u in range(XPOSE_UNROLL)]:
            x = lst_ref[pl.ds(i, n_inst, stride=LST_PITCH), :]
            lt_ref[i] = x.T[0:c_len, :]
        return carry

    lax.fori_loop(0, c_len // XPOSE_UNROLL, to_lanes, 0)
    _unit_lower_inverse_on_lanes(lt_ref, tt_ref, n_inst)

    def from_lanes(ii, carry):
        for i in [ii * XPOSE_UNROLL + u for u in range(XPOSE_UNROLL)]:
            t = tt_ref[i]
            lst_ref[pl.ds(i, n_inst, stride=LST_PITCH), :] = jnp.concatenate([t, jnp.zeros_like(t)], axis=0).T
        return carry

    lax.fori_loop(0, c_len // XPOSE_UNROLL, from_lanes, 0)

    def scan_chunk(c, carry):
        rows = pl.ds(pl.multiple_of(c * c_len, c_len), c_len)
        gct = gct_ref[c]
        gcc = gc_ref[rows, :]
        bct = bt_ref[c]
        g_row = [gct[h:h + 1, :] for h in heads]
        g_last = [g_row[h][:, c_len - 1:c_len] for h in heads]
        ks = [kn_ref[rows, lanes[h]] for h in heads]
        a_inv = [lst_ref[pl.ds(pl.multiple_of((c * GDN_HEADS + h) * LST_PITCH, SUBLANES), c_len), 0:c_len]
                 for h in heads]
        b_row = [bct[h:h + 1, :] for h in heads]
        u = [_dot((a_inv[h] * b_row[h]).astype(BF16), vn_ref[rows, lanes[h]]) for h in heads]
        w = [_dot((a_inv[h] * (b_row[h] * jnp.exp(g_row[h]))).astype(BF16), ks[h]) for h in heads]
        s_old = [s_ref[h] for h in heads]
        ws = [_dot(jnp.concatenate([w[h].astype(BF16), qn_ref[rows, lanes[h]]], axis=0), s_old[h].astype(BF16))
              for h in heads]
        v_new = [(u[h] - ws[h][0:c_len, :]).astype(BF16) for h in heads]
        kd_t = [(ks[h].astype(F32).T * jnp.exp(g_last[h] - g_row[h])).astype(BF16) for h in heads]
        for h in heads:
            s_ref[h] = s_old[h] * jnp.exp(g_last[h]) + _dot(kd_t[h], v_new[h])
        o = [jnp.exp(gcc[:, h:h + 1]) * ws[h][c_len:2 * c_len, :]
             + _dot(qk_ref[c * GDN_HEADS + h], v_new[h]) for h in heads]
        for h in heads:
            on = o[h] * lax.rsqrt(jnp.mean(o[h] * o[h], axis=-1, keepdims=True) + RMS_EPS) * nw_ref[...]
            zc = z_ref[rows, lanes[h]].astype(F32)
            o_ref[rows, lanes[h]] = (on * _silu(zc)).astype(BF16)
        return carry

    lax.fori_loop(0, n_chunk, scan_chunk, 0)


def _gdn(proj3, ab3, abt2, conv_w, a_log, dt_bias, gdn_norm_w):
    b, t, _ = proj3.shape
    tb = GDN_TB
    gw = GDN_WIDTH
    first = 4 * DA_WIDTH // gw
    n_chunk = tb // GDN_CHUNK
    n_inst = n_chunk * GDN_HEADS
    assert n_inst == HEAD_LANES, "one (chunk, head) matrix per lane"
    return pl.pallas_call(
        functools.partial(_gdn_kernel, tb=tb),
        grid=(b, t // tb),
        in_specs=[
            pl.BlockSpec((None, tb, gw), lambda bi, ti: (bi, ti, first)),
            pl.BlockSpec((None, tb, gw), lambda bi, ti: (bi, ti, first + 1)),
            pl.BlockSpec((None, tb, gw), lambda bi, ti: (bi, ti, first + 2)),
            pl.BlockSpec((None, tb, gw), lambda bi, ti: (bi, ti, first + 3)),
            pl.BlockSpec((None, tb, AB_COLS), lambda bi, ti: (bi, ti, 0)),
            pl.BlockSpec((AB_COLS, tb), lambda bi, ti: (0, bi * (t // tb) + ti)),
            pl.BlockSpec((CONV_K, 3 * gw), lambda bi, ti: (0, 0)),
            pl.BlockSpec((1, GDN_HEADS), lambda bi, ti: (0, 0)),
            pl.BlockSpec((1, GDN_HEADS), lambda bi, ti: (0, 0)),
            pl.BlockSpec((GDN_HEADS, 1), lambda bi, ti: (0, 0)),
            pl.BlockSpec((GDN_HEADS, 1), lambda bi, ti: (0, 0)),
            pl.BlockSpec((1, GDN_HEAD_DIM), lambda bi, ti: (0, 0)),
        ],
        out_specs=pl.BlockSpec((None, tb, gw), lambda bi, ti: (bi, ti, 0)),
        out_shape=jax.ShapeDtypeStruct((b, t, gw), BF16),
        scratch_shapes=[
            pltpu.VMEM((GDN_HEADS, GDN_HEAD_DIM, GDN_HEAD_DIM), F32),
            pltpu.VMEM((3, BF16_ROWS + GDN_CHUNK, gw), BF16),
            pltpu.VMEM((tb, gw), BF16),
            pltpu.VMEM((tb, gw), BF16),
            pltpu.VMEM((tb, gw), BF16),
            pltpu.VMEM((tb, GDN_HEADS), F32),
            pltpu.VMEM((n_chunk, GDN_HEADS, GDN_CHUNK), F32),
            pltpu.VMEM((tb, GDN_HEADS), F32),
            pltpu.VMEM((n_chunk, GDN_HEADS, GDN_CHUNK), F32),
            pltpu.VMEM((n_inst * LST_PITCH, HEAD_LANES), F32),
            pltpu.VMEM((GDN_CHUNK, GDN_CHUNK, n_inst), F32),
            pltpu.VMEM((GDN_CHUNK, GDN_CHUNK, n_inst), F32),
            pltpu.VMEM((n_inst, GDN_CHUNK, GDN_CHUNK), BF16),
        ],
        compiler_params=pltpu.CompilerParams(
            dimension_semantics=("parallel", "arbitrary"), vmem_limit_bytes=GDN_VMEM_LIMIT),
        name="gdn",
    )(proj3, proj3, proj3, proj3, ab3, abt2, conv_w,
      a_log.reshape(1, GDN_HEADS), dt_bias.reshape(1, GDN_HEADS),
      a_log.reshape(GDN_HEADS, 1), dt_bias.reshape(GDN_HEADS, 1),
      gdn_norm_w.reshape(1, GDN_HEAD_DIM))


def _outproj_kernel(oda_ref, ogdn_ref, w_ref, x_ref, fw_ref, o_ref, *, final):
    acc = _dot(oda_ref[...], w_ref[0:DA_WIDTH, :]) + _dot(ogdn_ref[...], w_ref[DA_WIDTH:DA_WIDTH + GDN_WIDTH, :])
    y = x_ref[...] + acc
    if final:
        y = y * lax.rsqrt(jnp.mean(y * y, axis=-1, keepdims=True) + RMS_EPS) * fw_ref[...]
    o_ref[...] = y


def _outproj(o_da2, o_gdn2, w_out_bf16, layer, x2, final_w, final):
    m = x2.shape[0]
    tm = OUTPROJ_TM
    return pl.pallas_call(
        functools.partial(_outproj_kernel, final=final),
        grid=(m // tm,),
        in_specs=[
            pl.BlockSpec((tm, DA_WIDTH), lambda i: (i, 0)),
            pl.BlockSpec((tm, GDN_WIDTH), lambda i: (i, 0)),
            pl.BlockSpec((None, DA_WIDTH + GDN_WIDTH, D_MODEL), lambda i: (layer, 0, 0)),
            pl.BlockSpec((tm, D_MODEL), lambda i: (i, 0)),
            pl.BlockSpec((1, D_MODEL), lambda i: (0, 0)),
        ],
        out_specs=pl.BlockSpec((tm, D_MODEL), lambda i: (i, 0)),
        out_shape=jax.ShapeDtypeStruct((m, D_MODEL), F32),
        compiler_params=pltpu.CompilerParams(
            dimension_semantics=("parallel",), vmem_limit_bytes=VMEM_LIMIT),
        name="outproj",
    )(o_da2, o_gdn2, w_out_bf16, x2, final_w)


def kernel(x, norm_w, w_in, w_out, lambda_q1, lambda_k1, lambda_q2, lambda_k2, da_subln_w, rel_bias,
           conv_w, a_log, dt_bias, gdn_norm_w, final_norm_w):
    b, t, d = x.shape
    m = b * t
    x2 = x.reshape(m, d)
    near_bias = _near_bias(rel_bias)
    final_w = final_norm_w.reshape(1, d)
    w_out_bf16 = w_out.astype(BF16)
    w_in_t = jnp.swapaxes(w_in, 1, 2)
    for l in range(DEPTH):
        lambda_init = 0.8 - 0.6 * math.exp(-0.3 * l)
        proj, ab, abt = _inproj(x2, norm_w[l].reshape(1, d), w_in_t, l)
        proj3 = proj.reshape(b, t, MAIN_COLS)
        lam_params = jnp.stack([lambda_q1[l], lambda_k1[l], lambda_q2[l], lambda_k2[l]])
        o_da = _attention(proj3, near_bias, lam_params, da_subln_w[l].reshape(1, 2 * DA_HEAD_DIM), lambda_init)
        o_gdn = _gdn(proj3, ab.reshape(b, t, AB_COLS), abt, conv_w[l], a_log[l], dt_bias[l], gdn_norm_w[l])
        x2 = _outproj(o_da.reshape(m, DA_WIDTH), o_gdn.reshape(m, GDN_WIDTH), w_out_bf16, l,
                      x2, final_w, final=(l == DEPTH - 1))
    return x2.reshape(b, t, d)
```

```python
import functools
import math

import jax
import jax.numpy as jnp
from jax import lax
from jax.experimental import pallas as pl
from jax.experimental.pallas import tpu as pltpu

F32 = jnp.float32
BF16 = jnp.bfloat16

D_MODEL = 2048
DEPTH = 2
DA_HEADS = 8
DA_HEAD_DIM = 64
DA_WIDTH = DA_HEADS * 2 * DA_HEAD_DIM
GDN_HEADS = 8
GDN_HEAD_DIM = 128
GDN_WIDTH = GDN_HEADS * GDN_HEAD_DIM
CONV_K = 4
GDN_CHUNK = 64
REL_BUCKETS = 32
REL_MAX_DIST = 128
REL_MAX_EXACT = REL_BUCKETS // 2
RMS_EPS = 1e-6
NEG_INF = -1e30
MAIN_COLS = 4 * DA_WIDTH + 4 * GDN_WIDTH
AB_COLS = 2 * GDN_HEADS
HEAD_LANES = 128
BF16_ROWS = 16
LOG2_E = math.log2(math.e)

INPROJ_TM = 1024
INPROJ_TN = 1024
ATTN_TQ = 512
ATTN_TK = 512
ATTN_LANE_TILE = 256
ATTN_HEADS_PER_STEP = 2
BIAS_BLOCK = (64, 128)
GDN_TB = 512
GDN_BATCH_PAIR = 2
OUTPROJ_TM = 256
VMEM_LIMIT = 48 * 1024 * 1024
GDN_VMEM_LIMIT = 56 * 1024 * 1024

NT_DIMS = (((1,), (1,)), ((), ()))


def _sigmoid(x):
    return 1.0 / (1.0 + jnp.exp(-x))


def _silu(x):
    h = 0.5 * x
    return h + h * jnp.tanh(h)


def _softplus(x):
    return jnp.maximum(x, 0.0) + jnp.log(1.0 + jnp.exp(-jnp.abs(x)))


def _dot(a, b):
    return jnp.dot(a, b, preferred_element_type=F32)


def _split_hi_lo(a):
    hi = a.astype(BF16)
    lo = (a - hi.astype(F32)).astype(BF16)
    return hi, lo


def _dot_split(a, b):
    a_hi, a_lo = _split_hi_lo(a)
    b_hi, b_lo = _split_hi_lo(b)
    return _dot(a_hi, b_hi) + _dot(a_lo, b_hi) + _dot(a_hi, b_lo)


def _inproj_kernel(x_ref, nw_ref, w_ref, wab_ref, proj_ref, ab_ref, abt_ref, h_ref):
    @pl.when(pl.program_id(1) == 0)
    def _():
        x = x_ref[...]
        ms = jnp.mean(x * x, axis=-1, keepdims=True)
        h = (x * lax.rsqrt(ms + RMS_EPS) * nw_ref[...]).astype(BF16)
        h_ref[...] = h
        w_ab = jnp.concatenate(
            [wab_ref[...].astype(BF16), jnp.zeros((HEAD_LANES - AB_COLS, D_MODEL), BF16)], axis=0)
        ab = lax.dot_general(h, w_ab, NT_DIMS, preferred_element_type=F32)
        ab_ref[...] = ab[:, 0:AB_COLS]
        abt_ref[...] = ab.T[0:AB_COLS, :]

    proj_ref[...] = lax.dot_general(h_ref[...], w_ref[...].astype(BF16), NT_DIMS,
                                    preferred_element_type=F32).astype(BF16)


def _inproj(x2, norm_w, w_in_t, layer):
    m = x2.shape[0]
    tm, tn = INPROJ_TM, INPROJ_TN
    return pl.pallas_call(
        _inproj_kernel,
        grid=(m // tm, MAIN_COLS // tn),
        in_specs=[
            pl.BlockSpec((tm, D_MODEL), lambda i, j: (i, 0)),
            pl.BlockSpec((1, D_MODEL), lambda i, j: (0, 0)),
            pl.BlockSpec((None, tn, D_MODEL), lambda i, j: (layer, j, 0)),
            pl.BlockSpec((None, AB_COLS, D_MODEL), lambda i, j: (layer, MAIN_COLS // AB_COLS, 0)),
        ],
        out_specs=[
            pl.BlockSpec((tm, tn), lambda i, j: (i, j)),
            pl.BlockSpec((tm, AB_COLS), lambda i, j: (i, 0)),
            pl.BlockSpec((AB_COLS, tm), lambda i, j: (0, i)),
        ],
        out_shape=[
            jax.ShapeDtypeStruct((m, MAIN_COLS), BF16),
            jax.ShapeDtypeStruct((m, AB_COLS), F32),
            jax.ShapeDtypeStruct((AB_COLS, m), F32),
        ],
        scratch_shapes=[pltpu.VMEM((tm, D_MODEL), BF16)],
        compiler_params=pltpu.CompilerParams(
            dimension_semantics=("parallel", "arbitrary"), vmem_limit_bytes=GDN_VMEM_LIMIT),
        name="inproj",
    )(x2, norm_w, w_in_t, w_in_t)


def _bias_kernel(rb_ref, o_ref, *, tk):
    h = pl.program_id(0)
    far = rb_ref[REL_BUCKETS - 1, h]
    br, bc = BIAS_BLOCK
    shape = (br, bc)
    for rb in range(o_ref.shape[0] // br):
        for cb in range(o_ref.shape[1] // bc):
            blk = (slice(rb * br, (rb + 1) * br), slice(cb * bc, (cb + 1) * bc))
            d_min = cb * bc - (rb * br + br - 1) + tk
            d_max = cb * bc + bc - 1 - rb * br + tk
            if d_max < 0:
                o_ref[blk] = jnp.full(shape, NEG_INF, F32)
            elif d_min >= REL_MAX_DIST:
                o_ref[blk] = jnp.zeros(shape, F32)
            else:
                c = lax.broadcasted_iota(jnp.int32, shape, 0) + rb * br
                r = lax.broadcasted_iota(jnp.int32, shape, 1) + cb * bc
                dist = r - c + tk
                n = jnp.maximum(dist, 0)
                nf = jnp.maximum(n, REL_MAX_EXACT).astype(F32)
                large = REL_MAX_EXACT + (jnp.log(nf / REL_MAX_EXACT) / math.log(REL_MAX_DIST / REL_MAX_EXACT)
                                         * (REL_BUCKETS - REL_MAX_EXACT)).astype(jnp.int32)
                large = jnp.minimum(large, REL_BUCKETS - 1)
                bucket = jnp.where(n < REL_MAX_EXACT, n, large)
                val = jnp.zeros(shape, F32)
                for b in range(REL_BUCKETS - 1):
                    val = jnp.where(bucket == b, rb_ref[b, h] - far, val)
                o_ref[blk] = jnp.where(dist >= 0, val * LOG2_E, NEG_INF)


def _near_bias(rel_bias):
    tq, tk = ATTN_TQ, ATTN_TK
    return pl.pallas_call(
        functools.partial(_bias_kernel, tk=tk),
        grid=(DA_HEADS,),
        in_specs=[pl.BlockSpec(memory_space=pltpu.SMEM)],
        out_specs=pl.BlockSpec((None, 2 * tk, tq), lambda h: (h, 0, 0)),
        out_shape=jax.ShapeDtypeStruct((DA_HEADS, 2 * tk, tq), F32),
        name="near_bias",
    )(rel_bias)


def _attn_kernel(q_ref, k_ref, v_ref, g_ref, bias_ref, lam_ref, sw_ref, o_ref,
                 qz_ref, vt_ref, m_ref, acc_ref, sa_ref, sb_ref, *, tq, tk, lambda_init):
    i = pl.program_id(2)
    hl = HEAD_LANES
    lt = ATTN_LANE_TILE
    hp = ATTN_HEADS_PER_STEP
    n_tiles = 2 * tq // lt
    q_tiles = tq // lt
    head_cols = [slice(hd * hl, (hd + 1) * hl) for hd in range(hp)]

    @pl.when(i == 0)
    def _():
        ones = jnp.ones((vt_ref.shape[2] - hl, tk), BF16)
        for hd in range(hp):
            for j in range(vt_ref.shape[1]):
                vt_ref[hd, j, 0:hl, :] = v_ref[j * tk:(j + 1) * tk, head_cols[hd]].astype(F32).T.astype(BF16)
                vt_ref[hd, j, hl:, :] = ones

    for hd in range(hp):
        q = q_ref[:, head_cols[hd]] * (DA_HEAD_DIM ** -0.5 * LOG2_E)
        lane = lax.broadcasted_iota(jnp.int32, q.shape, 1)
        zero = jnp.zeros_like(q)
        qz_ref[hd, 0:tq, :] = jnp.where(lane < DA_HEAD_DIM, q, zero)
        qz_ref[hd, tq:2 * tq, :] = jnp.where(lane >= DA_HEAD_DIM, q, zero)
    m_ref[...] = jnp.full(m_ref.shape, NEG_INF, F32)
    acc_ref[...] = jnp.zeros(acc_ref.shape, F32)

    units = [(hd, t) for hd in range(hp) for t in range(n_tiles)]
    cols = [slice(t * lt, (t + 1) * lt) for t in range(n_tiles)]

    def keys_used(t, diag):
        return (t % q_tiles + 1) * lt if diag else tk

    def scores(j, s_ref, diag=False):
        row0 = pl.multiple_of(j * tk, tk)
        for u, (hd, t) in enumerate(units):
            nk = keys_used(t, diag)
            kj = k_ref[pl.ds(row0, nk), head_cols[hd]]
            s_ref[u, 0:nk, :] = lax.dot_general(kj, qz_ref[hd, cols[t], :], NT_DIMS, preferred_element_type=F32)

    def softmax_pv(j, s_ref, bias_row0, bias_valid=None, diag=False):
        nk = [keys_used(t, diag) for _, t in units]
        s = [s_ref[u, 0:nk[u], :] for u in range(len(units))]
        if bias_row0 is not None:
            bias = [bias_ref[hd, bias_row0:bias_row0 + nk[u], cols[t % q_tiles]] for u, (hd, t) in enumerate(units)]
            if bias_valid is not None:
                bias = [jnp.where(bias_valid, b, NEG_INF) for b in bias]
            s = [a + b for a, b in zip(s, bias)]
        m_prev = [m_ref[hd, :, cols[t]] for hd, t in units]
        m_new = [jnp.maximum(mp, jnp.max(a, axis=0, keepdims=True)) for mp, a in zip(m_prev, s)]
        p = [jnp.exp2(a - mn).astype(BF16) for a, mn in zip(s, m_new)]
        pv = [_dot(vt_ref[hd, j, :, 0:nk[u]], p[u]) for u, (hd, t) in enumerate(units)]
        for u, (hd, t) in enumerate(units):
            acc_ref[hd, :, cols[t]] = jnp.exp2(m_prev[u] - m_new[u]) * acc_ref[hd, :, cols[t]] + pv[u]
            m_ref[hd, :, cols[t]] = m_new[u]

    n_far = jnp.maximum(i - 1, 0)
    odd = n_far % 2

    @pl.when(odd == 1)
    def _():
        scores(0, sa_ref)
        softmax_pv(0, sa_ref, None)

    scores(odd, sa_ref)

    def far_pair(jj, carry):
        j = odd + 2 * jj
        scores(j + 1, sb_ref)
        softmax_pv(j, sa_ref, None)
        scores(j + 2, sa_ref)
        softmax_pv(j + 1, sb_ref, None)
        return carry

    lax.fori_loop(0, n_far // 2, far_pair, 0)
    scores(i, sb_ref, diag=True)
    softmax_pv(n_far, sa_ref, 0, bias_valid=i > 0)
    softmax_pv(i, sb_ref, tk, diag=True)

    lp = lam_ref[...]
    s1 = jnp.sum(lp[0:1, :] * lp[1:2, :], axis=-1, keepdims=True)
    s2 = jnp.sum(lp[2:3, :] * lp[3:4, :], axis=-1, keepdims=True)
    lam = jnp.exp(s1) - jnp.exp(s2) + lambda_init
    for hd in range(hp):
        inv_l = 1.0 / acc_ref[hd, hl:hl + 1, :]
        o_t = (acc_ref[hd, 0:hl, 0:tq] * inv_l[:, 0:tq]
               - lam * (acc_ref[hd, 0:hl, tq:2 * tq] * inv_l[:, tq:2 * tq]))
        o = o_t.T
        ms = jnp.mean(o * o, axis=-1, keepdims=True)
        y = o * lax.rsqrt(ms + RMS_EPS) * sw_ref[...] * (1.0 - lambda_init)
        o_ref[:, head_cols[hd]] = (y * _silu(g_ref[:, head_cols[hd]].astype(F32))).astype(BF16)


def _attention(proj3, near_bias, lam_params, subln_w, lambda_init):
    b, t, _ = proj3.shape
    tq, tk = ATTN_TQ, ATTN_TK
    assert tq == tk and t % tq == 0
    hl = HEAD_LANES
    hp = ATTN_HEADS_PER_STEP
    wide = hp * hl
    sect = DA_WIDTH // wide
    n_units = hp * (2 * tq // ATTN_LANE_TILE)
    return pl.pallas_call(
        functools.partial(_attn_kernel, tq=tq, tk=tk, lambda_init=lambda_init),
        grid=(b, DA_HEADS // hp, t // tq),
        in_specs=[
            pl.BlockSpec((None, tq, wide), lambda bi, h, i: (bi, i, h)),
            pl.BlockSpec((None, t, wide), lambda bi, h, i: (bi, 0, sect + h)),
            pl.BlockSpec((None, t, wide), lambda bi, h, i: (bi, 0, 2 * sect + h)),
            pl.BlockSpec((None, tq, wide), lambda bi, h, i: (bi, i, 3 * sect + h)),
            pl.BlockSpec((hp, 2 * tk, tq), lambda bi, h, i: (h, 0, 0)),
            pl.BlockSpec((4, DA_HEAD_DIM), lambda bi, h, i: (0, 0)),
            pl.BlockSpec((1, hl), lambda bi, h, i: (0, 0)),
        ],
        out_specs=pl.BlockSpec((None, tq, wide), lambda bi, h, i: (bi, i, h)),
        out_shape=jax.ShapeDtypeStruct((b, t, DA_WIDTH), BF16),
        scratch_shapes=[
            pltpu.VMEM((hp, 2 * tq, hl), BF16),
            pltpu.VMEM((hp, t // tk, hl + BF16_ROWS, tk), BF16),
            pltpu.VMEM((hp, 1, 2 * tq), F32),
            pltpu.VMEM((hp, hl + BF16_ROWS, 2 * tq), F32),
            pltpu.VMEM((n_units, tk, ATTN_LANE_TILE), F32),
            pltpu.VMEM((n_units, tk, ATTN_LANE_TILE), F32),
        ],
        compiler_params=pltpu.CompilerParams(
            dimension_semantics=("parallel", "parallel", "arbitrary"), vmem_limit_bytes=VMEM_LIMIT),
        name="diff_attn",
    )(proj3, proj3, proj3, proj3, near_bias, lam_params, subln_w)


LST_PITCH = 72
SUBLANES = 8
XPOSE_UNROLL = 4


def _unit_lower_inverse_on_lanes(lt_ref, tt_ref, n_inst):
    c_len = GDN_CHUNK
    sub = lax.broadcasted_iota(jnp.int32, (SUBLANES, n_inst), 0)
    zero = jnp.zeros((SUBLANES, n_inst), F32)
    for i in range(c_len):
        n_blk = (i - 1) // SUBLANES + 1 if i > 0 else 0
        acc = [None] * n_blk
        for k in range(i):
            lik = jnp.broadcast_to(lt_ref[i, k:k + 1, :], (SUBLANES, n_inst))
            for jb in range(k // SUBLANES + 1):
                term = lik * tt_ref[k, jb * SUBLANES:(jb + 1) * SUBLANES, :]
                acc[jb] = term if acc[jb] is None else acc[jb] + term
        for jb in range(c_len // SUBLANES):
            val = -acc[jb] if jb < n_blk else zero
            if jb == i // SUBLANES:
                val = jnp.where(sub == i % SUBLANES, 1.0, val)
            tt_ref[i, jb * SUBLANES:(jb + 1) * SUBLANES, :] = val


def _gdn_kernel(qp_ref, kp_ref, vp_ref, z_ref, ab_ref, abt0_ref, abt1_ref, cw_ref, alog_ref, dtb_ref,
                alogt_ref, dtbt_ref, nw_ref, o_ref,
                s_ref, cbuf_ref, qn_ref, kn_ref, vn_ref, gc_ref, gct_ref, beta_ref, bt_ref,
                lst_ref, lt_ref, tt_ref, qk_ref, *, tb):
    c_len = GDN_CHUNK
    hl = HEAD_LANES
    bp = GDN_BATCH_PAIR
    n_chunk = tb // c_len
    n_inst = bp * n_chunk * GDN_HEADS
    halo = BF16_ROWS
    abt_refs = (abt0_ref, abt1_ref)

    @pl.when(pl.program_id(1) == 0)
    def _():
        s_ref[...] = jnp.zeros(s_ref.shape, F32)
        cbuf_ref[:, 0:halo, :] = jnp.zeros((bp * 3, halo, GDN_WIDTH), BF16)

    sel_r = lax.broadcasted_iota(jnp.int32, ((CONV_K - 1) * c_len, halo + c_len), 0)
    sel_c = lax.broadcasted_iota(jnp.int32, ((CONV_K - 1) * c_len, halo + c_len), 1)
    tap = sel_r // c_len
    shift_mat = (sel_c == sel_r - tap * c_len + halo - (CONV_K - 1) + tap).astype(BF16)

    def conv_chunk(cc, carry):
        bb = cc // n_chunk
        rows = pl.ds(pl.multiple_of((cc - bb * n_chunk) * c_len, c_len), c_len)
        for g, (src, dst) in enumerate(((qp_ref, qn_ref), (kp_ref, kn_ref), (vp_ref, vn_ref))):
            stage = bb * 3 + g
            xb = src[bb, rows, :]
            cbuf_ref[stage, halo:halo + c_len, :] = xb
            w = 0.5 * cw_ref[:, g * GDN_WIDTH:(g + 1) * GDN_WIDTH]
            taps = _dot(shift_mat, cbuf_ref[stage])
            hy = w[0:1, :] * taps[0:c_len, :]
            hy = hy + w[1:2, :] * taps[c_len:2 * c_len, :]
            hy = hy + w[2:3, :] * taps[2 * c_len:3 * c_len, :]
            hy = hy + w[3:4, :] * xb.astype(F32)
            cbuf_ref[stage, 0:halo, :] = xb[c_len - halo:c_len, :]
            y = hy + hy * jnp.tanh(hy)
            for h in range(GDN_HEADS):
                cs = slice(h * hl, (h + 1) * hl)
                yh = y[:, cs]
                if g < 2:
                    inv_norm = lax.rsqrt(jnp.sum(yh * yh, axis=-1, keepdims=True) + RMS_EPS)
                    if g == 0:
                        inv_norm = inv_norm * (GDN_HEAD_DIM ** -0.5)
                    yh = yh * inv_norm
                dst[bb, rows, cs] = yh.astype(BF16)
        return carry

    lax.fori_loop(0, bp * n_chunk, conv_chunk, 0)

    row = lax.broadcasted_iota(jnp.int32, (c_len, c_len), 0)
    col = lax.broadcasted_iota(jnp.int32, (c_len, c_len), 1)
    tril = row >= col
    strict = row > col
    tril_f = tril.astype(F32)
    triu_f = (row <= col).astype(F32)
    for bb in range(bp):
        ab = ab_ref[bb]
        beta_ref[bb] = _sigmoid(ab[:, 0:GDN_HEADS])
        g_raw = -jnp.exp(alog_ref[...]) * _softplus(ab[:, GDN_HEADS:AB_COLS] + dtb_ref[...])
        abt = abt_refs[bb][...]
        beta_t = _sigmoid(abt[0:GDN_HEADS, :])
        g_raw_t = -jnp.exp(alogt_ref[...]) * _softplus(abt[GDN_HEADS:AB_COLS, :] + dtbt_ref[...])
        for c in range(n_chunk):
            cc = bb * n_chunk + c
            gc_ref[bb, c * c_len:(c + 1) * c_len, :] = _dot_split(tril_f, g_raw[c * c_len:(c + 1) * c_len, :])
            gct_ref[cc] = _dot_split(g_raw_t[:, c * c_len:(c + 1) * c_len], triu_f)
            bt_ref[cc] = beta_t[:, c * c_len:(c + 1) * c_len]

    heads = range(GDN_HEADS)
    lanes = [slice(h * hl, (h + 1) * hl) for h in heads]

    def intra_chunk(cc, carry):
        bb = cc // n_chunk
        rows = pl.ds(pl.multiple_of((cc - bb * n_chunk) * c_len, c_len), c_len)
        gct = gct_ref[cc]
        gcc = gc_ref[bb, rows, :]
        betac = beta_ref[bb, rows, :]
        ks = [kn_ref[bb, rows, lanes[h]] for h in heads]
        kq = [jnp.concatenate([ks[h], qn_ref[bb, rows, lanes[h]]], axis=0) for h in heads]
        kk = [lax.dot_general(kq[h], ks[h], NT_DIMS, preferred_element_type=F32) for h in heads]
        decay = [jnp.where(tril, jnp.exp(jnp.where(tril, gcc[:, h:h + 1] - gct[h:h + 1, :], 0.0)), 0.0)
                 for h in heads]
        pad = jnp.zeros((c_len, hl - c_len), F32)
        for h in heads:
            l_strict = jnp.where(strict, kk[h][0:c_len, :] * decay[h] * betac[:, h:h + 1], 0.0)
            base = pl.multiple_of((cc * GDN_HEADS + h) * LST_PITCH, SUBLANES)
            lst_ref[pl.ds(base, c_len), :] = jnp.concatenate([l_strict, pad], axis=1)
            qk_ref[cc * GDN_HEADS + h] = (kk[h][c_len:2 * c_len, :] * decay[h]).astype(BF16)
        return carry

    lax.fori_loop(0, bp * n_chunk, intra_chunk, 0)

    def to_lanes(ii, carry):
        for i in [ii * XPOSE_UNROLL + u for u in range(XPOSE_UNROLL)]:
            x = lst_ref[pl.ds(i, n_inst, stride=LST_PITCH), :]
            lt_ref[i] = x.T[0:c_len, :]
        return carry

    lax.fori_loop(0, c_len // XPOSE_UNROLL, to_lanes, 0)
    _unit_lower_inverse_on_lanes(lt_ref, tt_ref, n_inst)

    def from_lanes(ii, carry):
        for i in [ii * XPOSE_UNROLL + u for u in range(XPOSE_UNROLL)]:
            t = tt_ref[i]
            lst_ref[pl.ds(i, n_inst, stride=LST_PITCH), :] = jnp.concatenate([t, jnp.zeros_like(t)], axis=0).T
        return carry

    lax.fori_loop(0, c_len // XPOSE_UNROLL, from_lanes, 0)

    chains = [(bb, h) for bb in range(bp) for h in heads]
    idx = range(len(chains))

    def scan_chunk(c, carry):
        rows = pl.ds(pl.multiple_of(c * c_len, c_len), c_len)
        ccs = [bb * n_chunk + c for bb in range(bp)]
        gct = [gct_ref[cc] for cc in ccs]
        gcc = [gc_ref[bb, rows, :] for bb in range(bp)]
        bct = [bt_ref[cc] for cc in ccs]
        inst = [ccs[bb] * GDN_HEADS + h for bb, h in chains]
        g_row = [gct[bb][h:h + 1, :] for bb, h in chains]
        g_last = [g[:, c_len - 1:c_len] for g in g_row]
        ks = [kn_ref[bb, rows, lanes[h]] for bb, h in chains]
        a_inv = [lst_ref[pl.ds(pl.multiple_of(n * LST_PITCH, SUBLANES), c_len), 0:c_len] for n in inst]
        b_row = [bct[bb][h:h + 1, :] for bb, h in chains]
        u = [_dot((a_inv[x] * b_row[x]).astype(BF16), vn_ref[bb, rows, lanes[h]]) for x, (bb, h) in enumerate(chains)]
        w = [_dot((a_inv[x] * (b_row[x] * jnp.exp(g_row[x]))).astype(BF16), ks[x]) for x in idx]
        s_old = [s_ref[bb * GDN_HEADS + h] for bb, h in chains]
        ws = [_dot(jnp.concatenate([w[x].astype(BF16), qn_ref[bb, rows, lanes[h]]], axis=0),
                   s_old[x].astype(BF16)) for x, (bb, h) in enumerate(chains)]
        v_new = [(u[x] - ws[x][0:c_len, :]).astype(BF16) for x in idx]
        kd_t = [(ks[x].astype(F32).T * jnp.exp(g_last[x] - g_row[x])).astype(BF16) for x in idx]
        for x, (bb, h) in enumerate(chains):
            s_ref[bb * GDN_HEADS + h] = s_old[x] * jnp.exp(g_last[x]) + _dot(kd_t[x], v_new[x])
        o = [jnp.exp(gcc[bb][:, h:h + 1]) * ws[x][c_len:2 * c_len, :] + _dot(qk_ref[inst[x]], v_new[x])
             for x, (bb, h) in enumerate(chains)]
        for x, (bb, h) in enumerate(chains):
            on = o[x] * lax.rsqrt(jnp.mean(o[x] * o[x], axis=-1, keepdims=True) + RMS_EPS) * nw_ref[...]
            zc = z_ref[bb, rows, lanes[h]].astype(F32)
            o_ref[bb, rows, lanes[h]] = (on * _silu(zc)).astype(BF16)
        return carry

    lax.fori_loop(0, n_chunk, scan_chunk, 0)


def _gdn(proj3, ab3, abt2, conv_w, a_log, dt_bias, gdn_norm_w):
    b, t, _ = proj3.shape
    tb = GDN_TB
    gw = GDN_WIDTH
    first = 4 * DA_WIDTH // gw
    bp = GDN_BATCH_PAIR
    n_chunk = tb // GDN_CHUNK
    n_inst = bp * n_chunk * GDN_HEADS
    n_tb = t // tb
    assert bp == 2 and b % bp == 0
    assert n_inst == HEAD_LANES, "one (batch row, chunk, head) matrix per lane"
    return pl.pallas_call(
        functools.partial(_gdn_kernel, tb=tb),
        grid=(b // bp, n_tb),
        in_specs=[
            pl.BlockSpec((bp, tb, gw), lambda bi, ti: (bi, ti, first)),
            pl.BlockSpec((bp, tb, gw), lambda bi, ti: (bi, ti, first + 1)),
            pl.BlockSpec((bp, tb, gw), lambda bi, ti: (bi, ti, first + 2)),
            pl.BlockSpec((bp, tb, gw), lambda bi, ti: (bi, ti, first + 3)),
            pl.BlockSpec((bp, tb, AB_COLS), lambda bi, ti: (bi, ti, 0)),
            pl.BlockSpec((AB_COLS, tb), lambda bi, ti: (0, bp * bi * n_tb + ti)),
            pl.BlockSpec((AB_COLS, tb), lambda bi, ti: (0, (bp * bi + 1) * n_tb + ti)),
            pl.BlockSpec((CONV_K, 3 * gw), lambda bi, ti: (0, 0)),
            pl.BlockSpec((1, GDN_HEADS), lambda bi, ti: (0, 0)),
            pl.BlockSpec((1, GDN_HEADS), lambda bi, ti: (0, 0)),
            pl.BlockSpec((GDN_HEADS, 1), lambda bi, ti: (0, 0)),
            pl.BlockSpec((GDN_HEADS, 1), lambda bi, ti: (0, 0)),
            pl.BlockSpec((1, GDN_HEAD_DIM), lambda bi, ti: (0, 0)),
        ],
        out_specs=pl.BlockSpec((bp, tb, gw), lambda bi, ti: (bi, ti, 0)),
        out_shape=jax.ShapeDtypeStruct((b, t, gw), BF16),
        scratch_shapes=[
            pltpu.VMEM((bp * GDN_HEADS, GDN_HEAD_DIM, GDN_HEAD_DIM), F32),
            pltpu.VMEM((bp * 3, BF16_ROWS + GDN_CHUNK, gw), BF16),
            pltpu.VMEM((bp, tb, gw), BF16),
            pltpu.VMEM((bp, tb, gw), BF16),
            pltpu.VMEM((bp, tb, gw), BF16),
            pltpu.VMEM((bp, tb, GDN_HEADS), F32),
            pltpu.VMEM((bp * n_chunk, GDN_HEADS, GDN_CHUNK), F32),
            pltpu.VMEM((bp, tb, GDN_HEADS), F32),
            pltpu.VMEM((bp * n_chunk, GDN_HEADS, GDN_CHUNK), F32),
            pltpu.VMEM((n_inst * LST_PITCH, HEAD_LANES), F32),
            pltpu.VMEM((GDN_CHUNK, GDN_CHUNK, n_inst), F32),
            pltpu.VMEM((GDN_CHUNK, GDN_CHUNK, n_inst), F32),
            pltpu.VMEM((n_inst, GDN_CHUNK, GDN_CHUNK), BF16),
        ],
        compiler_params=pltpu.CompilerParams(
            dimension_semantics=("parallel", "arbitrary"), vmem_limit_bytes=GDN_VMEM_LIMIT),
        name="gdn",
    )(proj3, proj3, proj3, proj3, ab3, abt2, abt2, conv_w,
      a_log.reshape(1, GDN_HEADS), dt_bias.reshape(1, GDN_HEADS),
      a_log.reshape(GDN_HEADS, 1), dt_bias.reshape(GDN_HEADS, 1),
      gdn_norm_w.reshape(1, GDN_HEAD_DIM))


def _outproj_kernel(oda_ref, ogdn_ref, w_ref, x_ref, fw_ref, o_ref, *, final):
    acc = _dot(oda_ref[...], w_ref[0:DA_WIDTH, :]) + _dot(ogdn_ref[...], w_ref[DA_WIDTH:DA_WIDTH + GDN_WIDTH, :])
    y = x_ref[...] + acc
    if final:
        y = y * lax.rsqrt(jnp.mean(y * y, axis=-1, keepdims=True) + RMS_EPS) * fw_ref[...]
    o_ref[...] = y


def _outproj(o_da2, o_gdn2, w_out_bf16, layer, x2, final_w, final):
    m = x2.shape[0]
    tm = OUTPROJ_TM
    return pl.pallas_call(
        functools.partial(_outproj_kernel, final=final),
        grid=(m // tm,),
        in_specs=[
            pl.BlockSpec((tm, DA_WIDTH), lambda i: (i, 0)),
            pl.BlockSpec((tm, GDN_WIDTH), lambda i: (i, 0)),
            pl.BlockSpec((None, DA_WIDTH + GDN_WIDTH, D_MODEL), lambda i: (layer, 0, 0)),
            pl.BlockSpec((tm, D_MODEL), lambda i: (i, 0)),
            pl.BlockSpec((1, D_MODEL), lambda i: (0, 0)),
        ],
        out_specs=pl.BlockSpec((tm, D_MODEL), lambda i: (i, 0)),
        out_shape=jax.ShapeDtypeStruct((m, D_MODEL), F32),
        compiler_params=pltpu.CompilerParams(
            dimension_semantics=("parallel",), vmem_limit_bytes=VMEM_LIMIT),
        name="outproj",
    )(o_da2, o_gdn2, w_out_bf16, x2, final_w)


def kernel(x, norm_w, w_in, w_out, lambda_q1, lambda_k1, lambda_q2, lambda_k2, da_subln_w, rel_bias,
           conv_w, a_log, dt_bias, gdn_norm_w, final_norm_w):
    b, t, d = x.shape
    m = b * t
    x2 = x.reshape(m, d)
    near_bias = _near_bias(rel_bias)
    final_w = final_norm_w.reshape(1, d)
    w_out_bf16 = w_out.astype(BF16)
    w_in_t = jnp.swapaxes(w_in, 1, 2)
    for l in range(DEPTH):
        lambda_init = 0.8 - 0.6 * math.exp(-0.3 * l)
        proj, ab, abt = _inproj(x2, norm_w[l].reshape(1, d), w_in_t, l)
        proj3 = proj.reshape(b, t, MAIN_COLS)
        lam_params = jnp.stack([lambda_q1[l], lambda_k1[l], lambda_q2[l], lambda_k2[l]])
        o_da = _attention(proj3, near_bias, lam_params, da_subln_w[l].reshape(1, 2 * DA_HEAD_DIM), lambda_init)
        o_gdn = _gdn(proj3, ab.reshape(b, t, AB_COLS), abt, conv_w[l], a_log[l], dt_bias[l], gdn_norm_w[l])
        x2 = _outproj(o_da.reshape(m, DA_WIDTH), o_gdn.reshape(m, GDN_WIDTH), w_out_bf16, l,
                      x2, final_w, final=(l == DEPTH - 1))
    return x2.reshape(b, t, d)
```

```python
import functools
import math

import jax
import jax.numpy as jnp
from jax import lax
from jax.experimental import pallas as pl
from jax.experimental.pallas import tpu as pltpu

F32 = jnp.float32
BF16 = jnp.bfloat16

D_MODEL = 2048
DEPTH = 2
DA_HEADS = 8
DA_HEAD_DIM = 64
DA_WIDTH = DA_HEADS * 2 * DA_HEAD_DIM
GDN_HEADS = 8
GDN_HEAD_DIM = 128
GDN_WIDTH = GDN_HEADS * GDN_HEAD_DIM
CONV_K = 4
GDN_CHUNK = 64
REL_BUCKETS = 32
REL_MAX_DIST = 128
REL_MAX_EXACT = REL_BUCKETS // 2
RMS_EPS = 1e-6
NEG_INF = -1e30
MAIN_COLS = 4 * DA_WIDTH + 4 * GDN_WIDTH
AB_COLS = 2 * GDN_HEADS
HEAD_LANES = 128
BF16_ROWS = 16
LOG2_E = math.log2(math.e)

INPROJ_TM = 1024
INPROJ_TN = 1024
ATTN_TQ = 512
ATTN_TK = 512
ATTN_LANE_TILE = 256
ATTN_HEADS_PER_STEP = 2
BIAS_BLOCK = (64, 128)
GDN_TB = 512
GDN_BATCH_PAIR = 2
OUTPROJ_TM = 512
VMEM_LIMIT = 48 * 1024 * 1024
GDN_VMEM_LIMIT = 56 * 1024 * 1024

NT_DIMS = (((1,), (1,)), ((), ()))


def _sigmoid(x):
    return 1.0 / (1.0 + jnp.exp(-x))


def _silu(x):
    h = 0.5 * x
    return h + h * jnp.tanh(h)


def _softplus(x):
    return jnp.maximum(x, 0.0) + jnp.log(1.0 + jnp.exp(-jnp.abs(x)))


def _dot(a, b):
    return jnp.dot(a, b, preferred_element_type=F32)


def _split_hi_lo(a):
    hi = a.astype(BF16)
    lo = (a - hi.astype(F32)).astype(BF16)
    return hi, lo


def _dot_split(a, b):
    a_hi, a_lo = _split_hi_lo(a)
    b_hi, b_lo = _split_hi_lo(b)
    return _dot(a_hi, b_hi) + _dot(a_lo, b_hi) + _dot(a_hi, b_lo)


def _inproj_kernel(x_ref, nw_ref, w_ref, wab_ref, proj_ref, ab_ref, abt_ref, h_ref):
    @pl.when(pl.program_id(1) == 0)
    def _():
        x = x_ref[...]
        ms = jnp.mean(x * x, axis=-1, keepdims=True)
        h = (x * lax.rsqrt(ms + RMS_EPS) * nw_ref[...]).astype(BF16)
        h_ref[...] = h
        w_ab = jnp.concatenate(
            [wab_ref[...].astype(BF16), jnp.zeros((HEAD_LANES - AB_COLS, D_MODEL), BF16)], axis=0)
        ab = lax.dot_general(h, w_ab, NT_DIMS, preferred_element_type=F32)
        ab_ref[...] = ab[:, 0:AB_COLS]
        abt_ref[...] = ab.T[0:AB_COLS, :]

    proj_ref[...] = lax.dot_general(h_ref[...], w_ref[...].astype(BF16), NT_DIMS,
                                    preferred_element_type=F32).astype(BF16)


def _inproj(x2, norm_w, w_in_t, layer):
    m = x2.shape[0]
    tm, tn = INPROJ_TM, INPROJ_TN
    return pl.pallas_call(
        _inproj_kernel,
        grid=(m // tm, MAIN_COLS // tn),
        in_specs=[
            pl.BlockSpec((tm, D_MODEL), lambda i, j: (i, 0)),
            pl.BlockSpec((1, D_MODEL), lambda i, j: (0, 0)),
            pl.BlockSpec((None, tn, D_MODEL), lambda i, j: (layer, j, 0)),
            pl.BlockSpec((None, AB_COLS, D_MODEL), lambda i, j: (layer, MAIN_COLS // AB_COLS, 0)),
        ],
        out_specs=[
            pl.BlockSpec((tm, tn), lambda i, j: (i, j)),
            pl.BlockSpec((tm, AB_COLS), lambda i, j: (i, 0)),
            pl.BlockSpec((AB_COLS, tm), lambda i, j: (0, i)),
        ],
        out_shape=[
            jax.ShapeDtypeStruct((m, MAIN_COLS), BF16),
            jax.ShapeDtypeStruct((m, AB_COLS), F32),
            jax.ShapeDtypeStruct((AB_COLS, m), F32),
        ],
        scratch_shapes=[pltpu.VMEM((tm, D_MODEL), BF16)],
        compiler_params=pltpu.CompilerParams(
            dimension_semantics=("parallel", "arbitrary"), vmem_limit_bytes=GDN_VMEM_LIMIT),
        name="inproj",
    )(x2, norm_w, w_in_t, w_in_t)


def _bias_kernel(rb_ref, o_ref, *, tk):
    h = pl.program_id(0)
    far = rb_ref[REL_BUCKETS - 1, h]
    br, bc = BIAS_BLOCK
    shape = (br, bc)
    for rb in range(o_ref.shape[0] // br):
        for cb in range(o_ref.shape[1] // bc):
            blk = (slice(rb * br, (rb + 1) * br), slice(cb * bc, (cb + 1) * bc))
            d_min = cb * bc - (rb * br + br - 1) + tk
            d_max = cb * bc + bc - 1 - rb * br + tk
            if d_max < 0:
                o_ref[blk] = jnp.full(shape, NEG_INF, F32)
            elif d_min >= REL_MAX_DIST:
                o_ref[blk] = jnp.zeros(shape, F32)
            else:
                c = lax.broadcasted_iota(jnp.int32, shape, 0) + rb * br
                r = lax.broadcasted_iota(jnp.int32, shape, 1) + cb * bc
                dist = r - c + tk
                n = jnp.maximum(dist, 0)
                nf = jnp.maximum(n, REL_MAX_EXACT).astype(F32)
                large = REL_MAX_EXACT + (jnp.log(nf / REL_MAX_EXACT) / math.log(REL_MAX_DIST / REL_MAX_EXACT)
                                         * (REL_BUCKETS - REL_MAX_EXACT)).astype(jnp.int32)
                large = jnp.minimum(large, REL_BUCKETS - 1)
                bucket = jnp.where(n < REL_MAX_EXACT, n, large)
                val = jnp.zeros(shape, F32)
                for b in range(REL_BUCKETS - 1):
                    val = jnp.where(bucket == b, rb_ref[b, h] - far, val)
                o_ref[blk] = jnp.where(dist >= 0, val * LOG2_E, NEG_INF)


def _near_bias(rel_bias):
    tq, tk = ATTN_TQ, ATTN_TK
    return pl.pallas_call(
        functools.partial(_bias_kernel, tk=tk),
        grid=(DA_HEADS,),
        in_specs=[pl.BlockSpec(memory_space=pltpu.SMEM)],
        out_specs=pl.BlockSpec((None, 2 * tk, tq), lambda h: (h, 0, 0)),
        out_shape=jax.ShapeDtypeStruct((DA_HEADS, 2 * tk, tq), F32),
        name="near_bias",
    )(rel_bias)


def _attn_kernel(q_ref, qnext_ref, k_ref, v_ref, g_ref, bias_ref, lam_ref, sw_ref, o_ref,
                 qz_ref, vt_ref, m_ref, acc_ref, sa_ref, sb_ref, *, tq, tk, lambda_init):
    i = pl.program_id(2)
    hl = HEAD_LANES
    lt = ATTN_LANE_TILE
    hp = ATTN_HEADS_PER_STEP
    n_tiles = 2 * tq // lt
    q_tiles = tq // lt
    head_cols = [slice(hd * hl, (hd + 1) * hl) for hd in range(hp)]

    units = [(hd, t) for hd in range(hp) for t in range(n_tiles)]
    cols = [slice(t * lt, (t + 1) * lt) for t in range(n_tiles)]

    def set_queries(src_ref):
        for hd in range(hp):
            q = src_ref[:, head_cols[hd]] * (DA_HEAD_DIM ** -0.5 * LOG2_E)
            lane = lax.broadcasted_iota(jnp.int32, q.shape, 1)
            zero = jnp.zeros_like(q)
            qz_ref[hd, 0:tq, :] = jnp.where(lane < DA_HEAD_DIM, q, zero)
            qz_ref[hd, tq:2 * tq, :] = jnp.where(lane >= DA_HEAD_DIM, q, zero)

    def keys_used(t, diag):
        return (t % q_tiles + 1) * lt if diag else tk

    def scores(j, s_ref, diag=False):
        row0 = pl.multiple_of(j * tk, tk)
        for u, (hd, t) in enumerate(units):
            nk = keys_used(t, diag)
            kj = k_ref[pl.ds(row0, nk), head_cols[hd]]
            s_ref[u, 0:nk, :] = lax.dot_general(kj, qz_ref[hd, cols[t], :], NT_DIMS, preferred_element_type=F32)

    @pl.when(i == 0)
    def _():
        ones = jnp.ones((vt_ref.shape[2] - hl, tk), BF16)
        for hd in range(hp):
            for j in range(vt_ref.shape[1]):
                vt_ref[hd, j, 0:hl, :] = v_ref[j * tk:(j + 1) * tk, head_cols[hd]].astype(F32).T.astype(BF16)
                vt_ref[hd, j, hl:, :] = ones
        set_queries(q_ref)
        scores(0, sa_ref)

    m_ref[...] = jnp.full(m_ref.shape, NEG_INF, F32)
    acc_ref[...] = jnp.zeros(acc_ref.shape, F32)

    def softmax_pv(j, s_ref, bias_row0, bias_valid=None, diag=False):
        nk = [keys_used(t, diag) for _, t in units]
        s = [s_ref[u, 0:nk[u], :] for u in range(len(units))]
        if bias_row0 is not None:
            bias = [bias_ref[hd, bias_row0:bias_row0 + nk[u], cols[t % q_tiles]] for u, (hd, t) in enumerate(units)]
            if bias_valid is not None:
                bias = [jnp.where(bias_valid, b, NEG_INF) for b in bias]
            s = [a + b for a, b in zip(s, bias)]
        m_prev = [m_ref[hd, :, cols[t]] for hd, t in units]
        m_new = [jnp.maximum(mp, jnp.max(a, axis=0, keepdims=True)) for mp, a in zip(m_prev, s)]
        p = [jnp.exp2(a - mn).astype(BF16) for a, mn in zip(s, m_new)]
        pv = [_dot(vt_ref[hd, j, :, 0:nk[u]], p[u]) for u, (hd, t) in enumerate(units)]
        for u, (hd, t) in enumerate(units):
            acc_ref[hd, :, cols[t]] = jnp.exp2(m_prev[u] - m_new[u]) * acc_ref[hd, :, cols[t]] + pv[u]
            m_ref[hd, :, cols[t]] = m_new[u]

    n_far = jnp.maximum(i - 1, 0)
    odd = n_far % 2

    @pl.when(odd == 1)
    def _():
        softmax_pv(0, sa_ref, None)
        scores(1, sa_ref)

    def far_pair(jj, carry):
        j = odd + 2 * jj
        scores(j + 1, sb_ref)
        softmax_pv(j, sa_ref, None)
        scores(j + 2, sa_ref)
        softmax_pv(j + 1, sb_ref, None)
        return carry

    lax.fori_loop(0, n_far // 2, far_pair, 0)
    scores(i, sb_ref, diag=True)
    softmax_pv(n_far, sa_ref, 0, bias_valid=i > 0)
    set_queries(qnext_ref)
    scores(0, sa_ref)
    softmax_pv(i, sb_ref, tk, diag=True)

    lp = lam_ref[...]
    s1 = jnp.sum(lp[0:1, :] * lp[1:2, :], axis=-1, keepdims=True)
    s2 = jnp.sum(lp[2:3, :] * lp[3:4, :], axis=-1, keepdims=True)
    lam = jnp.exp(s1) - jnp.exp(s2) + lambda_init
    for hd in range(hp):
        inv_l = 1.0 / acc_ref[hd, hl:hl + 1, :]
        o_t = (acc_ref[hd, 0:hl, 0:tq] * inv_l[:, 0:tq]
               - lam * (acc_ref[hd, 0:hl, tq:2 * tq] * inv_l[:, tq:2 * tq]))
        o = o_t.T
        ms = jnp.mean(o * o, axis=-1, keepdims=True)
        y = o * lax.rsqrt(ms + RMS_EPS) * sw_ref[...] * (1.0 - lambda_init)
        o_ref[:, head_cols[hd]] = (y * _silu(g_ref[:, head_cols[hd]].astype(F32))).astype(BF16)


def _attention(proj3, near_bias, lam_params, subln_w, lambda_init):
    b, t, _ = proj3.shape
    tq, tk = ATTN_TQ, ATTN_TK
    assert tq == tk and t % tq == 0
    hl = HEAD_LANES
    hp = ATTN_HEADS_PER_STEP
    wide = hp * hl
    sect = DA_WIDTH // wide
    n_units = hp * (2 * tq // ATTN_LANE_TILE)
    return pl.pallas_call(
        functools.partial(_attn_kernel, tq=tq, tk=tk, lambda_init=lambda_init),
        grid=(b, DA_HEADS // hp, t // tq),
        in_specs=[
            pl.BlockSpec((None, tq, wide), lambda bi, h, i: (bi, i, h)),
            pl.BlockSpec((None, tq, wide), lambda bi, h, i: (bi, jnp.minimum(i + 1, t // tq - 1), h)),
            pl.BlockSpec((None, t, wide), lambda bi, h, i: (bi, 0, sect + h)),
            pl.BlockSpec((None, t, wide), lambda bi, h, i: (bi, 0, 2 * sect + h)),
            pl.BlockSpec((None, tq, wide), lambda bi, h, i: (bi, i, 3 * sect + h)),
            pl.BlockSpec((hp, 2 * tk, tq), lambda bi, h, i: (h, 0, 0)),
            pl.BlockSpec((4, DA_HEAD_DIM), lambda bi, h, i: (0, 0)),
            pl.BlockSpec((1, hl), lambda bi, h, i: (0, 0)),
        ],
        out_specs=pl.BlockSpec((None, tq, wide), lambda bi, h, i: (bi, i, h)),
        out_shape=jax.ShapeDtypeStruct((b, t, DA_WIDTH), BF16),
        scratch_shapes=[
            pltpu.VMEM((hp, 2 * tq, hl), BF16),
            pltpu.VMEM((hp, t // tk, hl + BF16_ROWS, tk), BF16),
            pltpu.VMEM((hp, 1, 2 * tq), F32),
            pltpu.VMEM((hp, hl + BF16_ROWS, 2 * tq), F32),
            pltpu.VMEM((n_units, tk, ATTN_LANE_TILE), F32),
            pltpu.VMEM((n_units, tk, ATTN_LANE_TILE), F32),
        ],
        compiler_params=pltpu.CompilerParams(
            dimension_semantics=("parallel", "parallel", "arbitrary"), vmem_limit_bytes=VMEM_LIMIT),
        name="diff_attn",
    )(proj3, proj3, proj3, proj3, proj3, near_bias, lam_params, subln_w)


LST_PITCH = 72
SUBLANES = 8
XPOSE_UNROLL = 4


def _unit_lower_inverse_on_lanes(lt_ref, tt_ref, n_inst):
    c_len = GDN_CHUNK
    sub = lax.broadcasted_iota(jnp.int32, (SUBLANES, n_inst), 0)
    zero = jnp.zeros((SUBLANES, n_inst), F32)
    for i in range(c_len):
        n_blk = (i - 1) // SUBLANES + 1 if i > 0 else 0
        acc = [None] * n_blk
        for k in range(i):
            lik = jnp.broadcast_to(lt_ref[i, k:k + 1, :], (SUBLANES, n_inst))
            for jb in range(k // SUBLANES + 1):
                term = lik * tt_ref[k, jb * SUBLANES:(jb + 1) * SUBLANES, :]
                acc[jb] = term if acc[jb] is None else acc[jb] + term
        for jb in range(c_len // SUBLANES):
            val = -acc[jb] if jb < n_blk else zero
            if jb == i // SUBLANES:
                val = jnp.where(sub == i % SUBLANES, 1.0, val)
            tt_ref[i, jb * SUBLANES:(jb + 1) * SUBLANES, :] = val


def _gdn_kernel(qp_ref, kp_ref, vp_ref, z_ref, ab_ref, abt0_ref, abt1_ref, cw_ref, alog_ref, dtb_ref,
                alogt_ref, dtbt_ref, nw_ref, o_ref,
                s_ref, cbuf_ref, qn_ref, kn_ref, vn_ref, gc_ref, gct_ref, beta_ref, bt_ref,
                lst_ref, lt_ref, tt_ref, qk_ref, *, tb):
    c_len = GDN_CHUNK
    hl = HEAD_LANES
    bp = GDN_BATCH_PAIR
    n_chunk = tb // c_len
    n_inst = bp * n_chunk * GDN_HEADS
    halo = BF16_ROWS
    abt_refs = (abt0_ref, abt1_ref)

    @pl.when(pl.program_id(1) == 0)
    def _():
        s_ref[...] = jnp.zeros(s_ref.shape, F32)
        cbuf_ref[:, 0:halo, :] = jnp.zeros((bp * 3, halo, GDN_WIDTH), BF16)

    sel_r = lax.broadcasted_iota(jnp.int32, ((CONV_K - 1) * c_len, halo + c_len), 0)
    sel_c = lax.broadcasted_iota(jnp.int32, ((CONV_K - 1) * c_len, halo + c_len), 1)
    tap = sel_r // c_len
    shift_mat = (sel_c == sel_r - tap * c_len + halo - (CONV_K - 1) + tap).astype(BF16)

    def conv_chunk(cc, carry):
        bb = cc // n_chunk
        rows = pl.ds(pl.multiple_of((cc - bb * n_chunk) * c_len, c_len), c_len)
        for g, (src, dst) in enumerate(((qp_ref, qn_ref), (kp_ref, kn_ref), (vp_ref, vn_ref))):
            stage = bb * 3 + g
            xb = src[bb, rows, :]
            cbuf_ref[stage, halo:halo + c_len, :] = xb
            w = 0.5 * cw_ref[:, g * GDN_WIDTH:(g + 1) * GDN_WIDTH]
            taps = _dot(shift_mat, cbuf_ref[stage])
            hy = w[0:1, :] * taps[0:c_len, :]
            hy = hy + w[1:2, :] * taps[c_len:2 * c_len, :]
            hy = hy + w[2:3, :] * taps[2 * c_len:3 * c_len, :]
            hy = hy + w[3:4, :] * xb.astype(F32)
            cbuf_ref[stage, 0:halo, :] = xb[c_len - halo:c_len, :]
            y = hy + hy * jnp.tanh(hy)
            for h in range(GDN_HEADS):
                cs = slice(h * hl, (h + 1) * hl)
                yh = y[:, cs]
                if g < 2:
                    inv_norm = lax.rsqrt(jnp.sum(yh * yh, axis=-1, keepdims=True) + RMS_EPS)
                    if g == 0:
                        inv_norm = inv_norm * (GDN_HEAD_DIM ** -0.5)
                    yh = yh * inv_norm
                dst[bb, rows, cs] = yh.astype(BF16)
        return carry

    lax.fori_loop(0, bp * n_chunk, conv_chunk, 0)

    row = lax.broadcasted_iota(jnp.int32, (c_len, c_len), 0)
    col = lax.broadcasted_iota(jnp.int32, (c_len, c_len), 1)
    tril = row >= col
    strict = row > col
    tril_f = tril.astype(F32)
    triu_f = (row <= col).astype(F32)
    for bb in range(bp):
        ab = ab_ref[bb]
        beta_ref[bb] = _sigmoid(ab[:, 0:GDN_HEADS])
        g_raw = -jnp.exp(alog_ref[...]) * _softplus(ab[:, GDN_HEADS:AB_COLS] + dtb_ref[...])
        abt = abt_refs[bb][...]
        beta_t = _sigmoid(abt[0:GDN_HEADS, :])
        g_raw_t = -jnp.exp(alogt_ref[...]) * _softplus(abt[GDN_HEADS:AB_COLS, :] + dtbt_ref[...])
        for c in range(n_chunk):
            cc = bb * n_chunk + c
            gc_ref[bb, c * c_len:(c + 1) * c_len, :] = _dot_split(tril_f, g_raw[c * c_len:(c + 1) * c_len, :])
            gct_ref[cc] = _dot_split(g_raw_t[:, c * c_len:(c + 1) * c_len], triu_f)
            bt_ref[cc] = beta_t[:, c * c_len:(c + 1) * c_len]

    heads = range(GDN_HEADS)
    lanes = [slice(h * hl, (h + 1) * hl) for h in heads]

    def intra_chunk(cc, carry):
        bb = cc // n_chunk
        rows = pl.ds(pl.multiple_of((cc - bb * n_chunk) * c_len, c_len), c_len)
        gct = gct_ref[cc]
        gcc = gc_ref[bb, rows, :]
        betac = beta_ref[bb, rows, :]
        ks = [kn_ref[bb, rows, lanes[h]] for h in heads]
        kq = [jnp.concatenate([ks[h], qn_ref[bb, rows, lanes[h]]], axis=0) for h in heads]
        kk = [lax.dot_general(kq[h], ks[h], NT_DIMS, preferred_element_type=F32) for h in heads]
        decay = [jnp.where(tril, jnp.exp(jnp.where(tril, gcc[:, h:h + 1] - gct[h:h + 1, :], 0.0)), 0.0)
                 for h in heads]
        pad = jnp.zeros((c_len, hl - c_len), F32)
        for h in heads:
            l_strict = jnp.where(strict, kk[h][0:c_len, :] * decay[h] * betac[:, h:h + 1], 0.0)
            base = pl.multiple_of((cc * GDN_HEADS + h) * LST_PITCH, SUBLANES)
            lst_ref[pl.ds(base, c_len), :] = jnp.concatenate([l_strict, pad], axis=1)
            qk_ref[cc * GDN_HEADS + h] = (kk[h][c_len:2 * c_len, :] * decay[h]).astype(BF16)
        return carry

    lax.fori_loop(0, bp * n_chunk, intra_chunk, 0)

    def to_lanes(ii, carry):
        for i in [ii * XPOSE_UNROLL + u for u in range(XPOSE_UNROLL)]:
            x = lst_ref[pl.ds(i, n_inst, stride=LST_PITCH), :]
            lt_ref[i] = x.T[0:c_len, :]
        return carry

    lax.fori_loop(0, c_len // XPOSE_UNROLL, to_lanes, 0)
    _unit_lower_inverse_on_lanes(lt_ref, tt_ref, n_inst)

    def from_lanes(ii, carry):
        for i in [ii * XPOSE_UNROLL + u for u in range(XPOSE_UNROLL)]:
            t = tt_ref[i]
            lst_ref[pl.ds(i, n_inst, stride=LST_PITCH), :] = jnp.concatenate([t, jnp.zeros_like(t)], axis=0).T
        return carry

    lax.fori_loop(0, c_len // XPOSE_UNROLL, from_lanes, 0)

    chains = [(bb, h) for bb in range(bp) for h in heads]
    idx = range(len(chains))

    def scan_chunk(c, carry):
        rows = pl.ds(pl.multiple_of(c * c_len, c_len), c_len)
        ccs = [bb * n_chunk + c for bb in range(bp)]
        gct = [gct_ref[cc] for cc in ccs]
        gcc = [gc_ref[bb, rows, :] for bb in range(bp)]
        bct = [bt_ref[cc] for cc in ccs]
        inst = [ccs[bb] * GDN_HEADS + h for bb, h in chains]
        g_row = [gct[bb][h:h + 1, :] for bb, h in chains]
        g_last = [g[:, c_len - 1:c_len] for g in g_row]
        ks = [kn_ref[bb, rows, lanes[h]] for bb, h in chains]
        a_inv = [lst_ref[pl.ds(pl.multiple_of(n * LST_PITCH, SUBLANES), c_len), 0:c_len] for n in inst]
        b_row = [bct[bb][h:h + 1, :] for bb, h in chains]
        u = [_dot((a_inv[x] * b_row[x]).astype(BF16), vn_ref[bb, rows, lanes[h]]) for x, (bb, h) in enumerate(chains)]
        w = [_dot((a_inv[x] * (b_row[x] * jnp.exp(g_row[x]))).astype(BF16), ks[x]) for x in idx]
        s_old = [s_ref[bb * GDN_HEADS + h] for bb, h in chains]
        ws = [_dot(jnp.concatenate([w[x].astype(BF16), qn_ref[bb, rows, lanes[h]]], axis=0),
                   s_old[x].astype(BF16)) for x, (bb, h) in enumerate(chains)]
        v_new = [(u[x] - ws[x][0:c_len, :]).astype(BF16) for x in idx]
        kd_t = [(ks[x].astype(F32).T * jnp.exp(g_last[x] - g_row[x])).astype(BF16) for x in idx]
        for x, (bb, h) in enumerate(chains):
            s_ref[bb * GDN_HEADS + h] = s_old[x] * jnp.exp(g_last[x]) + _dot(kd_t[x], v_new[x])
        o = [jnp.exp(gcc[bb][:, h:h + 1]) * ws[x][c_len:2 * c_len, :] + _dot(qk_ref[inst[x]], v_new[x])
             for x, (bb, h) in enumerate(chains)]
        for x, (bb, h) in enumerate(chains):
            on = o[x] * lax.rsqrt(jnp.mean(o[x] * o[x], axis=-1, keepdims=True) + RMS_EPS) * nw_ref[...]
            zc = z_ref[bb, rows, lanes[h]].astype(F32)
            o_ref[bb, rows, lanes[h]] = (on * _silu(zc)).astype(BF16)
        return carry

    lax.fori_loop(0, n_chunk, scan_chunk, 0)


def _gdn(proj3, ab3, abt2, conv_w, a_log, dt_bias, gdn_norm_w):
    b, t, _ = proj3.shape
    tb = GDN_TB
    gw = GDN_WIDTH
    first = 4 * DA_WIDTH // gw
    bp = GDN_BATCH_PAIR
    n_chunk = tb // GDN_CHUNK
    n_inst = bp * n_chunk * GDN_HEADS
    n_tb = t // tb
    assert bp == 2 and b % bp == 0
    assert n_inst == HEAD_LANES, "one (batch row, chunk, head) matrix per lane"
    return pl.pallas_call(
        functools.partial(_gdn_kernel, tb=tb),
        grid=(b // bp, n_tb),
        in_specs=[
            pl.BlockSpec((bp, tb, gw), lambda bi, ti: (bi, ti, first)),
            pl.BlockSpec((bp, tb, gw), lambda bi, ti: (bi, ti, first + 1)),
            pl.BlockSpec((bp, tb, gw), lambda bi, ti: (bi, ti, first + 2)),
            pl.BlockSpec((bp, tb, gw), lambda bi, ti: (bi, ti, first + 3)),
            pl.BlockSpec((bp, tb, AB_COLS), lambda bi, ti: (bi, ti, 0)),
            pl.BlockSpec((AB_COLS, tb), lambda bi, ti: (0, bp * bi * n_tb + ti)),
            pl.BlockSpec((AB_COLS, tb), lambda bi, ti: (0, (bp * bi + 1) * n_tb + ti)),
            pl.BlockSpec((CONV_K, 3 * gw), lambda bi, ti: (0, 0)),
            pl.BlockSpec((1, GDN_HEADS), lambda bi, ti: (0, 0)),
            pl.BlockSpec((1, GDN_HEADS), lambda bi, ti: (0, 0)),
            pl.BlockSpec((GDN_HEADS, 1), lambda bi, ti: (0, 0)),
            pl.BlockSpec((GDN_HEADS, 1), lambda bi, ti: (0, 0)),
            pl.BlockSpec((1, GDN_HEAD_DIM), lambda bi, ti: (0, 0)),
        ],
        out_specs=pl.BlockSpec((bp, tb, gw), lambda bi, ti: (bi, ti, 0)),
        out_shape=jax.ShapeDtypeStruct((b, t, gw), BF16),
        scratch_shapes=[
            pltpu.VMEM((bp * GDN_HEADS, GDN_HEAD_DIM, GDN_HEAD_DIM), F32),
            pltpu.VMEM((bp * 3, BF16_ROWS + GDN_CHUNK, gw), BF16),
            pltpu.VMEM((bp, tb, gw), BF16),
            pltpu.VMEM((bp, tb, gw), BF16),
            pltpu.VMEM((bp, tb, gw), BF16),
            pltpu.VMEM((bp, tb, GDN_HEADS), F32),
            pltpu.VMEM((bp * n_chunk, GDN_HEADS, GDN_CHUNK), F32),
            pltpu.VMEM((bp, tb, GDN_HEADS), F32),
            pltpu.VMEM((bp * n_chunk, GDN_HEADS, GDN_CHUNK), F32),
            pltpu.VMEM((n_inst * LST_PITCH, HEAD_LANES), F32),
            pltpu.VMEM((GDN_CHUNK, GDN_CHUNK, n_inst), F32),
            pltpu.VMEM((GDN_CHUNK, GDN_CHUNK, n_inst), F32),
            pltpu.VMEM((n_inst, GDN_CHUNK, GDN_CHUNK), BF16),
        ],
        compiler_params=pltpu.CompilerParams(
            dimension_semantics=("parallel", "arbitrary"), vmem_limit_bytes=GDN_VMEM_LIMIT),
        name="gdn",
    )(proj3, proj3, proj3, proj3, ab3, abt2, abt2, conv_w,
      a_log.reshape(1, GDN_HEADS), dt_bias.reshape(1, GDN_HEADS),
      a_log.reshape(GDN_HEADS, 1), dt_bias.reshape(GDN_HEADS, 1),
      gdn_norm_w.reshape(1, GDN_HEAD_DIM))


def _outproj_kernel(oda_ref, ogdn_ref, w_ref, x_ref, fw_ref, o_ref, *, final):
    acc = _dot(oda_ref[...], w_ref[0:DA_WIDTH, :]) + _dot(ogdn_ref[...], w_ref[DA_WIDTH:DA_WIDTH + GDN_WIDTH, :])
    y = x_ref[...] + acc
    if final:
        y = y * lax.rsqrt(jnp.mean(y * y, axis=-1, keepdims=True) + RMS_EPS) * fw_ref[...]
    o_ref[...] = y


def _outproj(o_da2, o_gdn2, w_out_bf16, layer, x2, final_w, final):
    m = x2.shape[0]
    tm = OUTPROJ_TM
    return pl.pallas_call(
        functools.partial(_outproj_kernel, final=final),
        grid=(m // tm,),
        in_specs=[
            pl.BlockSpec((tm, DA_WIDTH), lambda i: (i, 0)),
            pl.BlockSpec((tm, GDN_WIDTH), lambda i: (i, 0)),
            pl.BlockSpec((None, DA_WIDTH + GDN_WIDTH, D_MODEL), lambda i: (layer, 0, 0)),
            pl.BlockSpec((tm, D_MODEL), lambda i: (i, 0)),
            pl.BlockSpec((1, D_MODEL), lambda i: (0, 0)),
        ],
        out_specs=pl.BlockSpec((tm, D_MODEL), lambda i: (i, 0)),
        out_shape=jax.ShapeDtypeStruct((m, D_MODEL), F32),
        compiler_params=pltpu.CompilerParams(
            dimension_semantics=("parallel",), vmem_limit_bytes=VMEM_LIMIT),
        name="outproj",
    )(o_da2, o_gdn2, w_out_bf16, x2, final_w)


def kernel(x, norm_w, w_in, w_out, lambda_q1, lambda_k1, lambda_q2, lambda_k2, da_subln_w, rel_bias,
           conv_w, a_log, dt_bias, gdn_norm_w, final_norm_w):
    b, t, d = x.shape
    m = b * t
    x2 = x.reshape(m, d)
    near_bias = _near_bias(rel_bias)
    final_w = final_norm_w.reshape(1, d)
    w_out_bf16 = w_out.astype(BF16)
    w_in_t = jnp.swapaxes(w_in, 1, 2)
    for l in range(DEPTH):
        lambda_init = 0.8 - 0.6 * math.exp(-0.3 * l)
        proj, ab, abt = _inproj(x2, norm_w[l].reshape(1, d), w_in_t, l)
        proj3 = proj.reshape(b, t, MAIN_COLS)
        lam_params = jnp.stack([lambda_q1[l], lambda_k1[l], lambda_q2[l], lambda_k2[l]])
        o_da = _attention(proj3, near_bias, lam_params, da_subln_w[l].reshape(1, 2 * DA_HEAD_DIM), lambda_init)
        o_gdn = _gdn(proj3, ab.reshape(b, t, AB_COLS), abt, conv_w[l], a_log[l], dt_bias[l], gdn_norm_w[l])
        x2 = _outproj(o_da.reshape(m, DA_WIDTH), o_gdn.reshape(m, GDN_WIDTH), w_out_bf16, l,
                      x2, final_w, final=(l == DEPTH - 1))
    return x2.reshape(b, t, d)
```

```python
import functools
import math

import jax
import jax.numpy as jnp
from jax import lax
from jax.experimental import pallas as pl
from jax.experimental.pallas import tpu as pltpu

F32 = jnp.float32
BF16 = jnp.bfloat16

D_MODEL = 2048
DEPTH = 2
DA_HEADS = 8
DA_HEAD_DIM = 64
DA_WIDTH = DA_HEADS * 2 * DA_HEAD_DIM
GDN_HEADS = 8
GDN_HEAD_DIM = 128
GDN_WIDTH = GDN_HEADS * GDN_HEAD_DIM
CONV_K = 4
GDN_CHUNK = 64
REL_BUCKETS = 32
REL_MAX_DIST = 128
REL_MAX_EXACT = REL_BUCKETS // 2
RMS_EPS = 1e-6
NEG_INF = -1e30
MAIN_COLS = 4 * DA_WIDTH + 4 * GDN_WIDTH
AB_COLS = 2 * GDN_HEADS
HEAD_LANES = 128
BF16_ROWS = 16
LOG2_E = math.log2(math.e)

INPROJ_TM = 1024
INPROJ_TN = 1024
ATTN_TQ = 512
ATTN_TK = 512
ATTN_LANE_TILE = 256
ATTN_HEADS_PER_STEP = 2
BIAS_BLOCK = (64, 128)
GDN_TB = 512
GDN_BATCH_PAIR = 2
OUTPROJ_TM = 512
VMEM_LIMIT = 48 * 1024 * 1024
GDN_VMEM_LIMIT = 56 * 1024 * 1024

NT_DIMS = (((1,), (1,)), ((), ()))


def _sigmoid(x):
    return 1.0 / (1.0 + jnp.exp(-x))


def _silu(x):
    h = 0.5 * x
    return h + h * jnp.tanh(h)


def _softplus(x):
    return jnp.maximum(x, 0.0) + jnp.log(1.0 + jnp.exp(-jnp.abs(x)))


def _dot(a, b):
    return jnp.dot(a, b, preferred_element_type=F32)


def _split_hi_lo(a):
    hi = a.astype(BF16)
    lo = (a - hi.astype(F32)).astype(BF16)
    return hi, lo


def _dot_split(a, b):
    a_hi, a_lo = _split_hi_lo(a)
    b_hi, b_lo = _split_hi_lo(b)
    return _dot(a_hi, b_hi) + _dot(a_lo, b_hi) + _dot(a_hi, b_lo)


def _inproj_kernel(x_ref, nw_ref, w_ref, wab_ref, proj_ref, ab_ref, abt_ref, h_ref):
    @pl.when(pl.program_id(1) == 0)
    def _():
        x = x_ref[...]
        ms = jnp.mean(x * x, axis=-1, keepdims=True)
        h = (x * lax.rsqrt(ms + RMS_EPS) * nw_ref[...]).astype(BF16)
        h_ref[...] = h
        w_ab = jnp.concatenate(
            [wab_ref[...].astype(BF16), jnp.zeros((HEAD_LANES - AB_COLS, D_MODEL), BF16)], axis=0)
        ab = lax.dot_general(h, w_ab, NT_DIMS, preferred_element_type=F32)
        ab_ref[...] = ab[:, 0:AB_COLS]
        abt_ref[...] = ab.T[0:AB_COLS, :]

    proj_ref[...] = lax.dot_general(h_ref[...], w_ref[...].astype(BF16), NT_DIMS,
                                    preferred_element_type=F32).astype(BF16)


def _inproj(x2, norm_w, w_in_t, layer):
    m = x2.shape[0]
    tm, tn = INPROJ_TM, INPROJ_TN
    return pl.pallas_call(
        _inproj_kernel,
        grid=(m // tm, MAIN_COLS // tn),
        in_specs=[
            pl.BlockSpec((tm, D_MODEL), lambda i, j: (i, 0)),
            pl.BlockSpec((1, D_MODEL), lambda i, j: (0, 0)),
            pl.BlockSpec((None, tn, D_MODEL), lambda i, j: (layer, j, 0)),
            pl.BlockSpec((None, AB_COLS, D_MODEL), lambda i, j: (layer, MAIN_COLS // AB_COLS, 0)),
        ],
        out_specs=[
            pl.BlockSpec((tm, tn), lambda i, j: (i, j)),
            pl.BlockSpec((tm, AB_COLS), lambda i, j: (i, 0)),
            pl.BlockSpec((AB_COLS, tm), lambda i, j: (0, i)),
        ],
        out_shape=[
            jax.ShapeDtypeStruct((m, MAIN_COLS), BF16),
            jax.ShapeDtypeStruct((m, AB_COLS), F32),
            jax.ShapeDtypeStruct((AB_COLS, m), F32),
        ],
        scratch_shapes=[pltpu.VMEM((tm, D_MODEL), BF16)],
        compiler_params=pltpu.CompilerParams(
            dimension_semantics=("parallel", "arbitrary"), vmem_limit_bytes=GDN_VMEM_LIMIT),
        name="inproj",
    )(x2, norm_w, w_in_t, w_in_t)


def _bias_kernel(rb_ref, o_ref, *, tk):
    h = pl.program_id(0)
    far = rb_ref[REL_BUCKETS - 1, h]
    br, bc = BIAS_BLOCK
    shape = (br, bc)
    for rb in range(o_ref.shape[0] // br):
        for cb in range(o_ref.shape[1] // bc):
            blk = (slice(rb * br, (rb + 1) * br), slice(cb * bc, (cb + 1) * bc))
            d_min = cb * bc - (rb * br + br - 1) + tk
            d_max = cb * bc + bc - 1 - rb * br + tk
            if d_max < 0:
                o_ref[blk] = jnp.full(shape, NEG_INF, F32)
            elif d_min >= REL_MAX_DIST:
                o_ref[blk] = jnp.zeros(shape, F32)
            else:
                c = lax.broadcasted_iota(jnp.int32, shape, 0) + rb * br
                r = lax.broadcasted_iota(jnp.int32, shape, 1) + cb * bc
                dist = r - c + tk
                n = jnp.maximum(dist, 0)
                nf = jnp.maximum(n, REL_MAX_EXACT).astype(F32)
                large = REL_MAX_EXACT + (jnp.log(nf / REL_MAX_EXACT) / math.log(REL_MAX_DIST / REL_MAX_EXACT)
                                         * (REL_BUCKETS - REL_MAX_EXACT)).astype(jnp.int32)
                large = jnp.minimum(large, REL_BUCKETS - 1)
                bucket = jnp.where(n < REL_MAX_EXACT, n, large)
                val = jnp.zeros(shape, F32)
                for b in range(REL_BUCKETS - 1):
                    val = jnp.where(bucket == b, rb_ref[b, h] - far, val)
                o_ref[blk] = jnp.where(dist >= 0, val * LOG2_E, NEG_INF)


def _near_bias(rel_bias):
    tq, tk = ATTN_TQ, ATTN_TK
    return pl.pallas_call(
        functools.partial(_bias_kernel, tk=tk),
        grid=(DA_HEADS,),
        in_specs=[pl.BlockSpec(memory_space=pltpu.SMEM)],
        out_specs=pl.BlockSpec((None, 2 * tk, tq), lambda h: (h, 0, 0)),
        out_shape=jax.ShapeDtypeStruct((DA_HEADS, 2 * tk, tq), F32),
        name="near_bias",
    )(rel_bias)


def _attn_kernel(q_ref, qnext_ref, k_ref, v_ref, g_ref, bias_ref, lam_ref, sw_ref, o_ref,
                 qz_ref, vt_ref, m_ref, acc_ref, sa_ref, sb_ref, *, tq, tk, lambda_init):
    i = pl.program_id(2)
    hl = HEAD_LANES
    lt = ATTN_LANE_TILE
    hp = ATTN_HEADS_PER_STEP
    n_tiles = 2 * tq // lt
    q_tiles = tq // lt
    head_cols = [slice(hd * hl, (hd + 1) * hl) for hd in range(hp)]

    units = [(hd, t) for hd in range(hp) for t in range(n_tiles)]
    cols = [slice(t * lt, (t + 1) * lt) for t in range(n_tiles)]

    def set_queries(src_ref):
        for hd in range(hp):
            q = src_ref[:, head_cols[hd]] * (DA_HEAD_DIM ** -0.5 * LOG2_E)
            lane = lax.broadcasted_iota(jnp.int32, q.shape, 1)
            zero = jnp.zeros_like(q)
            qz_ref[hd, 0:tq, :] = jnp.where(lane < DA_HEAD_DIM, q, zero)
            qz_ref[hd, tq:2 * tq, :] = jnp.where(lane >= DA_HEAD_DIM, q, zero)

    def keys_used(t, diag):
        return (t % q_tiles + 1) * lt if diag else tk

    def scores(j, s_ref, diag=False):
        row0 = pl.multiple_of(j * tk, tk)
        for u, (hd, t) in enumerate(units):
            nk = keys_used(t, diag)
            kj = k_ref[pl.ds(row0, nk), head_cols[hd]]
            s_ref[u, 0:nk, :] = lax.dot_general(kj, qz_ref[hd, cols[t], :], NT_DIMS, preferred_element_type=F32)

    @pl.when(i == 0)
    def _():
        ones = jnp.ones((vt_ref.shape[2] - hl, tk), BF16)
        for hd in range(hp):
            for j in range(vt_ref.shape[1]):
                vt_ref[hd, j, 0:hl, :] = v_ref[j * tk:(j + 1) * tk, head_cols[hd]].astype(F32).T.astype(BF16)
                vt_ref[hd, j, hl:, :] = ones
        set_queries(q_ref)
        scores(0, sa_ref)

    m_ref[...] = jnp.full(m_ref.shape, NEG_INF, F32)
    acc_ref[...] = jnp.zeros(acc_ref.shape, F32)

    def add_bias(a, hd, t, tile_row0, first_row):
        n_rows = a.shape[0]
        if first_row >= n_rows:
            return a
        b = bias_ref[hd, tile_row0 + first_row:tile_row0 + n_rows, cols[t % q_tiles]]
        if first_row == 0:
            return a + b
        return jnp.concatenate([a[0:first_row, :], a[first_row:, :] + b], axis=0)

    def softmax_pv(j, s_ref, kind="far"):
        diag = kind == "diag"
        nk = [keys_used(t, diag) for _, t in units]
        s = [s_ref[u, 0:nk[u], :] for u in range(len(units))]
        if kind == "prev":
            s = [add_bias(s[u], hd, t, 0, tk - REL_MAX_DIST) if t % q_tiles == 0 else s[u]
                 for u, (hd, t) in enumerate(units)]
        elif diag:
            s = [add_bias(s[u], hd, t, tk, max((t % q_tiles) * lt - REL_MAX_DIST, 0))
                 for u, (hd, t) in enumerate(units)]
        m_prev = [m_ref[hd, :, cols[t]] for hd, t in units]
        m_new = [jnp.maximum(mp, jnp.max(a, axis=0, keepdims=True)) for mp, a in zip(m_prev, s)]
        p = [jnp.exp2(a - mn).astype(BF16) for a, mn in zip(s, m_new)]
        pv = [_dot(vt_ref[hd, j, :, 0:nk[u]], p[u]) for u, (hd, t) in enumerate(units)]
        for u, (hd, t) in enumerate(units):
            acc_ref[hd, :, cols[t]] = jnp.exp2(m_prev[u] - m_new[u]) * acc_ref[hd, :, cols[t]] + pv[u]
            m_ref[hd, :, cols[t]] = m_new[u]

    n_far = jnp.maximum(i - 1, 0)
    odd = n_far % 2

    @pl.when(odd == 1)
    def _():
        softmax_pv(0, sa_ref)
        scores(1, sa_ref)

    def far_pair(jj, carry):
        j = odd + 2 * jj
        scores(j + 1, sb_ref)
        softmax_pv(j, sa_ref)
        scores(j + 2, sa_ref)
        softmax_pv(j + 1, sb_ref)
        return carry

    lax.fori_loop(0, n_far // 2, far_pair, 0)
    scores(i, sb_ref, diag=True)
    softmax_pv(n_far, sa_ref, "prev")
    has_prev = i > 0
    m_ref[...] = jnp.where(has_prev, m_ref[...], NEG_INF)
    acc_ref[...] = jnp.where(has_prev, acc_ref[...], 0.0)
    set_queries(qnext_ref)
    scores(0, sa_ref)
    softmax_pv(i, sb_ref, "diag")

    lp = lam_ref[...]
    s1 = jnp.sum(lp[0:1, :] * lp[1:2, :], axis=-1, keepdims=True)
    s2 = jnp.sum(lp[2:3, :] * lp[3:4, :], axis=-1, keepdims=True)
    lam = jnp.exp(s1) - jnp.exp(s2) + lambda_init
    for hd in range(hp):
        inv_l = 1.0 / acc_ref[hd, hl:hl + 1, :]
        o_t = (acc_ref[hd, 0:hl, 0:tq] * inv_l[:, 0:tq]
               - lam * (acc_ref[hd, 0:hl, tq:2 * tq] * inv_l[:, tq:2 * tq]))
        o = o_t.T
        ms = jnp.mean(o * o, axis=-1, keepdims=True)
        y = o * lax.rsqrt(ms + RMS_EPS) * sw_ref[...] * (1.0 - lambda_init)
        o_ref[:, head_cols[hd]] = (y * _silu(g_ref[:, head_cols[hd]].astype(F32))).astype(BF16)


def _attention(proj3, near_bias, lam_params, subln_w, lambda_init):
    b, t, _ = proj3.shape
    tq, tk = ATTN_TQ, ATTN_TK
    assert tq == tk and t % tq == 0
    hl = HEAD_LANES
    hp = ATTN_HEADS_PER_STEP
    wide = hp * hl
    sect = DA_WIDTH // wide
    n_units = hp * (2 * tq // ATTN_LANE_TILE)
    return pl.pallas_call(
        functools.partial(_attn_kernel, tq=tq, tk=tk, lambda_init=lambda_init),
        grid=(b, DA_HEADS // hp, t // tq),
        in_specs=[
            pl.BlockSpec((None, tq, wide), lambda bi, h, i: (bi, i, h)),
            pl.BlockSpec((None, tq, wide), lambda bi, h, i: (bi, jnp.minimum(i + 1, t // tq - 1), h)),
            pl.BlockSpec((None, t, wide), lambda bi, h, i: (bi, 0, sect + h)),
            pl.BlockSpec((None, t, wide), lambda bi, h, i: (bi, 0, 2 * sect + h)),
            pl.BlockSpec((None, tq, wide), lambda bi, h, i: (bi, i, 3 * sect + h)),
            pl.BlockSpec((hp, 2 * tk, tq), lambda bi, h, i: (h, 0, 0)),
            pl.BlockSpec((4, DA_HEAD_DIM), lambda bi, h, i: (0, 0)),
            pl.BlockSpec((1, hl), lambda bi, h, i: (0, 0)),
        ],
        out_specs=pl.BlockSpec((None, tq, wide), lambda bi, h, i: (bi, i, h)),
        out_shape=jax.ShapeDtypeStruct((b, t, DA_WIDTH), BF16),
        scratch_shapes=[
            pltpu.VMEM((hp, 2 * tq, hl), BF16),
            pltpu.VMEM((hp, t // tk, hl + BF16_ROWS, tk), BF16),
            pltpu.VMEM((hp, 1, 2 * tq), F32),
            pltpu.VMEM((hp, hl + BF16_ROWS, 2 * tq), F32),
            pltpu.VMEM((n_units, tk, ATTN_LANE_TILE), F32),
            pltpu.VMEM((n_units, tk, ATTN_LANE_TILE), F32),
        ],
        compiler_params=pltpu.CompilerParams(
            dimension_semantics=("parallel", "parallel", "arbitrary"), vmem_limit_bytes=VMEM_LIMIT),
        name="diff_attn",
    )(proj3, proj3, proj3, proj3, proj3, near_bias, lam_params, subln_w)


LST_PITCH = 72
SUBLANES = 8
XPOSE_UNROLL = 4


def _unit_lower_inverse_on_lanes(lt_ref, tt_ref, n_inst):
    c_len = GDN_CHUNK
    sub = lax.broadcasted_iota(jnp.int32, (SUBLANES, n_inst), 0)
    zero = jnp.zeros((SUBLANES, n_inst), F32)
    for i in range(c_len):
        n_blk = (i - 1) // SUBLANES + 1 if i > 0 else 0
        acc = [None] * n_blk
        for k in range(i):
            lik = jnp.broadcast_to(lt_ref[i, k:k + 1, :], (SUBLANES, n_inst))
            for jb in range(k // SUBLANES + 1):
                term = lik * tt_ref[k, jb * SUBLANES:(jb + 1) * SUBLANES, :]
                acc[jb] = term if acc[jb] is None else acc[jb] + term
        for jb in range(c_len // SUBLANES):
            val = -acc[jb] if jb < n_blk else zero
            if jb == i // SUBLANES:
                val = jnp.where(sub == i % SUBLANES, 1.0, val)
            tt_ref[i, jb * SUBLANES:(jb + 1) * SUBLANES, :] = val


def _gdn_kernel(qp_ref, kp_ref, vp_ref, z_ref, ab_ref, abt0_ref, abt1_ref, cw_ref, alog_ref, dtb_ref,
                alogt_ref, dtbt_ref, nw_ref, o_ref,
                s_ref, cbuf_ref, qn_ref, kn_ref, vn_ref, gc_ref, gct_ref, beta_ref, bt_ref,
                lst_ref, lt_ref, tt_ref, qk_ref, *, tb):
    c_len = GDN_CHUNK
    hl = HEAD_LANES
    bp = GDN_BATCH_PAIR
    n_chunk = tb // c_len
    n_inst = bp * n_chunk * GDN_HEADS
    halo = BF16_ROWS
    abt_refs = (abt0_ref, abt1_ref)

    @pl.when(pl.program_id(1) == 0)
    def _():
        s_ref[...] = jnp.zeros(s_ref.shape, F32)
        cbuf_ref[:, 0:halo, :] = jnp.zeros((bp * 3, halo, GDN_WIDTH), BF16)

    sel_r = lax.broadcasted_iota(jnp.int32, ((CONV_K - 1) * c_len, halo + c_len), 0)
    sel_c = lax.broadcasted_iota(jnp.int32, ((CONV_K - 1) * c_len, halo + c_len), 1)
    tap = sel_r // c_len
    shift_mat = (sel_c == sel_r - tap * c_len + halo - (CONV_K - 1) + tap).astype(BF16)

    def conv_chunk(cc, carry):
        bb = cc // n_chunk
        rows = pl.ds(pl.multiple_of((cc - bb * n_chunk) * c_len, c_len), c_len)
        for g, (src, dst) in enumerate(((qp_ref, qn_ref), (kp_ref, kn_ref), (vp_ref, vn_ref))):
            stage = bb * 3 + g
            xb = src[bb, rows, :]
            cbuf_ref[stage, halo:halo + c_len, :] = xb
            w = 0.5 * cw_ref[:, g * GDN_WIDTH:(g + 1) * GDN_WIDTH]
            taps = _dot(shift_mat, cbuf_ref[stage])
            hy = w[0:1, :] * taps[0:c_len, :]
            hy = hy + w[1:2, :] * taps[c_len:2 * c_len, :]
            hy = hy + w[2:3, :] * taps[2 * c_len:3 * c_len, :]
            hy = hy + w[3:4, :] * xb.astype(F32)
            cbuf_ref[stage, 0:halo, :] = xb[c_len - halo:c_len, :]
            y = hy + hy * jnp.tanh(hy)
            for h in range(GDN_HEADS):
                cs = slice(h * hl, (h + 1) * hl)
                yh = y[:, cs]
                if g < 2:
                    inv_norm = lax.rsqrt(jnp.sum(yh * yh, axis=-1, keepdims=True) + RMS_EPS)
                    if g == 0:
                        inv_norm = inv_norm * (GDN_HEAD_DIM ** -0.5)
                    yh = yh * inv_norm
                dst[bb, rows, cs] = yh.astype(BF16)
        return carry

    lax.fori_loop(0, bp * n_chunk, conv_chunk, 0)

    row = lax.broadcasted_iota(jnp.int32, (c_len, c_len), 0)
    col = lax.broadcasted_iota(jnp.int32, (c_len, c_len), 1)
    tril = row >= col
    strict = row > col
    tril_f = tril.astype(F32)
    triu_f = (row <= col).astype(F32)
    for bb in range(bp):
        ab = ab_ref[bb]
        beta_ref[bb] = _sigmoid(ab[:, 0:GDN_HEADS])
        g_raw = -jnp.exp(alog_ref[...]) * _softplus(ab[:, GDN_HEADS:AB_COLS] + dtb_ref[...])
        abt = abt_refs[bb][...]
        beta_t = _sigmoid(abt[0:GDN_HEADS, :])
        g_raw_t = -jnp.exp(alogt_ref[...]) * _softplus(abt[GDN_HEADS:AB_COLS, :] + dtbt_ref[...])
        for c in range(n_chunk):
            cc = bb * n_chunk + c
            gc_ref[bb, c * c_len:(c + 1) * c_len, :] = _dot_split(tril_f, g_raw[c * c_len:(c + 1) * c_len, :])
            gct_ref[cc] = _dot_split(g_raw_t[:, c * c_len:(c + 1) * c_len], triu_f)
            bt_ref[cc] = beta_t[:, c * c_len:(c + 1) * c_len]

    heads = range(GDN_HEADS)
    lanes = [slice(h * hl, (h + 1) * hl) for h in heads]

    def intra_chunk(cc, carry):
        bb = cc // n_chunk
        rows = pl.ds(pl.multiple_of((cc - bb * n_chunk) * c_len, c_len), c_len)
        gct = gct_ref[cc]
        gcc = gc_ref[bb, rows, :]
        betac = beta_ref[bb, rows, :]
        ks = [kn_ref[bb, rows, lanes[h]] for h in heads]
        kq = [jnp.concatenate([ks[h], qn_ref[bb, rows, lanes[h]]], axis=0) for h in heads]
        kk = [lax.dot_general(kq[h], ks[h], NT_DIMS, preferred_element_type=F32) for h in heads]
        decay = [jnp.where(tril, jnp.exp(jnp.where(tril, gcc[:, h:h + 1] - gct[h:h + 1, :], 0.0)), 0.0)
                 for h in heads]
        pad = jnp.zeros((c_len, hl - c_len), F32)
        for h in heads:
            l_strict = jnp.where(strict, kk[h][0:c_len, :] * decay[h] * betac[:, h:h + 1], 0.0)
            base = pl.multiple_of((cc * GDN_HEADS + h) * LST_PITCH, SUBLANES)
            lst_ref[pl.ds(base, c_len), :] = jnp.concatenate([l_strict, pad], axis=1)
            qk_ref[cc * GDN_HEADS + h] = (kk[h][c_len:2 * c_len, :] * decay[h]).astype(BF16)
        return carry

    lax.fori_loop(0, bp * n_chunk, intra_chunk, 0)

    def to_lanes(ii, carry):
        for i in [ii * XPOSE_UNROLL + u for u in range(XPOSE_UNROLL)]:
            x = lst_ref[pl.ds(i, n_inst, stride=LST_PITCH), :]
            lt_ref[i] = x.T[0:c_len, :]
        return carry

    lax.fori_loop(0, c_len // XPOSE_UNROLL, to_lanes, 0)
    _unit_lower_inverse_on_lanes(lt_ref, tt_ref, n_inst)

    def from_lanes(ii, carry):
        for i in [ii * XPOSE_UNROLL + u for u in range(XPOSE_UNROLL)]:
            t = tt_ref[i]
            lst_ref[pl.ds(i, n_inst, stride=LST_PITCH), :] = jnp.concatenate([t, jnp.zeros_like(t)], axis=0).T
        return carry

    lax.fori_loop(0, c_len // XPOSE_UNROLL, from_lanes, 0)

    chains = [(bb, h) for bb in range(bp) for h in heads]
    idx = range(len(chains))

    def scan_chunk(c, carry):
        rows = pl.ds(pl.multiple_of(c * c_len, c_len), c_len)
        ccs = [bb * n_chunk + c for bb in range(bp)]
        gct = [gct_ref[cc] for cc in ccs]
        gcc = [gc_ref[bb, rows, :] for bb in range(bp)]
        bct = [bt_ref[cc] for cc in ccs]
        inst = [ccs[bb] * GDN_HEADS + h for bb, h in chains]
        g_row = [gct[bb][h:h + 1, :] for bb, h in chains]
        g_last = [g[:, c_len - 1:c_len] for g in g_row]
        ks = [kn_ref[bb, rows, lanes[h]] for bb, h in chains]
        a_inv = [lst_ref[pl.ds(pl.multiple_of(n * LST_PITCH, SUBLANES), c_len), 0:c_len] for n in inst]
        b_row = [bct[bb][h:h + 1, :] for bb, h in chains]
        u = [_dot((a_inv[x] * b_row[x]).astype(BF16), vn_ref[bb, rows, lanes[h]]) for x, (bb, h) in enumerate(chains)]
        w = [_dot((a_inv[x] * (b_row[x] * jnp.exp(g_row[x]))).astype(BF16), ks[x]) for x in idx]
        s_old = [s_ref[bb * GDN_HEADS + h] for bb, h in chains]
        ws = [_dot(jnp.concatenate([w[x].astype(BF16), qn_ref[bb, rows, lanes[h]]], axis=0),
                   s_old[x].astype(BF16)) for x, (bb, h) in enumerate(chains)]
        v_new = [(u[x] - ws[x][0:c_len, :]).astype(BF16) for x in idx]
        kd_t = [(ks[x].astype(F32).T * jnp.exp(g_last[x] - g_row[x])).astype(BF16) for x in idx]
        for x, (bb, h) in enumerate(chains):
            s_ref[bb * GDN_HEADS + h] = s_old[x] * jnp.exp(g_last[x]) + _dot(kd_t[x], v_new[x])
        o = [jnp.exp(gcc[bb][:, h:h + 1]) * ws[x][c_len:2 * c_len, :] + _dot(qk_ref[inst[x]], v_new[x])
             for x, (bb, h) in enumerate(chains)]
        for x, (bb, h) in enumerate(chains):
            on = o[x] * lax.rsqrt(jnp.mean(o[x] * o[x], axis=-1, keepdims=True) + RMS_EPS) * nw_ref[...]
            zc = z_ref[bb, rows, lanes[h]].astype(F32)
            o_ref[bb, rows, lanes[h]] = (on * _silu(zc)).astype(BF16)
        return carry

    lax.fori_loop(0, n_chunk, scan_chunk, 0)


def _gdn(proj3, ab3, abt2, conv_w, a_log, dt_bias, gdn_norm_w):
    b, t, _ = proj3.shape
    tb = GDN_TB
    gw = GDN_WIDTH
    first = 4 * DA_WIDTH // gw
    bp = GDN_BATCH_PAIR
    n_chunk = tb // GDN_CHUNK
    n_inst = bp * n_chunk * GDN_HEADS
    n_tb = t // tb
    assert bp == 2 and b % bp == 0
    assert n_inst == HEAD_LANES, "one (batch row, chunk, head) matrix per lane"
    return pl.pallas_call(
        functools.partial(_gdn_kernel, tb=tb),
        grid=(b // bp, n_tb),
        in_specs=[
            pl.BlockSpec((bp, tb, gw), lambda bi, ti: (bi, ti, first)),
            pl.BlockSpec((bp, tb, gw), lambda bi, ti: (bi, ti, first + 1)),
            pl.BlockSpec((bp, tb, gw), lambda bi, ti: (bi, ti, first + 2)),
            pl.BlockSpec((bp, tb, gw), lambda bi, ti: (bi, ti, first + 3)),
            pl.BlockSpec((bp, tb, AB_COLS), lambda bi, ti: (bi, ti, 0)),
            pl.BlockSpec((AB_COLS, tb), lambda bi, ti: (0, bp * bi * n_tb + ti)),
            pl.BlockSpec((AB_COLS, tb), lambda bi, ti: (0, (bp * bi + 1) * n_tb + ti)),
            pl.BlockSpec((CONV_K, 3 * gw), lambda bi, ti: (0, 0)),
            pl.BlockSpec((1, GDN_HEADS), lambda bi, ti: (0, 0)),
            pl.BlockSpec((1, GDN_HEADS), lambda bi, ti: (0, 0)),
            pl.BlockSpec((GDN_HEADS, 1), lambda bi, ti: (0, 0)),
            pl.BlockSpec((GDN_HEADS, 1), lambda bi, ti: (0, 0)),
            pl.BlockSpec((1, GDN_HEAD_DIM), lambda bi, ti: (0, 0)),
        ],
        out_specs=pl.BlockSpec((bp, tb, gw), lambda bi, ti: (bi, ti, 0)),
        out_shape=jax.ShapeDtypeStruct((b, t, gw), BF16),
        scratch_shapes=[
            pltpu.VMEM((bp * GDN_HEADS, GDN_HEAD_DIM, GDN_HEAD_DIM), F32),
            pltpu.VMEM((bp * 3, BF16_ROWS + GDN_CHUNK, gw), BF16),
            pltpu.VMEM((bp, tb, gw), BF16),
            pltpu.VMEM((bp, tb, gw), BF16),
            pltpu.VMEM((bp, tb, gw), BF16),
            pltpu.VMEM((bp, tb, GDN_HEADS), F32),
            pltpu.VMEM((bp * n_chunk, GDN_HEADS, GDN_CHUNK), F32),
            pltpu.VMEM((bp, tb, GDN_HEADS), F32),
            pltpu.VMEM((bp * n_chunk, GDN_HEADS, GDN_CHUNK), F32),
            pltpu.VMEM((n_inst * LST_PITCH, HEAD_LANES), F32),
            pltpu.VMEM((GDN_CHUNK, GDN_CHUNK, n_inst), F32),
            pltpu.VMEM((GDN_CHUNK, GDN_CHUNK, n_inst), F32),
            pltpu.VMEM((n_inst, GDN_CHUNK, GDN_CHUNK), BF16),
        ],
        compiler_params=pltpu.CompilerParams(
            dimension_semantics=("parallel", "arbitrary"), vmem_limit_bytes=GDN_VMEM_LIMIT),
        name="gdn",
    )(proj3, proj3, proj3, proj3, ab3, abt2, abt2, conv_w,
      a_log.reshape(1, GDN_HEADS), dt_bias.reshape(1, GDN_HEADS),
      a_log.reshape(GDN_HEADS, 1), dt_bias.reshape(GDN_HEADS, 1),
      gdn_norm_w.reshape(1, GDN_HEAD_DIM))


def _outproj_kernel(oda_ref, ogdn_ref, w_ref, x_ref, fw_ref, o_ref, *, final):
    acc = _dot(oda_ref[...], w_ref[0:DA_WIDTH, :]) + _dot(ogdn_ref[...], w_ref[DA_WIDTH:DA_WIDTH + GDN_WIDTH, :])
    y = x_ref[...] + acc
    if final:
        y = y * lax.rsqrt(jnp.mean(y * y, axis=-1, keepdims=True) + RMS_EPS) * fw_ref[...]
    o_ref[...] = y


def _outproj(o_da2, o_gdn2, w_out_bf16, layer, x2, final_w, final):
    m = x2.shape[0]
    tm = OUTPROJ_TM
    return pl.pallas_call(
        functools.partial(_outproj_kernel, final=final),
        grid=(m // tm,),
        in_specs=[
            pl.BlockSpec((tm, DA_WIDTH), lambda i: (i, 0)),
            pl.BlockSpec((tm, GDN_WIDTH), lambda i: (i, 0)),
            pl.BlockSpec((None, DA_WIDTH + GDN_WIDTH, D_MODEL), lambda i: (layer, 0, 0)),
            pl.BlockSpec((tm, D_MODEL), lambda i: (i, 0)),
            pl.BlockSpec((1, D_MODEL), lambda i: (0, 0)),
        ],
        out_specs=pl.BlockSpec((tm, D_MODEL), lambda i: (i, 0)),
        out_shape=jax.ShapeDtypeStruct((m, D_MODEL), F32),
        compiler_params=pltpu.CompilerParams(
            dimension_semantics=("parallel",), vmem_limit_bytes=VMEM_LIMIT),
        name="outproj",
    )(o_da2, o_gdn2, w_out_bf16, x2, final_w)


def kernel(x, norm_w, w_in, w_out, lambda_q1, lambda_k1, lambda_q2, lambda_k2, da_subln_w, rel_bias,
           conv_w, a_log, dt_bias, gdn_norm_w, final_norm_w):
    b, t, d = x.shape
    m = b * t
    x2 = x.reshape(m, d)
    near_bias = _near_bias(rel_bias)
    final_w = final_norm_w.reshape(1, d)
    w_out_bf16 = w_out.astype(BF16)
    w_in_t = jnp.swapaxes(w_in, 1, 2)
    for l in range(DEPTH):
        lambda_init = 0.8 - 0.6 * math.exp(-0.3 * l)
        proj, ab, abt = _inproj(x2, norm_w[l].reshape(1, d), w_in_t, l)
        proj3 = proj.reshape(b, t, MAIN_COLS)
        lam_params = jnp.stack([lambda_q1[l], lambda_k1[l], lambda_q2[l], lambda_k2[l]])
        o_da = _attention(proj3, near_bias, lam_params, da_subln_w[l].reshape(1, 2 * DA_HEAD_DIM), lambda_init)
        o_gdn = _gdn(proj3, ab.reshape(b, t, AB_COLS), abt, conv_w[l], a_log[l], dt_bias[l], gdn_norm_w[l])
        x2 = _outproj(o_da.reshape(m, DA_WIDTH), o_gdn.reshape(m, GDN_WIDTH), w_out_bf16, l,
                      x2, final_w, final=(l == DEPTH - 1))
    return x2.reshape(b, t, d)
```

```python
import functools
import math

import jax
import jax.numpy as jnp
from jax import lax
from jax.experimental import pallas as pl
from jax.experimental.pallas import tpu as pltpu

F32 = jnp.float32
BF16 = jnp.bfloat16

D_MODEL = 2048
DEPTH = 2
DA_HEADS = 8
DA_HEAD_DIM = 64
DA_WIDTH = DA_HEADS * 2 * DA_HEAD_DIM
GDN_HEADS = 8
GDN_HEAD_DIM = 128
GDN_WIDTH = GDN_HEADS * GDN_HEAD_DIM
CONV_K = 4
GDN_CHUNK = 64
REL_BUCKETS = 32
REL_MAX_DIST = 128
REL_MAX_EXACT = REL_BUCKETS // 2
RMS_EPS = 1e-6
NEG_INF = -1e30
MAIN_COLS = 4 * DA_WIDTH + 4 * GDN_WIDTH
AB_COLS = 2 * GDN_HEADS
HEAD_LANES = 128
BF16_ROWS = 16
LOG2_E = math.log2(math.e)

INPROJ_TM = 1024
INPROJ_TN = 1024
ATTN_TQ = 512
ATTN_TK = 512
ATTN_LANE_TILE = 256
ATTN_HEADS_PER_STEP = 2
BIAS_BLOCK = (64, 128)
GDN_TB = 512
GDN_BATCH_PAIR = 2
OUTPROJ_TM = 512
VMEM_LIMIT = 48 * 1024 * 1024
GDN_VMEM_LIMIT = 56 * 1024 * 1024

NT_DIMS = (((1,), (1,)), ((), ()))


def _sigmoid(x):
    return 1.0 / (1.0 + jnp.exp(-x))


def _silu(x):
    h = 0.5 * x
    return h + h * jnp.tanh(h)


def _softplus(x):
    return jnp.maximum(x, 0.0) + jnp.log(1.0 + jnp.exp(-jnp.abs(x)))


def _dot(a, b):
    return jnp.dot(a, b, preferred_element_type=F32)


def _split_hi_lo(a):
    hi = a.astype(BF16)
    lo = (a - hi.astype(F32)).astype(BF16)
    return hi, lo


def _dot_split(a, b):
    a_hi, a_lo = _split_hi_lo(a)
    b_hi, b_lo = _split_hi_lo(b)
    return _dot(a_hi, b_hi) + _dot(a_lo, b_hi) + _dot(a_hi, b_lo)


def _inproj_kernel(x_ref, nw_ref, w_ref, wab_ref, proj_ref, ab_ref, abt_ref, h_ref):
    @pl.when(pl.program_id(1) == 0)
    def _():
        x = x_ref[...]
        ms = jnp.mean(x * x, axis=-1, keepdims=True)
        h = (x * lax.rsqrt(ms + RMS_EPS) * nw_ref[...]).astype(BF16)
        h_ref[...] = h
        w_ab = jnp.concatenate(
            [wab_ref[...].astype(BF16), jnp.zeros((HEAD_LANES - AB_COLS, D_MODEL), BF16)], axis=0)
        ab = lax.dot_general(h, w_ab, NT_DIMS, preferred_element_type=F32)
        ab_ref[...] = ab[:, 0:AB_COLS]
        abt_ref[...] = ab.T[0:AB_COLS, :]

    proj_ref[...] = lax.dot_general(h_ref[...], w_ref[...].astype(BF16), NT_DIMS,
                                    preferred_element_type=F32).astype(BF16)


def _inproj(x2, norm_w, w_in_t, layer):
    m = x2.shape[0]
    tm, tn = INPROJ_TM, INPROJ_TN
    return pl.pallas_call(
        _inproj_kernel,
        grid=(m // tm, MAIN_COLS // tn),
        in_specs=[
            pl.BlockSpec((tm, D_MODEL), lambda i, j: (i, 0)),
            pl.BlockSpec((1, D_MODEL), lambda i, j: (0, 0)),
            pl.BlockSpec((None, tn, D_MODEL), lambda i, j: (layer, j, 0)),
            pl.BlockSpec((None, AB_COLS, D_MODEL), lambda i, j: (layer, MAIN_COLS // AB_COLS, 0)),
        ],
        out_specs=[
            pl.BlockSpec((tm, tn), lambda i, j: (i, j)),
            pl.BlockSpec((tm, AB_COLS), lambda i, j: (i, 0)),
            pl.BlockSpec((AB_COLS, tm), lambda i, j: (0, i)),
        ],
        out_shape=[
            jax.ShapeDtypeStruct((m, MAIN_COLS), BF16),
            jax.ShapeDtypeStruct((m, AB_COLS), F32),
            jax.ShapeDtypeStruct((AB_COLS, m), F32),
        ],
        scratch_shapes=[pltpu.VMEM((tm, D_MODEL), BF16)],
        compiler_params=pltpu.CompilerParams(
            dimension_semantics=("parallel", "arbitrary"), vmem_limit_bytes=GDN_VMEM_LIMIT),
        name="inproj",
    )(x2, norm_w, w_in_t, w_in_t)


def _bias_kernel(rb_ref, o_ref, *, tk):
    h = pl.program_id(0)
    far = rb_ref[REL_BUCKETS - 1, h]
    br, bc = BIAS_BLOCK
    shape = (br, bc)
    for rb in range(o_ref.shape[0] // br):
        for cb in range(o_ref.shape[1] // bc):
            blk = (slice(rb * br, (rb + 1) * br), slice(cb * bc, (cb + 1) * bc))
            d_min = cb * bc - (rb * br + br - 1) + tk
            d_max = cb * bc + bc - 1 - rb * br + tk
            if d_max < 0:
                o_ref[blk] = jnp.full(shape, NEG_INF, F32)
            elif d_min >= REL_MAX_DIST:
                o_ref[blk] = jnp.zeros(shape, F32)
            else:
                c = lax.broadcasted_iota(jnp.int32, shape, 0) + rb * br
                r = lax.broadcasted_iota(jnp.int32, shape, 1) + cb * bc
                dist = r - c + tk
                n = jnp.maximum(dist, 0)
                nf = jnp.maximum(n, REL_MAX_EXACT).astype(F32)
                large = REL_MAX_EXACT + (jnp.log(nf / REL_MAX_EXACT) / math.log(REL_MAX_DIST / REL_MAX_EXACT)
                                         * (REL_BUCKETS - REL_MAX_EXACT)).astype(jnp.int32)
                large = jnp.minimum(large, REL_BUCKETS - 1)
                bucket = jnp.where(n < REL_MAX_EXACT, n, large)
                val = jnp.zeros(shape, F32)
                for b in range(REL_BUCKETS - 1):
                    val = jnp.where(bucket == b, rb_ref[b, h] - far, val)
                o_ref[blk] = jnp.where(dist >= 0, val * LOG2_E, NEG_INF)


def _near_bias(rel_bias):
    tq, tk = ATTN_TQ, ATTN_TK
    return pl.pallas_call(
        functools.partial(_bias_kernel, tk=tk),
        grid=(DA_HEADS,),
        in_specs=[pl.BlockSpec(memory_space=pltpu.SMEM)],
        out_specs=pl.BlockSpec((None, 2 * tk, tq), lambda h: (h, 0, 0)),
        out_shape=jax.ShapeDtypeStruct((DA_HEADS, 2 * tk, tq), F32),
        name="near_bias",
    )(rel_bias)


def _attn_kernel(q_ref, qnext_ref, k_ref, v_ref, g_ref, bias_ref, lam_ref, sw_ref, o_ref,
                 qz_ref, vt_ref, m_ref, acc_ref, sa_ref, sb_ref, *, tq, tk, lambda_init):
    i = pl.program_id(2)
    hl = HEAD_LANES
    lt = ATTN_LANE_TILE
    hp = ATTN_HEADS_PER_STEP
    n_tiles = 2 * tq // lt
    q_tiles = tq // lt
    head_cols = [slice(hd * hl, (hd + 1) * hl) for hd in range(hp)]

    units = [(hd, t) for hd in range(hp) for t in range(n_tiles)]
    cols = [slice(t * lt, (t + 1) * lt) for t in range(n_tiles)]

    def set_queries(src_ref):
        for hd in range(hp):
            q = src_ref[:, head_cols[hd]] * (DA_HEAD_DIM ** -0.5 * LOG2_E)
            lane = lax.broadcasted_iota(jnp.int32, q.shape, 1)
            zero = jnp.zeros_like(q)
            qz_ref[hd, 0:tq, :] = jnp.where(lane < DA_HEAD_DIM, q, zero)
            qz_ref[hd, tq:2 * tq, :] = jnp.where(lane >= DA_HEAD_DIM, q, zero)

    def keys_used(t, diag):
        return (t % q_tiles + 1) * lt if diag else tk

    def scores(j, s_ref, diag=False):
        row0 = pl.multiple_of(j * tk, tk)
        for u, (hd, t) in enumerate(units):
            nk = keys_used(t, diag)
            kj = k_ref[pl.ds(row0, nk), head_cols[hd]]
            s_ref[u, 0:nk, :] = lax.dot_general(kj, qz_ref[hd, cols[t], :], NT_DIMS, preferred_element_type=F32)

    @pl.when(i == 0)
    def _():
        ones = jnp.ones((vt_ref.shape[2] - hl, tk), BF16)
        for hd in range(hp):
            for j in range(vt_ref.shape[1]):
                vt_ref[hd, j, 0:hl, :] = v_ref[j * tk:(j + 1) * tk, head_cols[hd]].astype(F32).T.astype(BF16)
                vt_ref[hd, j, hl:, :] = ones
        set_queries(q_ref)
        scores(0, sa_ref)

    m_ref[...] = jnp.full(m_ref.shape, NEG_INF, F32)
    acc_ref[...] = jnp.zeros(acc_ref.shape, F32)

    def add_bias(a, hd, t, tile_row0, first_row):
        n_rows = a.shape[0]
        if first_row >= n_rows:
            return a
        b = bias_ref[hd, tile_row0 + first_row:tile_row0 + n_rows, cols[t % q_tiles]]
        if first_row == 0:
            return a + b
        return jnp.concatenate([a[0:first_row, :], a[first_row:, :] + b], axis=0)

    def softmax_pv(j, s_ref, kind="far"):
        diag = kind == "diag"
        nk = [keys_used(t, diag) for _, t in units]
        s = [s_ref[u, 0:nk[u], :] for u in range(len(units))]
        if kind == "prev":
            s = [add_bias(s[u], hd, t, 0, tk - REL_MAX_DIST) if t % q_tiles == 0 else s[u]
                 for u, (hd, t) in enumerate(units)]
        elif diag:
            s = [add_bias(s[u], hd, t, tk, max((t % q_tiles) * lt - REL_MAX_DIST, 0))
                 for u, (hd, t) in enumerate(units)]
        m_prev = [m_ref[hd, :, cols[t]] for hd, t in units]
        m_new = [jnp.maximum(mp, jnp.max(a, axis=0, keepdims=True)) for mp, a in zip(m_prev, s)]
        p = [jnp.exp2(a - mn).astype(BF16) for a, mn in zip(s, m_new)]
        pv = [_dot(vt_ref[hd, j, :, 0:nk[u]], p[u]) for u, (hd, t) in enumerate(units)]
        for u, (hd, t) in enumerate(units):
            acc_ref[hd, :, cols[t]] = jnp.exp2(m_prev[u] - m_new[u]) * acc_ref[hd, :, cols[t]] + pv[u]
            m_ref[hd, :, cols[t]] = m_new[u]

    n_far = jnp.maximum(i - 1, 0)
    odd = n_far % 2

    @pl.when(odd == 1)
    def _():
        softmax_pv(0, sa_ref)
        scores(1, sa_ref)

    def far_pair(jj, carry):
        j = odd + 2 * jj
        scores(j + 1, sb_ref)
        softmax_pv(j, sa_ref)
        scores(j + 2, sa_ref)
        softmax_pv(j + 1, sb_ref)
        return carry

    lax.fori_loop(0, n_far // 2, far_pair, 0)
    scores(i, sb_ref, diag=True)
    softmax_pv(n_far, sa_ref, "prev")
    has_prev = i > 0
    m_ref[...] = jnp.where(has_prev, m_ref[...], NEG_INF)
    acc_ref[...] = jnp.where(has_prev, acc_ref[...], 0.0)
    set_queries(qnext_ref)
    scores(0, sa_ref)
    softmax_pv(i, sb_ref, "diag")

    lp = lam_ref[...]
    s1 = jnp.sum(lp[0:1, :] * lp[1:2, :], axis=-1, keepdims=True)
    s2 = jnp.sum(lp[2:3, :] * lp[3:4, :], axis=-1, keepdims=True)
    lam = jnp.exp(s1) - jnp.exp(s2) + lambda_init
    for hd in range(hp):
        inv_l = 1.0 / acc_ref[hd, hl:hl + 1, :]
        o_t = (acc_ref[hd, 0:hl, 0:tq] * inv_l[:, 0:tq]
               - lam * (acc_ref[hd, 0:hl, tq:2 * tq] * inv_l[:, tq:2 * tq]))
        o = o_t.T
        ms = jnp.mean(o * o, axis=-1, keepdims=True)
        y = o * lax.rsqrt(ms + RMS_EPS) * sw_ref[...] * (1.0 - lambda_init)
        o_ref[:, head_cols[hd]] = (y * _silu(g_ref[:, head_cols[hd]].astype(F32))).astype(BF16)


def _attention(proj3, near_bias, lam_params, subln_w, lambda_init):
    b, t, _ = proj3.shape
    tq, tk = ATTN_TQ, ATTN_TK
    assert tq == tk and t % tq == 0
    hl = HEAD_LANES
    hp = ATTN_HEADS_PER_STEP
    wide = hp * hl
    sect = DA_WIDTH // wide
    n_units = hp * (2 * tq // ATTN_LANE_TILE)
    return pl.pallas_call(
        functools.partial(_attn_kernel, tq=tq, tk=tk, lambda_init=lambda_init),
        grid=(b, DA_HEADS // hp, t // tq),
        in_specs=[
            pl.BlockSpec((None, tq, wide), lambda bi, h, i: (bi, i, h)),
            pl.BlockSpec((None, tq, wide), lambda bi, h, i: (bi, jnp.minimum(i + 1, t // tq - 1), h)),
            pl.BlockSpec((None, t, wide), lambda bi, h, i: (bi, 0, sect + h)),
            pl.BlockSpec((None, t, wide), lambda bi, h, i: (bi, 0, 2 * sect + h)),
            pl.BlockSpec((None, tq, wide), lambda bi, h, i: (bi, i, 3 * sect + h)),
            pl.BlockSpec((hp, 2 * tk, tq), lambda bi, h, i: (h, 0, 0)),
            pl.BlockSpec((4, DA_HEAD_DIM), lambda bi, h, i: (0, 0)),
            pl.BlockSpec((1, hl), lambda bi, h, i: (0, 0)),
        ],
        out_specs=pl.BlockSpec((None, tq, wide), lambda bi, h, i: (bi, i, h)),
        out_shape=jax.ShapeDtypeStruct((b, t, DA_WIDTH), BF16),
        scratch_shapes=[
            pltpu.VMEM((hp, 2 * tq, hl), BF16),
            pltpu.VMEM((hp, t // tk, hl + BF16_ROWS, tk), BF16),
            pltpu.VMEM((hp, 1, 2 * tq), F32),
            pltpu.VMEM((hp, hl + BF16_ROWS, 2 * tq), F32),
            pltpu.VMEM((n_units, tk, ATTN_LANE_TILE), F32),
            pltpu.VMEM((n_units, tk, ATTN_LANE_TILE), F32),
        ],
        compiler_params=pltpu.CompilerParams(
            dimension_semantics=("parallel", "parallel", "arbitrary"), vmem_limit_bytes=VMEM_LIMIT),
        name="diff_attn",
    )(proj3, proj3, proj3, proj3, proj3, near_bias, lam_params, subln_w)


LST_PITCH = 72
SUBLANES = 8
XPOSE_LEAD = 4


def _unit_lower_inverse_on_lanes(lst_ref, ainv_ref, lt_ref, tt_ref, n_inst):
    c_len = GDN_CHUNK
    sub = lax.broadcasted_iota(jnp.int32, (SUBLANES, n_inst), 0)
    zero = jnp.zeros((SUBLANES, n_inst), F32)

    def gather(r):
        lt_ref[r] = lst_ref[pl.ds(r, n_inst, stride=LST_PITCH), :].T[0:c_len, :]

    def scatter(r):
        t = tt_ref[r]
        ainv_ref[pl.ds(r, n_inst, stride=LST_PITCH), :] = jnp.concatenate([t, jnp.zeros_like(t)], axis=0).T

    for r in range(XPOSE_LEAD):
        gather(r)
    for i in range(c_len):
        if i + XPOSE_LEAD < c_len:
            gather(i + XPOSE_LEAD)
        if i >= XPOSE_LEAD:
            scatter(i - XPOSE_LEAD)
        n_blk = (i - 1) // SUBLANES + 1 if i > 0 else 0
        acc = [None] * n_blk
        for k in range(i):
            lik = jnp.broadcast_to(lt_ref[i, k:k + 1, :], (SUBLANES, n_inst))
            for jb in range(k // SUBLANES + 1):
                term = lik * tt_ref[k, jb * SUBLANES:(jb + 1) * SUBLANES, :]
                acc[jb] = term if acc[jb] is None else acc[jb] + term
        for jb in range(c_len // SUBLANES):
            val = -acc[jb] if jb < n_blk else zero
            if jb == i // SUBLANES:
                val = jnp.where(sub == i % SUBLANES, 1.0, val)
            tt_ref[i, jb * SUBLANES:(jb + 1) * SUBLANES, :] = val
    for r in range(c_len - XPOSE_LEAD, c_len):
        scatter(r)


def _gdn_kernel(qp_ref, kp_ref, vp_ref, z_ref, ab_ref, abt0_ref, abt1_ref, cw_ref, alog_ref, dtb_ref,
                alogt_ref, dtbt_ref, nw_ref, o_ref,
                s_ref, cbuf_ref, qn_ref, kn_ref, vn_ref, gc_ref, gct_ref, beta_ref, bt_ref,
                lst_ref, ainv_ref, lt_ref, tt_ref, qk_ref, *, tb):
    c_len = GDN_CHUNK
    hl = HEAD_LANES
    bp = GDN_BATCH_PAIR
    n_chunk = tb // c_len
    n_inst = bp * n_chunk * GDN_HEADS
    halo = BF16_ROWS
    abt_refs = (abt0_ref, abt1_ref)

    @pl.when(pl.program_id(1) == 0)
    def _():
        s_ref[...] = jnp.zeros(s_ref.shape, F32)
        cbuf_ref[:, 0:halo, :] = jnp.zeros((bp * 3, halo, GDN_WIDTH), BF16)

    sel_r = lax.broadcasted_iota(jnp.int32, ((CONV_K - 1) * c_len, halo + c_len), 0)
    sel_c = lax.broadcasted_iota(jnp.int32, ((CONV_K - 1) * c_len, halo + c_len), 1)
    tap = sel_r // c_len
    shift_mat = (sel_c == sel_r - tap * c_len + halo - (CONV_K - 1) + tap).astype(BF16)

    def conv_chunk(cc, carry):
        bb = cc // n_chunk
        rows = pl.ds(pl.multiple_of((cc - bb * n_chunk) * c_len, c_len), c_len)
        for g, (src, dst) in enumerate(((qp_ref, qn_ref), (kp_ref, kn_ref), (vp_ref, vn_ref))):
            stage = bb * 3 + g
            xb = src[bb, rows, :]
            cbuf_ref[stage, halo:halo + c_len, :] = xb
            w = 0.5 * cw_ref[:, g * GDN_WIDTH:(g + 1) * GDN_WIDTH]
            taps = _dot(shift_mat, cbuf_ref[stage])
            hy = w[0:1, :] * taps[0:c_len, :]
            hy = hy + w[1:2, :] * taps[c_len:2 * c_len, :]
            hy = hy + w[2:3, :] * taps[2 * c_len:3 * c_len, :]
            hy = hy + w[3:4, :] * xb.astype(F32)
            cbuf_ref[stage, 0:halo, :] = xb[c_len - halo:c_len, :]
            y = hy + hy * jnp.tanh(hy)
            for h in range(GDN_HEADS):
                cs = slice(h * hl, (h + 1) * hl)
                yh = y[:, cs]
                if g < 2:
                    inv_norm = lax.rsqrt(jnp.sum(yh * yh, axis=-1, keepdims=True) + RMS_EPS)
                    if g == 0:
                        inv_norm = inv_norm * (GDN_HEAD_DIM ** -0.5)
                    yh = yh * inv_norm
                dst[bb, rows, cs] = yh.astype(BF16)
        return carry

    lax.fori_loop(0, bp * n_chunk, conv_chunk, 0)

    row = lax.broadcasted_iota(jnp.int32, (c_len, c_len), 0)
    col = lax.broadcasted_iota(jnp.int32, (c_len, c_len), 1)
    tril = row >= col
    strict = row > col
    tril_f = tril.astype(F32)
    triu_f = (row <= col).astype(F32)
    for bb in range(bp):
        ab = ab_ref[bb]
        beta_ref[bb] = _sigmoid(ab[:, 0:GDN_HEADS])
        g_raw = -jnp.exp(alog_ref[...]) * _softplus(ab[:, GDN_HEADS:AB_COLS] + dtb_ref[...])
        abt = abt_refs[bb][...]
        beta_t = _sigmoid(abt[0:GDN_HEADS, :])
        g_raw_t = -jnp.exp(alogt_ref[...]) * _softplus(abt[GDN_HEADS:AB_COLS, :] + dtbt_ref[...])
        for c in range(n_chunk):
            cc = bb * n_chunk + c
            gc_ref[bb, c * c_len:(c + 1) * c_len, :] = _dot_split(tril_f, g_raw[c * c_len:(c + 1) * c_len, :])
            gct_ref[cc] = _dot_split(g_raw_t[:, c * c_len:(c + 1) * c_len], triu_f)
            bt_ref[cc] = beta_t[:, c * c_len:(c + 1) * c_len]

    heads = range(GDN_HEADS)
    lanes = [slice(h * hl, (h + 1) * hl) for h in heads]

    def intra_chunk(cc, carry):
        bb = cc // n_chunk
        rows = pl.ds(pl.multiple_of((cc - bb * n_chunk) * c_len, c_len), c_len)
        gct = gct_ref[cc]
        gcc = gc_ref[bb, rows, :]
        betac = beta_ref[bb, rows, :]
        ks = [kn_ref[bb, rows, lanes[h]] for h in heads]
        kq = [jnp.concatenate([ks[h], qn_ref[bb, rows, lanes[h]]], axis=0) for h in heads]
        kk = [lax.dot_general(kq[h], ks[h], NT_DIMS, preferred_element_type=F32) for h in heads]
        decay = [jnp.where(tril, jnp.exp(jnp.where(tril, gcc[:, h:h + 1] - gct[h:h + 1, :], 0.0)), 0.0)
                 for h in heads]
        pad = jnp.zeros((c_len, hl - c_len), F32)
        for h in heads:
            l_strict = jnp.where(strict, kk[h][0:c_len, :] * decay[h] * betac[:, h:h + 1], 0.0)
            base = pl.multiple_of((cc * GDN_HEADS + h) * LST_PITCH, SUBLANES)
            lst_ref[pl.ds(base, c_len), :] = jnp.concatenate([l_strict, pad], axis=1)
            qk_ref[cc * GDN_HEADS + h] = (kk[h][c_len:2 * c_len, :] * decay[h]).astype(BF16)
        return carry

    lax.fori_loop(0, bp * n_chunk, intra_chunk, 0)

    _unit_lower_inverse_on_lanes(lst_ref, ainv_ref, lt_ref, tt_ref, n_inst)

    chains = [(bb, h) for bb in range(bp) for h in heads]
    idx = range(len(chains))

    def scan_chunk(c, carry):
        rows = pl.ds(pl.multiple_of(c * c_len, c_len), c_len)
        ccs = [bb * n_chunk + c for bb in range(bp)]
        gct = [gct_ref[cc] for cc in ccs]
        gcc = [gc_ref[bb, rows, :] for bb in range(bp)]
        bct = [bt_ref[cc] for cc in ccs]
        inst = [ccs[bb] * GDN_HEADS + h for bb, h in chains]
        g_row = [gct[bb][h:h + 1, :] for bb, h in chains]
        g_last = [g[:, c_len - 1:c_len] for g in g_row]
        ks = [kn_ref[bb, rows, lanes[h]] for bb, h in chains]
        a_inv = [ainv_ref[pl.ds(pl.multiple_of(n * LST_PITCH, SUBLANES), c_len), 0:c_len] for n in inst]
        b_row = [bct[bb][h:h + 1, :] for bb, h in chains]
        u = [_dot((a_inv[x] * b_row[x]).astype(BF16), vn_ref[bb, rows, lanes[h]]) for x, (bb, h) in enumerate(chains)]
        w = [_dot((a_inv[x] * (b_row[x] * jnp.exp(g_row[x]))).astype(BF16), ks[x]) for x in idx]
        s_old = [s_ref[bb * GDN_HEADS + h] for bb, h in chains]
        ws = [_dot(jnp.concatenate([w[x].astype(BF16), qn_ref[bb, rows, lanes[h]]], axis=0),
                   s_old[x].astype(BF16)) for x, (bb, h) in enumerate(chains)]
        v_new = [(u[x] - ws[x][0:c_len, :]).astype(BF16) for x in idx]
        kd_t = [(ks[x].astype(F32).T * jnp.exp(g_last[x] - g_row[x])).astype(BF16) for x in idx]
        for x, (bb, h) in enumerate(chains):
            s_ref[bb * GDN_HEADS + h] = s_old[x] * jnp.exp(g_last[x]) + _dot(kd_t[x], v_new[x])
        o = [jnp.exp(gcc[bb][:, h:h + 1]) * ws[x][c_len:2 * c_len, :] + _dot(qk_ref[inst[x]], v_new[x])
             for x, (bb, h) in enumerate(chains)]
        for x, (bb, h) in enumerate(chains):
            on = o[x] * lax.rsqrt(jnp.mean(o[x] * o[x], axis=-1, keepdims=True) + RMS_EPS) * nw_ref[...]
            zc = z_ref[bb, rows, lanes[h]].astype(F32)
            o_ref[bb, rows, lanes[h]] = (on * _silu(zc)).astype(BF16)
        return carry

    lax.fori_loop(0, n_chunk, scan_chunk, 0)


def _gdn(proj3, ab3, abt2, conv_w, a_log, dt_bias, gdn_norm_w):
    b, t, _ = proj3.shape
    tb = GDN_TB
    gw = GDN_WIDTH
    first = 4 * DA_WIDTH // gw
    bp = GDN_BATCH_PAIR
    n_chunk = tb // GDN_CHUNK
    n_inst = bp * n_chunk * GDN_HEADS
    n_tb = t // tb
    assert bp == 2 and b % bp == 0
    assert n_inst == HEAD_LANES, "one (batch row, chunk, head) matrix per lane"
    return pl.pallas_call(
        functools.partial(_gdn_kernel, tb=tb),
        grid=(b // bp, n_tb),
        in_specs=[
            pl.BlockSpec((bp, tb, gw), lambda bi, ti: (bi, ti, first)),
            pl.BlockSpec((bp, tb, gw), lambda bi, ti: (bi, ti, first + 1)),
            pl.BlockSpec((bp, tb, gw), lambda bi, ti: (bi, ti, first + 2)),
            pl.BlockSpec((bp, tb, gw), lambda bi, ti: (bi, ti, first + 3)),
            pl.BlockSpec((bp, tb, AB_COLS), lambda bi, ti: (bi, ti, 0)),
            pl.BlockSpec((AB_COLS, tb), lambda bi, ti: (0, bp * bi * n_tb + ti)),
            pl.BlockSpec((AB_COLS, tb), lambda bi, ti: (0, (bp * bi + 1) * n_tb + ti)),
            pl.BlockSpec((CONV_K, 3 * gw), lambda bi, ti: (0, 0)),
            pl.BlockSpec((1, GDN_HEADS), lambda bi, ti: (0, 0)),
            pl.BlockSpec((1, GDN_HEADS), lambda bi, ti: (0, 0)),
            pl.BlockSpec((GDN_HEADS, 1), lambda bi, ti: (0, 0)),
            pl.BlockSpec((GDN_HEADS, 1), lambda bi, ti: (0, 0)),
            pl.BlockSpec((1, GDN_HEAD_DIM), lambda bi, ti: (0, 0)),
        ],
        out_specs=pl.BlockSpec((bp, tb, gw), lambda bi, ti: (bi, ti, 0)),
        out_shape=jax.ShapeDtypeStruct((b, t, gw), BF16),
        scratch_shapes=[
            pltpu.VMEM((bp * GDN_HEADS, GDN_HEAD_DIM, GDN_HEAD_DIM), F32),
            pltpu.VMEM((bp * 3, BF16_ROWS + GDN_CHUNK, gw), BF16),
            pltpu.VMEM((bp, tb, gw), BF16),
            pltpu.VMEM((bp, tb, gw), BF16),
            pltpu.VMEM((bp, tb, gw), BF16),
            pltpu.VMEM((bp, tb, GDN_HEADS), F32),
            pltpu.VMEM((bp * n_chunk, GDN_HEADS, GDN_CHUNK), F32),
            pltpu.VMEM((bp, tb, GDN_HEADS), F32),
            pltpu.VMEM((bp * n_chunk, GDN_HEADS, GDN_CHUNK), F32),
            pltpu.VMEM((n_inst * LST_PITCH, HEAD_LANES), F32),
            pltpu.VMEM((n_inst * LST_PITCH, HEAD_LANES), F32),
            pltpu.VMEM((GDN_CHUNK, GDN_CHUNK, n_inst), F32),
            pltpu.VMEM((GDN_CHUNK, GDN_CHUNK, n_inst), F32),
            pltpu.VMEM((n_inst, GDN_CHUNK, GDN_CHUNK), BF16),
        ],
        compiler_params=pltpu.CompilerParams(
            dimension_semantics=("parallel", "arbitrary"), vmem_limit_bytes=GDN_VMEM_LIMIT),
        name="gdn",
    )(proj3, proj3, proj3, proj3, ab3, abt2, abt2, conv_w,
      a_log.reshape(1, GDN_HEADS), dt_bias.reshape(1, GDN_HEADS),
      a_log.reshape(GDN_HEADS, 1), dt_bias.reshape(GDN_HEADS, 1),
      gdn_norm_w.reshape(1, GDN_HEAD_DIM))


def _outproj_kernel(oda_ref, ogdn_ref, w_ref, x_ref, fw_ref, o_ref, *, final):
    acc = _dot(oda_ref[...], w_ref[0:DA_WIDTH, :]) + _dot(ogdn_ref[...], w_ref[DA_WIDTH:DA_WIDTH + GDN_WIDTH, :])
    y = x_ref[...] + acc
    if final:
        y = y * lax.rsqrt(jnp.mean(y * y, axis=-1, keepdims=True) + RMS_EPS) * fw_ref[...]
    o_ref[...] = y


def _outproj(o_da2, o_gdn2, w_out_bf16, layer, x2, final_w, final):
    m = x2.shape[0]
    tm = OUTPROJ_TM
    return pl.pallas_call(
        functools.partial(_outproj_kernel, final=final),
        grid=(m // tm,),
        in_specs=[
            pl.BlockSpec((tm, DA_WIDTH), lambda i: (i, 0)),
            pl.BlockSpec((tm, GDN_WIDTH), lambda i: (i, 0)),
            pl.BlockSpec((None, DA_WIDTH + GDN_WIDTH, D_MODEL), lambda i: (layer, 0, 0)),
            pl.BlockSpec((tm, D_MODEL), lambda i: (i, 0)),
            pl.BlockSpec((1, D_MODEL), lambda i: (0, 0)),
        ],
        out_specs=pl.BlockSpec((tm, D_MODEL), lambda i: (i, 0)),
        out_shape=jax.ShapeDtypeStruct((m, D_MODEL), F32),
        compiler_params=pltpu.CompilerParams(
            dimension_semantics=("parallel",), vmem_limit_bytes=VMEM_LIMIT),
        name="outproj",
    )(o_da2, o_gdn2, w_out_bf16, x2, final_w)


def kernel(x, norm_w, w_in, w_out, lambda_q1, lambda_k1, lambda_q2, lambda_k2, da_subln_w, rel_bias,
           conv_w, a_log, dt_bias, gdn_norm_w, final_norm_w):
    b, t, d = x.shape
    m = b * t
    x2 = x.reshape(m, d)
    near_bias = _near_bias(rel_bias)
    final_w = final_norm_w.reshape(1, d)
    w_out_bf16 = w_out.astype(BF16)
    w_in_t = jnp.swapaxes(w_in, 1, 2)
    for l in range(DEPTH):
        lambda_init = 0.8 - 0.6 * math.exp(-0.3 * l)
        proj, ab, abt = _inproj(x2, norm_w[l].reshape(1, d), w_in_t, l)
        proj3 = proj.reshape(b, t, MAIN_COLS)
        lam_params = jnp.stack([lambda_q1[l], lambda_k1[l], lambda_q2[l], lambda_k2[l]])
        o_da = _attention(proj3, near_bias, lam_params, da_subln_w[l].reshape(1, 2 * DA_HEAD_DIM), lambda_init)
        o_gdn = _gdn(proj3, ab.reshape(b, t, AB_COLS), abt, conv_w[l], a_log[l], dt_bias[l], gdn_norm_w[l])
        x2 = _outproj(o_da.reshape(m, DA_WIDTH), o_gdn.reshape(m, GDN_WIDTH), w_out_bf16, l,
                      x2, final_w, final=(l == DEPTH - 1))
    return x2.reshape(b, t, d)
```

```python
import functools
import math

import jax
import jax.numpy as jnp
from jax import lax
from jax.experimental import pallas as pl
from jax.experimental.pallas import tpu as pltpu

F32 = jnp.float32
BF16 = jnp.bfloat16

D_MODEL = 2048
DEPTH = 2
DA_HEADS = 8
DA_HEAD_DIM = 64
DA_WIDTH = DA_HEADS * 2 * DA_HEAD_DIM
GDN_HEADS = 8
GDN_HEAD_DIM = 128
GDN_WIDTH = GDN_HEADS * GDN_HEAD_DIM
CONV_K = 4
GDN_CHUNK = 64
REL_BUCKETS = 32
REL_MAX_DIST = 128
REL_MAX_EXACT = REL_BUCKETS // 2
RMS_EPS = 1e-6
NEG_INF = -1e30
MAIN_COLS = 4 * DA_WIDTH + 4 * GDN_WIDTH
AB_COLS = 2 * GDN_HEADS
HEAD_LANES = 128
BF16_ROWS = 16
LOG2_E = math.log2(math.e)

INPROJ_TM = 1024
INPROJ_TN = 1024
ATTN_TQ = 512
ATTN_TK = 512
ATTN_LANE_TILE = 256
ATTN_HEADS_PER_STEP = 2
BIAS_BLOCK = (64, 128)
GDN_TB = 512
GDN_BATCH_PAIR = 2
OUTPROJ_TM = 512
VMEM_LIMIT = 48 * 1024 * 1024
GDN_VMEM_LIMIT = 56 * 1024 * 1024

NT_DIMS = (((1,), (1,)), ((), ()))


def _sigmoid(x):
    return 1.0 / (1.0 + jnp.exp(-x))


def _silu(x):
    h = 0.5 * x
    return h + h * jnp.tanh(h)


def _softplus(x):
    return jnp.maximum(x, 0.0) + jnp.log(1.0 + jnp.exp(-jnp.abs(x)))


def _dot(a, b):
    return jnp.dot(a, b, preferred_element_type=F32)


def _split_hi_lo(a):
    hi = a.astype(BF16)
    lo = (a - hi.astype(F32)).astype(BF16)
    return hi, lo


def _dot_split(a, b):
    a_hi, a_lo = _split_hi_lo(a)
    b_hi, b_lo = _split_hi_lo(b)
    return _dot(a_hi, b_hi) + _dot(a_lo, b_hi) + _dot(a_hi, b_lo)


def _inproj_kernel(x_ref, nw_ref, w_ref, wab_ref, proj_ref, ab_ref, abt_ref, h_ref):
    @pl.when(pl.program_id(1) == 0)
    def _():
        x = x_ref[...]
        ms = jnp.mean(x * x, axis=-1, keepdims=True)
        h = (x * lax.rsqrt(ms + RMS_EPS) * nw_ref[...]).astype(BF16)
        h_ref[...] = h
        w_ab = jnp.concatenate(
            [wab_ref[...].astype(BF16), jnp.zeros((HEAD_LANES - AB_COLS, D_MODEL), BF16)], axis=0)
        ab = lax.dot_general(h, w_ab, NT_DIMS, preferred_element_type=F32)
        ab_ref[...] = ab[:, 0:AB_COLS]
        abt_ref[...] = ab.T[0:AB_COLS, :]

    proj_ref[...] = lax.dot_general(h_ref[...], w_ref[...].astype(BF16), NT_DIMS,
                                    preferred_element_type=F32).astype(BF16)


def _inproj(x2, norm_w, w_in_t, layer):
    m = x2.shape[0]
    tm, tn = INPROJ_TM, INPROJ_TN
    return pl.pallas_call(
        _inproj_kernel,
        grid=(m // tm, MAIN_COLS // tn),
        in_specs=[
            pl.BlockSpec((tm, D_MODEL), lambda i, j: (i, 0)),
            pl.BlockSpec((1, D_MODEL), lambda i, j: (0, 0)),
            pl.BlockSpec((None, tn, D_MODEL), lambda i, j: (layer, j, 0)),
            pl.BlockSpec((None, AB_COLS, D_MODEL), lambda i, j: (layer, MAIN_COLS // AB_COLS, 0)),
        ],
        out_specs=[
            pl.BlockSpec((tm, tn), lambda i, j: (i, j)),
            pl.BlockSpec((tm, AB_COLS), lambda i, j: (i, 0)),
            pl.BlockSpec((AB_COLS, tm), lambda i, j: (0, i)),
        ],
        out_shape=[
            jax.ShapeDtypeStruct((m, MAIN_COLS), BF16),
            jax.ShapeDtypeStruct((m, AB_COLS), F32),
            jax.ShapeDtypeStruct((AB_COLS, m), F32),
        ],
        scratch_shapes=[pltpu.VMEM((tm, D_MODEL), BF16)],
        compiler_params=pltpu.CompilerParams(
            dimension_semantics=("parallel", "arbitrary"), vmem_limit_bytes=GDN_VMEM_LIMIT),
        name="inproj",
    )(x2, norm_w, w_in_t, w_in_t)


def _bias_kernel(rb_ref, o_ref, *, tk):
    h = pl.program_id(0)
    far = rb_ref[REL_BUCKETS - 1, h]
    br, bc = BIAS_BLOCK
    shape = (br, bc)
    for rb in range(o_ref.shape[0] // br):
        for cb in range(o_ref.shape[1] // bc):
            blk = (slice(rb * br, (rb + 1) * br), slice(cb * bc, (cb + 1) * bc))
            d_min = cb * bc - (rb * br + br - 1) + tk
            d_max = cb * bc + bc - 1 - rb * br + tk
            if d_max < 0:
                o_ref[blk] = jnp.full(shape, NEG_INF, F32)
            elif d_min >= REL_MAX_DIST:
                o_ref[blk] = jnp.zeros(shape, F32)
            else:
                c = lax.broadcasted_iota(jnp.int32, shape, 0) + rb * br
                r = lax.broadcasted_iota(jnp.int32, shape, 1) + cb * bc
                dist = r - c + tk
                n = jnp.maximum(dist, 0)
                nf = jnp.maximum(n, REL_MAX_EXACT).astype(F32)
                large = REL_MAX_EXACT + (jnp.log(nf / REL_MAX_EXACT) / math.log(REL_MAX_DIST / REL_MAX_EXACT)
                                         * (REL_BUCKETS - REL_MAX_EXACT)).astype(jnp.int32)
                large = jnp.minimum(large, REL_BUCKETS - 1)
                bucket = jnp.where(n < REL_MAX_EXACT, n, large)
                val = jnp.zeros(shape, F32)
                for b in range(REL_BUCKETS - 1):
                    val = jnp.where(bucket == b, rb_ref[b, h] - far, val)
                o_ref[blk] = jnp.where(dist >= 0, val * LOG2_E, NEG_INF)


def _near_bias(rel_bias):
    tq, tk = ATTN_TQ, ATTN_TK
    return pl.pallas_call(
        functools.partial(_bias_kernel, tk=tk),
        grid=(DA_HEADS,),
        in_specs=[pl.BlockSpec(memory_space=pltpu.SMEM)],
        out_specs=pl.BlockSpec((None, 2 * tk, tq), lambda h: (h, 0, 0)),
        out_shape=jax.ShapeDtypeStruct((DA_HEADS, 2 * tk, tq), F32),
        name="near_bias",
    )(rel_bias)


def _attn_kernel(q_ref, qnext_ref, k_ref, v_ref, g_ref, bias_ref, lam_ref, sw_ref, o_ref,
                 qz_ref, vt_ref, m_ref, acc_ref, sa_ref, sb_ref, sc_ref, *, tq, tk, lambda_init):
    i = pl.program_id(2)
    hl = HEAD_LANES
    lt = ATTN_LANE_TILE
    hp = ATTN_HEADS_PER_STEP
    n_tiles = 2 * tq // lt
    q_tiles = tq // lt
    head_cols = [slice(hd * hl, (hd + 1) * hl) for hd in range(hp)]

    units = [(hd, t) for hd in range(hp) for t in range(n_tiles)]
    cols = [slice(t * lt, (t + 1) * lt) for t in range(n_tiles)]

    def set_queries(src_ref):
        for hd in range(hp):
            q = src_ref[:, head_cols[hd]] * (DA_HEAD_DIM ** -0.5 * LOG2_E)
            lane = lax.broadcasted_iota(jnp.int32, q.shape, 1)
            zero = jnp.zeros_like(q)
            qz_ref[hd, 0:tq, :] = jnp.where(lane < DA_HEAD_DIM, q, zero)
            qz_ref[hd, tq:2 * tq, :] = jnp.where(lane >= DA_HEAD_DIM, q, zero)

    def keys_used(t, diag):
        return (t % q_tiles + 1) * lt if diag else tk

    def scores(j, s_ref, diag=False):
        row0 = pl.multiple_of(j * tk, tk)
        for u, (hd, t) in enumerate(units):
            nk = keys_used(t, diag)
            kj = k_ref[pl.ds(row0, nk), head_cols[hd]]
            s_ref[u, 0:nk, :] = lax.dot_general(kj, qz_ref[hd, cols[t], :], NT_DIMS, preferred_element_type=F32)

    @pl.when(i == 0)
    def _():
        ones = jnp.ones((vt_ref.shape[2] - hl, tk), BF16)
        for hd in range(hp):
            for j in range(vt_ref.shape[1]):
                vt_ref[hd, j, 0:hl, :] = v_ref[j * tk:(j + 1) * tk, head_cols[hd]].astype(F32).T.astype(BF16)
                vt_ref[hd, j, hl:, :] = ones
        set_queries(q_ref)
        scores(0, sa_ref)

    m_ref[...] = jnp.full(m_ref.shape, NEG_INF, F32)
    acc_ref[...] = jnp.zeros(acc_ref.shape, F32)

    def add_bias(a, hd, t, tile_row0, first_row):
        n_rows = a.shape[0]
        if first_row >= n_rows:
            return a
        b = bias_ref[hd, tile_row0 + first_row:tile_row0 + n_rows, cols[t % q_tiles]]
        if first_row == 0:
            return a + b
        return jnp.concatenate([a[0:first_row, :], a[first_row:, :] + b], axis=0)

    def softmax_pv(j, s_ref, kind="far"):
        diag = kind == "diag"
        nk = [keys_used(t, diag) for _, t in units]
        s = [s_ref[u, 0:nk[u], :] for u in range(len(units))]
        if kind == "prev":
            s = [add_bias(s[u], hd, t, 0, tk - REL_MAX_DIST) if t % q_tiles == 0 else s[u]
                 for u, (hd, t) in enumerate(units)]
        elif diag:
            s = [add_bias(s[u], hd, t, tk, max((t % q_tiles) * lt - REL_MAX_DIST, 0))
                 for u, (hd, t) in enumerate(units)]
        m_prev = [m_ref[hd, :, cols[t]] for hd, t in units]
        m_new = [jnp.maximum(mp, jnp.max(a, axis=0, keepdims=True)) for mp, a in zip(m_prev, s)]
        p = [jnp.exp2(a - mn).astype(BF16) for a, mn in zip(s, m_new)]
        pv = [_dot(vt_ref[hd, j, :, 0:nk[u]], p[u]) for u, (hd, t) in enumerate(units)]
        for u, (hd, t) in enumerate(units):
            acc_ref[hd, :, cols[t]] = jnp.exp2(m_prev[u] - m_new[u]) * acc_ref[hd, :, cols[t]] + pv[u]
            m_ref[hd, :, cols[t]] = m_new[u]

    n_far = jnp.maximum(i - 1, 0)
    odd = n_far % 2

    def far_pair(jj, carry):
        j = 2 * jj
        scores(j + 1, sb_ref)
        softmax_pv(j, sa_ref)
        scores(j + 2, sa_ref)
        softmax_pv(j + 1, sb_ref)
        return carry

    lax.fori_loop(0, n_far // 2, far_pair, 0)

    def finish(prev_ref):
        if prev_ref is None:
            softmax_pv(0, sa_ref, "diag")
        else:
            scores(i, sb_ref, diag=True)
            softmax_pv(n_far, prev_ref, "prev")
        set_queries(qnext_ref)
        scores(0, sa_ref)
        if prev_ref is not None:
            softmax_pv(i, sb_ref, "diag")

        lp = lam_ref[...]
        s1 = jnp.sum(lp[0:1, :] * lp[1:2, :], axis=-1, keepdims=True)
        s2 = jnp.sum(lp[2:3, :] * lp[3:4, :], axis=-1, keepdims=True)
        lam = jnp.exp(s1) - jnp.exp(s2) + lambda_init
        for hd in range(hp):
            inv_l = 1.0 / acc_ref[hd, hl:hl + 1, :]
            o_t = (acc_ref[hd, 0:hl, 0:tq] * inv_l[:, 0:tq]
                   - lam * (acc_ref[hd, 0:hl, tq:2 * tq] * inv_l[:, tq:2 * tq]))
            o = o_t.T
            ms = jnp.mean(o * o, axis=-1, keepdims=True)
            y = o * lax.rsqrt(ms + RMS_EPS) * sw_ref[...] * (1.0 - lambda_init)
            o_ref[:, head_cols[hd]] = (y * _silu(g_ref[:, head_cols[hd]].astype(F32))).astype(BF16)

    @pl.when(i == 0)
    def _():
        finish(None)

    @pl.when(jnp.logical_and(i > 0, odd == 0))
    def _():
        finish(sa_ref)

    @pl.when(odd == 1)
    def _():
        scores(n_far, sc_ref)
        softmax_pv(n_far - 1, sa_ref)
        finish(sc_ref)


def _attention(proj3, near_bias, lam_params, subln_w, lambda_init):
    b, t, _ = proj3.shape
    tq, tk = ATTN_TQ, ATTN_TK
    assert tq == tk and t % tq == 0
    hl = HEAD_LANES
    hp = ATTN_HEADS_PER_STEP
    wide = hp * hl
    sect = DA_WIDTH // wide
    n_units = hp * (2 * tq // ATTN_LANE_TILE)
    return pl.pallas_call(
        functools.partial(_attn_kernel, tq=tq, tk=tk, lambda_init=lambda_init),
        grid=(b, DA_HEADS // hp, t // tq),
        in_specs=[
            pl.BlockSpec((None, tq, wide), lambda bi, h, i: (bi, i, h)),
            pl.BlockSpec((None, tq, wide), lambda bi, h, i: (bi, jnp.minimum(i + 1, t // tq - 1), h)),
            pl.BlockSpec((None, t, wide), lambda bi, h, i: (bi, 0, sect + h)),
            pl.BlockSpec((None, t, wide), lambda bi, h, i: (bi, 0, 2 * sect + h)),
            pl.BlockSpec((None, tq, wide), lambda bi, h, i: (bi, i, 3 * sect + h)),
            pl.BlockSpec((hp, 2 * tk, tq), lambda bi, h, i: (h, 0, 0)),
            pl.BlockSpec((4, DA_HEAD_DIM), lambda bi, h, i: (0, 0)),
            pl.BlockSpec((1, hl), lambda bi, h, i: (0, 0)),
        ],
        out_specs=pl.BlockSpec((None, tq, wide), lambda bi, h, i: (bi, i, h)),
        out_shape=jax.ShapeDtypeStruct((b, t, DA_WIDTH), BF16),
        scratch_shapes=[
            pltpu.VMEM((hp, 2 * tq, hl), BF16),
            pltpu.VMEM((hp, t // tk, hl + BF16_ROWS, tk), BF16),
            pltpu.VMEM((hp, 1, 2 * tq), F32),
            pltpu.VMEM((hp, hl + BF16_ROWS, 2 * tq), F32),
            pltpu.VMEM((n_units, tk, ATTN_LANE_TILE), F32),
            pltpu.VMEM((n_units, tk, ATTN_LANE_TILE), F32),
            pltpu.VMEM((n_units, tk, ATTN_LANE_TILE), F32),
        ],
        compiler_params=pltpu.CompilerParams(
            dimension_semantics=("parallel", "parallel", "arbitrary"), vmem_limit_bytes=VMEM_LIMIT),
        name="diff_attn",
    )(proj3, proj3, proj3, proj3, proj3, near_bias, lam_params, subln_w)


LST_PITCH = 72
SUBLANES = 8
XPOSE_LEAD = 4


def _unit_lower_inverse_on_lanes(lst_ref, ainv_ref, lt_ref, tt_ref, n_inst):
    c_len = GDN_CHUNK
    sub = lax.broadcasted_iota(jnp.int32, (SUBLANES, n_inst), 0)
    zero = jnp.zeros((SUBLANES, n_inst), F32)

    def gather(r):
        lt_ref[r] = lst_ref[pl.ds(r, n_inst, stride=LST_PITCH), :].T[0:c_len, :]

    def scatter(r):
        t = tt_ref[r]
        ainv_ref[pl.ds(r, n_inst, stride=LST_PITCH), :] = jnp.concatenate([t, jnp.zeros_like(t)], axis=0).T

    for r in range(XPOSE_LEAD):
        gather(r)
    for i in range(c_len):
        if i + XPOSE_LEAD < c_len:
            gather(i + XPOSE_LEAD)
        if i >= XPOSE_LEAD:
            scatter(i - XPOSE_LEAD)
        n_blk = (i - 1) // SUBLANES + 1 if i > 0 else 0
        acc = [None] * n_blk
        for k in range(i):
            lik = jnp.broadcast_to(lt_ref[i, k:k + 1, :], (SUBLANES, n_inst))
            for jb in range(k // SUBLANES + 1):
                term = lik * tt_ref[k, jb * SUBLANES:(jb + 1) * SUBLANES, :]
                acc[jb] = term if acc[jb] is None else acc[jb] + term
        for jb in range(c_len // SUBLANES):
            val = -acc[jb] if jb < n_blk else zero
            if jb == i // SUBLANES:
                val = jnp.where(sub == i % SUBLANES, 1.0, val)
            tt_ref[i, jb * SUBLANES:(jb + 1) * SUBLANES, :] = val
    for r in range(c_len - XPOSE_LEAD, c_len):
        scatter(r)


def _gdn_kernel(qp_ref, kp_ref, vp_ref, z_ref, ab_ref, abt0_ref, abt1_ref, cw_ref, alog_ref, dtb_ref,
                alogt_ref, dtbt_ref, nw_ref, o_ref,
                s_ref, cbuf_ref, qn_ref, kn_ref, vn_ref, gc_ref, gct_ref, beta_ref, bt_ref,
                lst_ref, ainv_ref, lt_ref, tt_ref, qk_ref, *, tb):
    c_len = GDN_CHUNK
    hl = HEAD_LANES
    bp = GDN_BATCH_PAIR
    n_chunk = tb // c_len
    n_inst = bp * n_chunk * GDN_HEADS
    halo = BF16_ROWS
    abt_refs = (abt0_ref, abt1_ref)

    @pl.when(pl.program_id(1) == 0)
    def _():
        s_ref[...] = jnp.zeros(s_ref.shape, F32)
        cbuf_ref[:, 0:halo, :] = jnp.zeros((bp * 3, halo, GDN_WIDTH), BF16)

    sel_r = lax.broadcasted_iota(jnp.int32, ((CONV_K - 1) * c_len, halo + c_len), 0)
    sel_c = lax.broadcasted_iota(jnp.int32, ((CONV_K - 1) * c_len, halo + c_len), 1)
    tap = sel_r // c_len
    shift_mat = (sel_c == sel_r - tap * c_len + halo - (CONV_K - 1) + tap).astype(BF16)

    def conv_chunk(cc, carry):
        bb = cc // n_chunk
        rows = pl.ds(pl.multiple_of((cc - bb * n_chunk) * c_len, c_len), c_len)
        for g, (src, dst) in enumerate(((qp_ref, qn_ref), (kp_ref, kn_ref), (vp_ref, vn_ref))):
            stage = bb * 3 + g
            xb = src[bb, rows, :]
            cbuf_ref[stage, halo:halo + c_len, :] = xb
            w = 0.5 * cw_ref[:, g * GDN_WIDTH:(g + 1) * GDN_WIDTH]
            taps = _dot(shift_mat, cbuf_ref[stage])
            hy = w[0:1, :] * taps[0:c_len, :]
            hy = hy + w[1:2, :] * taps[c_len:2 * c_len, :]
            hy = hy + w[2:3, :] * taps[2 * c_len:3 * c_len, :]
            hy = hy + w[3:4, :] * xb.astype(F32)
            cbuf_ref[stage, 0:halo, :] = xb[c_len - halo:c_len, :]
            y = hy + hy * jnp.tanh(hy)
            for h in range(GDN_HEADS):
                cs = slice(h * hl, (h + 1) * hl)
                yh = y[:, cs]
                if g < 2:
                    inv_norm = lax.rsqrt(jnp.sum(yh * yh, axis=-1, keepdims=True) + RMS_EPS)
                    if g == 0:
                        inv_norm = inv_norm * (GDN_HEAD_DIM ** -0.5)
                    yh = yh * inv_norm
                dst[bb, rows, cs] = yh.astype(BF16)
        return carry

    lax.fori_loop(0, bp * n_chunk, conv_chunk, 0)

    row = lax.broadcasted_iota(jnp.int32, (c_len, c_len), 0)
    col = lax.broadcasted_iota(jnp.int32, (c_len, c_len), 1)
    tril = row >= col
    strict = row > col
    tril_f = tril.astype(F32)
    triu_f = (row <= col).astype(F32)
    for bb in range(bp):
        ab = ab_ref[bb]
        beta_ref[bb] = _sigmoid(ab[:, 0:GDN_HEADS])
        g_raw = -jnp.exp(alog_ref[...]) * _softplus(ab[:, GDN_HEADS:AB_COLS] + dtb_ref[...])
        abt = abt_refs[bb][...]
        beta_t = _sigmoid(abt[0:GDN_HEADS, :])
        g_raw_t = -jnp.exp(alogt_ref[...]) * _softplus(abt[GDN_HEADS:AB_COLS, :] + dtbt_ref[...])
        for c in range(n_chunk):
            cc = bb * n_chunk + c
            gc_ref[bb, c * c_len:(c + 1) * c_len, :] = _dot_split(tril_f, g_raw[c * c_len:(c + 1) * c_len, :])
            gct_ref[cc] = _dot_split(g_raw_t[:, c * c_len:(c + 1) * c_len], triu_f)
            bt_ref[cc] = beta_t[:, c * c_len:(c + 1) * c_len]

    heads = range(GDN_HEADS)
    lanes = [slice(h * hl, (h + 1) * hl) for h in heads]

    def intra_chunk(cc, carry):
        bb = cc // n_chunk
        rows = pl.ds(pl.multiple_of((cc - bb * n_chunk) * c_len, c_len), c_len)
        gct = gct_ref[cc]
        gcc = gc_ref[bb, rows, :]
        betac = beta_ref[bb, rows, :]
        ks = [kn_ref[bb, rows, lanes[h]] for h in heads]
        kq = [jnp.concatenate([ks[h], qn_ref[bb, rows, lanes[h]]], axis=0) for h in heads]
        kk = [lax.dot_general(kq[h], ks[h], NT_DIMS, preferred_element_type=F32) for h in heads]
        decay = [jnp.where(tril, jnp.exp(jnp.where(tril, gcc[:, h:h + 1] - gct[h:h + 1, :], 0.0)), 0.0)
                 for h in heads]
        pad = jnp.zeros((c_len, hl - c_len), F32)
        for h in heads:
            l_strict = jnp.where(strict, kk[h][0:c_len, :] * decay[h] * betac[:, h:h + 1], 0.0)
            base = pl.multiple_of((cc * GDN_HEADS + h) * LST_PITCH, SUBLANES)
            lst_ref[pl.ds(base, c_len), :] = jnp.concatenate([l_strict, pad], axis=1)
            qk_ref[cc * GDN_HEADS + h] = (kk[h][c_len:2 * c_len, :] * decay[h]).astype(BF16)
        return carry

    lax.fori_loop(0, bp * n_chunk, intra_chunk, 0)

    _unit_lower_inverse_on_lanes(lst_ref, ainv_ref, lt_ref, tt_ref, n_inst)

    chains = [(bb, h) for bb in range(bp) for h in heads]
    idx = range(len(chains))

    def scan_chunk(c, carry):
        rows = pl.ds(pl.multiple_of(c * c_len, c_len), c_len)
        ccs = [bb * n_chunk + c for bb in range(bp)]
        gct = [gct_ref[cc] for cc in ccs]
        gcc = [gc_ref[bb, rows, :] for bb in range(bp)]
        bct = [bt_ref[cc] for cc in ccs]
        inst = [ccs[bb] * GDN_HEADS + h for bb, h in chains]
        g_row = [gct[bb][h:h + 1, :] for bb, h in chains]
        g_last = [g[:, c_len - 1:c_len] for g in g_row]
        ks = [kn_ref[bb, rows, lanes[h]] for bb, h in chains]
        a_inv = [ainv_ref[pl.ds(pl.multiple_of(n * LST_PITCH, SUBLANES), c_len), 0:c_len] for n in inst]
        b_row = [bct[bb][h:h + 1, :] for bb, h in chains]
        u = [_dot((a_inv[x] * b_row[x]).astype(BF16), vn_ref[bb, rows, lanes[h]]) for x, (bb, h) in enumerate(chains)]
        w = [_dot((a_inv[x] * (b_row[x] * jnp.exp(g_row[x]))).astype(BF16), ks[x]) for x in idx]
        s_old = [s_ref[bb * GDN_HEADS + h] for bb, h in chains]
        ws = [_dot(jnp.concatenate([w[x].astype(BF16), qn_ref[bb, rows, lanes[h]]], axis=0),
                   s_old[x].astype(BF16)) for x, (bb, h) in enumerate(chains)]
        v_new = [(u[x] - ws[x][0:c_len, :]).astype(BF16) for x in idx]
        kd_t = [(ks[x].astype(F32).T * jnp.exp(g_last[x] - g_row[x])).astype(BF16) for x in idx]
        for x, (bb, h) in enumerate(chains):
            s_ref[bb * GDN_HEADS + h] = s_old[x] * jnp.exp(g_last[x]) + _dot(kd_t[x], v_new[x])
        o = [jnp.exp(gcc[bb][:, h:h + 1]) * ws[x][c_len:2 * c_len, :] + _dot(qk_ref[inst[x]], v_new[x])
             for x, (bb, h) in enumerate(chains)]
        for x, (bb, h) in enumerate(chains):
            on = o[x] * lax.rsqrt(jnp.mean(o[x] * o[x], axis=-1, keepdims=True) + RMS_EPS) * nw_ref[...]
            zc = z_ref[bb, rows, lanes[h]].astype(F32)
            o_ref[bb, rows, lanes[h]] = (on * _silu(zc)).astype(BF16)
        return carry

    lax.fori_loop(0, n_chunk, scan_chunk, 0)


def _gdn(proj3, ab3, abt2, conv_w, a_log, dt_bias, gdn_norm_w):
    b, t, _ = proj3.shape
    tb = GDN_TB
    gw = GDN_WIDTH
    first = 4 * DA_WIDTH // gw
    bp = GDN_BATCH_PAIR
    n_chunk = tb // GDN_CHUNK
    n_inst = bp * n_chunk * GDN_HEADS
    n_tb = t // tb
    assert bp == 2 and b % bp == 0
    assert n_inst == HEAD_LANES, "one (batch row, chunk, head) matrix per lane"
    return pl.pallas_call(
        functools.partial(_gdn_kernel, tb=tb),
        grid=(b // bp, n_tb),
        in_specs=[
            pl.BlockSpec((bp, tb, gw), lambda bi, ti: (bi, ti, first)),
            pl.BlockSpec((bp, tb, gw), lambda bi, ti: (bi, ti, first + 1)),
            pl.BlockSpec((bp, tb, gw), lambda bi, ti: (bi, ti, first + 2)),
            pl.BlockSpec((bp, tb, gw), lambda bi, ti: (bi, ti, first + 3)),
            pl.BlockSpec((bp, tb, AB_COLS), lambda bi, ti: (bi, ti, 0)),
            pl.BlockSpec((AB_COLS, tb), lambda bi, ti: (0, bp * bi * n_tb + ti)),
            pl.BlockSpec((AB_COLS, tb), lambda bi, ti: (0, (bp * bi + 1) * n_tb + ti)),
            pl.BlockSpec((CONV_K, 3 * gw), lambda bi, ti: (0, 0)),
            pl.BlockSpec((1, GDN_HEADS), lambda bi, ti: (0, 0)),
            pl.BlockSpec((1, GDN_HEADS), lambda bi, ti: (0, 0)),
            pl.BlockSpec((GDN_HEADS, 1), lambda bi, ti: (0, 0)),
            pl.BlockSpec((GDN_HEADS, 1), lambda bi, ti: (0, 0)),
            pl.BlockSpec((1, GDN_HEAD_DIM), lambda bi, ti: (0, 0)),
        ],
        out_specs=pl.BlockSpec((bp, tb, gw), lambda bi, ti: (bi, ti, 0)),
        out_shape=jax.ShapeDtypeStruct((b, t, gw), BF16),
        scratch_shapes=[
            pltpu.VMEM((bp * GDN_HEADS, GDN_HEAD_DIM, GDN_HEAD_DIM), F32),
            pltpu.VMEM((bp * 3, BF16_ROWS + GDN_CHUNK, gw), BF16),
            pltpu.VMEM((bp, tb, gw), BF16),
            pltpu.VMEM((bp, tb, gw), BF16),
            pltpu.VMEM((bp, tb, gw), BF16),
            pltpu.VMEM((bp, tb, GDN_HEADS), F32),
            pltpu.VMEM((bp * n_chunk, GDN_HEADS, GDN_CHUNK), F32),
            pltpu.VMEM((bp, tb, GDN_HEADS), F32),
            pltpu.VMEM((bp * n_chunk, GDN_HEADS, GDN_CHUNK), F32),
            pltpu.VMEM((n_inst * LST_PITCH, HEAD_LANES), F32),
            pltpu.VMEM((n_inst * LST_PITCH, HEAD_LANES), F32),
            pltpu.VMEM((GDN_CHUNK, GDN_CHUNK, n_inst), F32),
            pltpu.VMEM((GDN_CHUNK, GDN_CHUNK, n_inst), F32),
            pltpu.VMEM((n_inst, GDN_CHUNK, GDN_CHUNK), BF16),
        ],
        compiler_params=pltpu.CompilerParams(
            dimension_semantics=("parallel", "arbitrary"), vmem_limit_bytes=GDN_VMEM_LIMIT),
        name="gdn",
    )(proj3, proj3, proj3, proj3, ab3, abt2, abt2, conv_w,
      a_log.reshape(1, GDN_HEADS), dt_bias.reshape(1, GDN_HEADS),
      a_log.reshape(GDN_HEADS, 1), dt_bias.reshape(GDN_HEADS, 1),
      gdn_norm_w.reshape(1, GDN_HEAD_DIM))


def _outproj_kernel(oda_ref, ogdn_ref, w_ref, x_ref, fw_ref, o_ref, *, final):
    acc = _dot(oda_ref[...], w_ref[0:DA_WIDTH, :]) + _dot(ogdn_ref[...], w_ref[DA_WIDTH:DA_WIDTH + GDN_WIDTH, :])
    y = x_ref[...] + acc
    if final:
        y = y * lax.rsqrt(jnp.mean(y * y, axis=-1, keepdims=True) + RMS_EPS) * fw_ref[...]
    o_ref[...] = y


def _outproj(o_da2, o_gdn2, w_out_bf16, layer, x2, final_w, final):
    m = x2.shape[0]
    tm = OUTPROJ_TM
    return pl.pallas_call(
        functools.partial(_outproj_kernel, final=final),
        grid=(m // tm,),
        in_specs=[
            pl.BlockSpec((tm, DA_WIDTH), lambda i: (i, 0)),
            pl.BlockSpec((tm, GDN_WIDTH), lambda i: (i, 0)),
            pl.BlockSpec((None, DA_WIDTH + GDN_WIDTH, D_MODEL), lambda i: (layer, 0, 0)),
            pl.BlockSpec((tm, D_MODEL), lambda i: (i, 0)),
            pl.BlockSpec((1, D_MODEL), lambda i: (0, 0)),
        ],
        out_specs=pl.BlockSpec((tm, D_MODEL), lambda i: (i, 0)),
        out_shape=jax.ShapeDtypeStruct((m, D_MODEL), F32),
        compiler_params=pltpu.CompilerParams(
            dimension_semantics=("parallel",), vmem_limit_bytes=VMEM_LIMIT),
        name="outproj",
    )(o_da2, o_gdn2, w_out_bf16, x2, final_w)


def kernel(x, norm_w, w_in, w_out, lambda_q1, lambda_k1, lambda_q2, lambda_k2, da_subln_w, rel_bias,
           conv_w, a_log, dt_bias, gdn_norm_w, final_norm_w):
    b, t, d = x.shape
    m = b * t
    x2 = x.reshape(m, d)
    near_bias = _near_bias(rel_bias)
    final_w = final_norm_w.reshape(1, d)
    w_out_bf16 = w_out.astype(BF16)
    w_in_t = jnp.swapaxes(w_in, 1, 2)
    for l in range(DEPTH):
        lambda_init = 0.8 - 0.6 * math.exp(-0.3 * l)
        proj, ab, abt = _inproj(x2, norm_w[l].reshape(1, d), w_in_t, l)
        proj3 = proj.reshape(b, t, MAIN_COLS)
        lam_params = jnp.stack([lambda_q1[l], lambda_k1[l], lambda_q2[l], lambda_k2[l]])
        o_da = _attention(proj3, near_bias, lam_params, da_subln_w[l].reshape(1, 2 * DA_HEAD_DIM), lambda_init)
        o_gdn = _gdn(proj3, ab.reshape(b, t, AB_COLS), abt, conv_w[l], a_log[l], dt_bias[l], gdn_norm_w[l])
        x2 = _outproj(o_da.reshape(m, DA_WIDTH), o_gdn.reshape(m, GDN_WIDTH), w_out_bf16, l,
                      x2, final_w, final=(l == DEPTH - 1))
    return x2.reshape(b, t, d)
```

```python
import functools
import math

import jax
import jax.numpy as jnp
from jax import lax
from jax.experimental import pallas as pl
from jax.experimental.pallas import tpu as pltpu

F32 = jnp.float32
BF16 = jnp.bfloat16

D_MODEL = 2048
DEPTH = 2
DA_HEADS = 8
DA_HEAD_DIM = 64
DA_WIDTH = DA_HEADS * 2 * DA_HEAD_DIM
GDN_HEADS = 8
GDN_HEAD_DIM = 128
GDN_WIDTH = GDN_HEADS * GDN_HEAD_DIM
CONV_K = 4
GDN_CHUNK = 64
REL_BUCKETS = 32
REL_MAX_DIST = 128
REL_MAX_EXACT = REL_BUCKETS // 2
RMS_EPS = 1e-6
NEG_INF = -1e30
MAIN_COLS = 4 * DA_WIDTH + 4 * GDN_WIDTH
AB_COLS = 2 * GDN_HEADS
HEAD_LANES = 128
BF16_ROWS = 16
LOG2_E = math.log2(math.e)

INPROJ_TM = 1024
INPROJ_TN = 1024
ATTN_TQ = 512
ATTN_TK = 512
ATTN_LANE_TILE = 256
ATTN_HEADS_PER_STEP = 2
BIAS_BLOCK = (64, 128)
GDN_TB = 512
GDN_BATCH_PAIR = 2
OUTPROJ_TM = 512
VMEM_LIMIT = 48 * 1024 * 1024
GDN_VMEM_LIMIT = 56 * 1024 * 1024

NT_DIMS = (((1,), (1,)), ((), ()))


def _sigmoid(x):
    return 1.0 / (1.0 + jnp.exp(-x))


def _silu(x):
    h = 0.5 * x
    return h + h * jnp.tanh(h)


def _softplus(x):
    return jnp.maximum(x, 0.0) + jnp.log(1.0 + jnp.exp(-jnp.abs(x)))


def _dot(a, b):
    return jnp.dot(a, b, preferred_element_type=F32)


def _split_hi_lo(a):
    hi = a.astype(BF16)
    lo = (a - hi.astype(F32)).astype(BF16)
    return hi, lo


def _dot_split(a, b):
    a_hi, a_lo = _split_hi_lo(a)
    b_hi, b_lo = _split_hi_lo(b)
    return _dot(a_hi, b_hi) + _dot(a_lo, b_hi) + _dot(a_hi, b_lo)


def _inproj_kernel(x_ref, nw_ref, w_ref, wab_ref, proj_ref, ab_ref, abt_ref, h_ref):
    @pl.when(pl.program_id(1) == 0)
    def _():
        x = x_ref[...]
        ms = jnp.mean(x * x, axis=-1, keepdims=True)
        h = (x * lax.rsqrt(ms + RMS_EPS) * nw_ref[...]).astype(BF16)
        h_ref[...] = h
        w_ab = jnp.concatenate(
            [wab_ref[...].astype(BF16), jnp.zeros((HEAD_LANES - AB_COLS, D_MODEL), BF16)], axis=0)
        ab = lax.dot_general(h, w_ab, NT_DIMS, preferred_element_type=F32)
        ab_ref[...] = ab[:, 0:AB_COLS]
        abt_ref[...] = ab.T[0:AB_COLS, :]

    proj_ref[...] = lax.dot_general(h_ref[...], w_ref[...].astype(BF16), NT_DIMS,
                                    preferred_element_type=F32).astype(BF16)


def _inproj(x2, norm_w, w_in_t, layer):
    m = x2.shape[0]
    tm, tn = INPROJ_TM, INPROJ_TN
    return pl.pallas_call(
        _inproj_kernel,
        grid=(m // tm, MAIN_COLS // tn),
        in_specs=[
            pl.BlockSpec((tm, D_MODEL), lambda i, j: (i, 0)),
            pl.BlockSpec((1, D_MODEL), lambda i, j: (0, 0)),
            pl.BlockSpec((None, tn, D_MODEL), lambda i, j: (layer, j, 0)),
            pl.BlockSpec((None, AB_COLS, D_MODEL), lambda i, j: (layer, MAIN_COLS // AB_COLS, 0)),
        ],
        out_specs=[
            pl.BlockSpec((tm, tn), lambda i, j: (i, j)),
            pl.BlockSpec((tm, AB_COLS), lambda i, j: (i, 0)),
            pl.BlockSpec((AB_COLS, tm), lambda i, j: (0, i)),
        ],
        out_shape=[
            jax.ShapeDtypeStruct((m, MAIN_COLS), BF16),
            jax.ShapeDtypeStruct((m, AB_COLS), F32),
            jax.ShapeDtypeStruct((AB_COLS, m), F32),
        ],
        scratch_shapes=[pltpu.VMEM((tm, D_MODEL), BF16)],
        compiler_params=pltpu.CompilerParams(
            dimension_semantics=("parallel", "arbitrary"), vmem_limit_bytes=GDN_VMEM_LIMIT),
        name="inproj",
    )(x2, norm_w, w_in_t, w_in_t)


def _bias_kernel(rb_ref, o_ref, *, tk):
    h = pl.program_id(0)
    far = rb_ref[REL_BUCKETS - 1, h]
    br, bc = BIAS_BLOCK
    shape = (br, bc)
    for rb in range(o_ref.shape[0] // br):
        for cb in range(o_ref.shape[1] // bc):
            blk = (slice(rb * br, (rb + 1) * br), slice(cb * bc, (cb + 1) * bc))
            d_min = cb * bc - (rb * br + br - 1) + tk
            d_max = cb * bc + bc - 1 - rb * br + tk
            if d_max < 0:
                o_ref[blk] = jnp.full(shape, NEG_INF, F32)
            elif d_min >= REL_MAX_DIST:
                o_ref[blk] = jnp.zeros(shape, F32)
            else:
                c = lax.broadcasted_iota(jnp.int32, shape, 0) + rb * br
                r = lax.broadcasted_iota(jnp.int32, shape, 1) + cb * bc
                dist = r - c + tk
                n = jnp.maximum(dist, 0)
                nf = jnp.maximum(n, REL_MAX_EXACT).astype(F32)
                large = REL_MAX_EXACT + (jnp.log(nf / REL_MAX_EXACT) / math.log(REL_MAX_DIST / REL_MAX_EXACT)
                                         * (REL_BUCKETS - REL_MAX_EXACT)).astype(jnp.int32)
                large = jnp.minimum(large, REL_BUCKETS - 1)
                bucket = jnp.where(n < REL_MAX_EXACT, n, large)
                val = jnp.zeros(shape, F32)
                for b in range(REL_BUCKETS - 1):
                    val = jnp.where(bucket == b, rb_ref[b, h] - far, val)
                o_ref[blk] = jnp.where(dist >= 0, val * LOG2_E, NEG_INF)


def _near_bias(rel_bias):
    tq, tk = ATTN_TQ, ATTN_TK
    return pl.pallas_call(
        functools.partial(_bias_kernel, tk=tk),
        grid=(DA_HEADS,),
        in_specs=[pl.BlockSpec(memory_space=pltpu.SMEM)],
        out_specs=pl.BlockSpec((None, 2 * tk, tq), lambda h: (h, 0, 0)),
        out_shape=jax.ShapeDtypeStruct((DA_HEADS, 2 * tk, tq), F32),
        name="near_bias",
    )(rel_bias)


def _attn_kernel(q_ref, qnext_ref, k_ref, v_ref, g_ref, bias_ref, lam_ref, sw_ref, o_ref,
                 qz_ref, vt_ref, m_ref, acc_ref, sa_ref, sb_ref, sc_ref, *, tq, tk, lambda_init):
    i = pl.program_id(2)
    hl = HEAD_LANES
    lt = ATTN_LANE_TILE
    hp = ATTN_HEADS_PER_STEP
    n_tiles = 2 * tq // lt
    q_tiles = tq // lt
    head_cols = [slice(hd * hl, (hd + 1) * hl) for hd in range(hp)]

    units = [(hd, t) for hd in range(hp) for t in range(n_tiles)]
    cols = [slice(t * lt, (t + 1) * lt) for t in range(n_tiles)]

    def set_queries(src_ref):
        for hd in range(hp):
            q = src_ref[:, head_cols[hd]] * (DA_HEAD_DIM ** -0.5 * LOG2_E)
            lane = lax.broadcasted_iota(jnp.int32, q.shape, 1)
            zero = jnp.zeros_like(q)
            qz_ref[hd, 0:tq, :] = jnp.where(lane < DA_HEAD_DIM, q, zero)
            qz_ref[hd, tq:2 * tq, :] = jnp.where(lane >= DA_HEAD_DIM, q, zero)

    def keys_used(t, diag):
        return (t % q_tiles + 1) * lt if diag else tk

    def scores(j, s_ref, diag=False):
        row0 = pl.multiple_of(j * tk, tk)
        for u, (hd, t) in enumerate(units):
            nk = keys_used(t, diag)
            kj = k_ref[pl.ds(row0, nk), head_cols[hd]]
            s_ref[u, 0:nk, :] = lax.dot_general(kj, qz_ref[hd, cols[t], :], NT_DIMS, preferred_element_type=F32)

    @pl.when(i == 0)
    def _():
        ones = jnp.ones((vt_ref.shape[2] - hl, tk), BF16)
        for hd in range(hp):
            for j in range(vt_ref.shape[1]):
                vt_ref[hd, j, 0:hl, :] = v_ref[j * tk:(j + 1) * tk, head_cols[hd]].astype(F32).T.astype(BF16)
                vt_ref[hd, j, hl:, :] = ones
        set_queries(q_ref)
        scores(0, sa_ref)

    m_ref[...] = jnp.full(m_ref.shape, NEG_INF, F32)
    acc_ref[...] = jnp.zeros(acc_ref.shape, F32)

    def add_bias(a, hd, t, tile_row0, first_row):
        n_rows = a.shape[0]
        if first_row >= n_rows:
            return a
        b = bias_ref[hd, tile_row0 + first_row:tile_row0 + n_rows, cols[t % q_tiles]]
        if first_row == 0:
            return a + b
        return jnp.concatenate([a[0:first_row, :], a[first_row:, :] + b], axis=0)

    def softmax_pv(j, s_ref, kind="far"):
        diag = kind == "diag"
        nk = [keys_used(t, diag) for _, t in units]
        s = [s_ref[u, 0:nk[u], :] for u in range(len(units))]
        if kind == "prev":
            s = [add_bias(s[u], hd, t, 0, tk - REL_MAX_DIST) if t % q_tiles == 0 else s[u]
                 for u, (hd, t) in enumerate(units)]
        elif diag:
            s = [add_bias(s[u], hd, t, tk, max((t % q_tiles) * lt - REL_MAX_DIST, 0))
                 for u, (hd, t) in enumerate(units)]
        m_prev = [m_ref[hd, :, cols[t]] for hd, t in units]
        m_new = [jnp.maximum(mp, jnp.max(a, axis=0, keepdims=True)) for mp, a in zip(m_prev, s)]
        p = [jnp.exp2(a - mn).astype(BF16) for a, mn in zip(s, m_new)]
        pv = [_dot(vt_ref[hd, j, :, 0:nk[u]], p[u]) for u, (hd, t) in enumerate(units)]
        for u, (hd, t) in enumerate(units):
            acc_ref[hd, :, cols[t]] = jnp.exp2(m_prev[u] - m_new[u]) * acc_ref[hd, :, cols[t]] + pv[u]
            m_ref[hd, :, cols[t]] = m_new[u]

    n_far = jnp.maximum(i - 1, 0)
    odd = n_far % 2

    def far_pair(jj, carry):
        j = 2 * jj
        scores(j + 1, sb_ref)
        softmax_pv(j, sa_ref)
        scores(j + 2, sa_ref)
        softmax_pv(j + 1, sb_ref)
        return carry

    lax.fori_loop(0, n_far // 2, far_pair, 0)

    def finish(prev_ref):
        if prev_ref is None:
            softmax_pv(0, sa_ref, "diag")
        else:
            scores(i, sb_ref, diag=True)
            softmax_pv(n_far, prev_ref, "prev")
        set_queries(qnext_ref)
        scores(0, sa_ref)
        if prev_ref is not None:
            softmax_pv(i, sb_ref, "diag")

        lp = lam_ref[...]
        s1 = jnp.sum(lp[0:1, :] * lp[1:2, :], axis=-1, keepdims=True)
        s2 = jnp.sum(lp[2:3, :] * lp[3:4, :], axis=-1, keepdims=True)
        lam = jnp.exp(s1) - jnp.exp(s2) + lambda_init
        for hd in range(hp):
            inv_l = 1.0 / acc_ref[hd, hl:hl + 1, :]
            o_t = (acc_ref[hd, 0:hl, 0:tq] * inv_l[:, 0:tq]
                   - lam * (acc_ref[hd, 0:hl, tq:2 * tq] * inv_l[:, tq:2 * tq]))
            o = o_t.T
            ms = jnp.mean(o * o, axis=-1, keepdims=True)
            y = o * lax.rsqrt(ms + RMS_EPS) * sw_ref[...] * (1.0 - lambda_init)
            o_ref[:, head_cols[hd]] = (y * _silu(g_ref[:, head_cols[hd]].astype(F32))).astype(BF16)

    @pl.when(i == 0)
    def _():
        finish(None)

    @pl.when(jnp.logical_and(i > 0, odd == 0))
    def _():
        finish(sa_ref)

    @pl.when(odd == 1)
    def _():
        scores(n_far, sc_ref)
        softmax_pv(n_far - 1, sa_ref)
        finish(sc_ref)


def _attention(proj3, near_bias, lam_params, subln_w, lambda_init):
    b, t, _ = proj3.shape
    tq, tk = ATTN_TQ, ATTN_TK
    assert tq == tk and t % tq == 0
    hl = HEAD_LANES
    hp = ATTN_HEADS_PER_STEP
    wide = hp * hl
    sect = DA_WIDTH // wide
    n_units = hp * (2 * tq // ATTN_LANE_TILE)
    return pl.pallas_call(
        functools.partial(_attn_kernel, tq=tq, tk=tk, lambda_init=lambda_init),
        grid=(b, DA_HEADS // hp, t // tq),
        in_specs=[
            pl.BlockSpec((None, tq, wide), lambda bi, h, i: (bi, i, h)),
            pl.BlockSpec((None, tq, wide), lambda bi, h, i: (bi, jnp.minimum(i + 1, t // tq - 1), h)),
            pl.BlockSpec((None, t, wide), lambda bi, h, i: (bi, 0, sect + h)),
            pl.BlockSpec((None, t, wide), lambda bi, h, i: (bi, 0, 2 * sect + h)),
            pl.BlockSpec((None, tq, wide), lambda bi, h, i: (bi, i, 3 * sect + h)),
            pl.BlockSpec((hp, 2 * tk, tq), lambda bi, h, i: (h, 0, 0)),
            pl.BlockSpec((4, DA_HEAD_DIM), lambda bi, h, i: (0, 0)),
            pl.BlockSpec((1, hl), lambda bi, h, i: (0, 0)),
        ],
        out_specs=pl.BlockSpec((None, tq, wide), lambda bi, h, i: (bi, i, h)),
        out_shape=jax.ShapeDtypeStruct((b, t, DA_WIDTH), BF16),
        scratch_shapes=[
            pltpu.VMEM((hp, 2 * tq, hl), BF16),
            pltpu.VMEM((hp, t // tk, hl + BF16_ROWS, tk), BF16),
            pltpu.VMEM((hp, 1, 2 * tq), F32),
            pltpu.VMEM((hp, hl + BF16_ROWS, 2 * tq), F32),
            pltpu.VMEM((n_units, tk, ATTN_LANE_TILE), F32),
            pltpu.VMEM((n_units, tk, ATTN_LANE_TILE), F32),
            pltpu.VMEM((n_units, tk, ATTN_LANE_TILE), F32),
        ],
        compiler_params=pltpu.CompilerParams(
            dimension_semantics=("parallel", "parallel", "arbitrary"), vmem_limit_bytes=VMEM_LIMIT),
        name="diff_attn",
    )(proj3, proj3, proj3, proj3, proj3, near_bias, lam_params, subln_w)


LST_PITCH = 72
SUBLANES = 8
XPOSE_LEAD = 8


def _unit_lower_inverse_on_lanes(lst_ref, ainv_ref, lt_ref, tt_ref, n_inst):
    c_len = GDN_CHUNK
    sub = lax.broadcasted_iota(jnp.int32, (SUBLANES, n_inst), 0)
    zero = jnp.zeros((SUBLANES, n_inst), F32)

    def gather(r):
        lt_ref[r] = lst_ref[pl.ds(r, n_inst, stride=LST_PITCH), :].T[0:c_len, :]

    def scatter(r):
        t = tt_ref[r]
        ainv_ref[pl.ds(r, n_inst, stride=LST_PITCH), :] = jnp.concatenate([t, jnp.zeros_like(t)], axis=0).T

    for r in range(XPOSE_LEAD):
        gather(r)
    for i in range(c_len):
        if i + XPOSE_LEAD < c_len:
            gather(i + XPOSE_LEAD)
        if i >= XPOSE_LEAD:
            scatter(i - XPOSE_LEAD)
        n_blk = (i - 1) // SUBLANES + 1 if i > 0 else 0
        acc = [None] * n_blk
        for k in range(i):
            lik = jnp.broadcast_to(lt_ref[i, k:k + 1, :], (SUBLANES, n_inst))
            for jb in range(k // SUBLANES + 1):
                term = lik * tt_ref[k, jb * SUBLANES:(jb + 1) * SUBLANES, :]
                acc[jb] = term if acc[jb] is None else acc[jb] + term
        for jb in range(c_len // SUBLANES):
            val = -acc[jb] if jb < n_blk else zero
            if jb == i // SUBLANES:
                val = jnp.where(sub == i % SUBLANES, 1.0, val)
            tt_ref[i, jb * SUBLANES:(jb + 1) * SUBLANES, :] = val
    for r in range(c_len - XPOSE_LEAD, c_len):
        scatter(r)


def _gdn_kernel(qp_ref, kp_ref, vp_ref, z_ref, ab_ref, abt0_ref, abt1_ref, cw_ref, alog_ref, dtb_ref,
                alogt_ref, dtbt_ref, nw_ref, o_ref,
                s_ref, cbuf_ref, qn_ref, kn_ref, vn_ref, gc_ref, gct_ref, beta_ref, bt_ref,
                lst_ref, ainv_ref, lt_ref, tt_ref, qk_ref, *, tb):
    c_len = GDN_CHUNK
    hl = HEAD_LANES
    bp = GDN_BATCH_PAIR
    n_chunk = tb // c_len
    n_inst = bp * n_chunk * GDN_HEADS
    halo = BF16_ROWS
    abt_refs = (abt0_ref, abt1_ref)

    @pl.when(pl.program_id(1) == 0)
    def _():
        s_ref[...] = jnp.zeros(s_ref.shape, F32)
        cbuf_ref[:, 0:halo, :] = jnp.zeros((bp * 3, halo, GDN_WIDTH), BF16)

    sel_r = lax.broadcasted_iota(jnp.int32, ((CONV_K - 1) * c_len, halo + c_len), 0)
    sel_c = lax.broadcasted_iota(jnp.int32, ((CONV_K - 1) * c_len, halo + c_len), 1)
    tap = sel_r // c_len
    shift_mat = (sel_c == sel_r - tap * c_len + halo - (CONV_K - 1) + tap).astype(BF16)

    def conv_chunk(c, carry):
        rows = pl.ds(pl.multiple_of(c * c_len, c_len), c_len)
        groups = ((qp_ref, qn_ref), (kp_ref, kn_ref), (vp_ref, vn_ref))
        for bb, (g, (src, dst)) in [(bb, gs) for bb in range(bp) for gs in enumerate(groups)]:
            stage = bb * 3 + g
            xb = src[bb, rows, :]
            cbuf_ref[stage, halo:halo + c_len, :] = xb
            w = 0.5 * cw_ref[:, g * GDN_WIDTH:(g + 1) * GDN_WIDTH]
            taps = _dot(shift_mat, cbuf_ref[stage])
            hy = w[0:1, :] * taps[0:c_len, :]
            hy = hy + w[1:2, :] * taps[c_len:2 * c_len, :]
            hy = hy + w[2:3, :] * taps[2 * c_len:3 * c_len, :]
            hy = hy + w[3:4, :] * xb.astype(F32)
            cbuf_ref[stage, 0:halo, :] = xb[c_len - halo:c_len, :]
            y = hy + hy * jnp.tanh(hy)
            for h in range(GDN_HEADS):
                cs = slice(h * hl, (h + 1) * hl)
                yh = y[:, cs]
                if g < 2:
                    inv_norm = lax.rsqrt(jnp.sum(yh * yh, axis=-1, keepdims=True) + RMS_EPS)
                    if g == 0:
                        inv_norm = inv_norm * (GDN_HEAD_DIM ** -0.5)
                    yh = yh * inv_norm
                dst[bb, rows, cs] = yh.astype(BF16)
        return carry

    lax.fori_loop(0, n_chunk, conv_chunk, 0)

    row = lax.broadcasted_iota(jnp.int32, (c_len, c_len), 0)
    col = lax.broadcasted_iota(jnp.int32, (c_len, c_len), 1)
    tril = row >= col
    strict = row > col
    tril_f = tril.astype(F32)
    triu_f = (row <= col).astype(F32)
    for bb in range(bp):
        ab = ab_ref[bb]
        beta_ref[bb] = _sigmoid(ab[:, 0:GDN_HEADS])
        g_raw = -jnp.exp(alog_ref[...]) * _softplus(ab[:, GDN_HEADS:AB_COLS] + dtb_ref[...])
        abt = abt_refs[bb][...]
        beta_t = _sigmoid(abt[0:GDN_HEADS, :])
        g_raw_t = -jnp.exp(alogt_ref[...]) * _softplus(abt[GDN_HEADS:AB_COLS, :] + dtbt_ref[...])
        for c in range(n_chunk):
            cc = bb * n_chunk + c
            gc_ref[bb, c * c_len:(c + 1) * c_len, :] = _dot_split(tril_f, g_raw[c * c_len:(c + 1) * c_len, :])
            gct_ref[cc] = _dot_split(g_raw_t[:, c * c_len:(c + 1) * c_len], triu_f)
            bt_ref[cc] = beta_t[:, c * c_len:(c + 1) * c_len]

    heads = range(GDN_HEADS)
    lanes = [slice(h * hl, (h + 1) * hl) for h in heads]

    def intra_chunk(c, carry):
        rows = pl.ds(pl.multiple_of(c * c_len, c_len), c_len)
        pad = jnp.zeros((c_len, hl - c_len), F32)
        for bb in range(bp):
            cc = bb * n_chunk + c
            gct = gct_ref[cc]
            gcc = gc_ref[bb, rows, :]
            betac = beta_ref[bb, rows, :]
            ks = [kn_ref[bb, rows, lanes[h]] for h in heads]
            kq = [jnp.concatenate([ks[h], qn_ref[bb, rows, lanes[h]]], axis=0) for h in heads]
            kk = [lax.dot_general(kq[h], ks[h], NT_DIMS, preferred_element_type=F32) for h in heads]
            decay = [jnp.where(tril, jnp.exp(jnp.where(tril, gcc[:, h:h + 1] - gct[h:h + 1, :], 0.0)), 0.0)
                     for h in heads]
            for h in heads:
                l_strict = jnp.where(strict, kk[h][0:c_len, :] * decay[h] * betac[:, h:h + 1], 0.0)
                base = pl.multiple_of((cc * GDN_HEADS + h) * LST_PITCH, SUBLANES)
                lst_ref[pl.ds(base, c_len), :] = jnp.concatenate([l_strict, pad], axis=1)
                qk_ref[cc * GDN_HEADS + h] = (kk[h][c_len:2 * c_len, :] * decay[h]).astype(BF16)
        return carry

    lax.fori_loop(0, n_chunk, intra_chunk, 0)

    _unit_lower_inverse_on_lanes(lst_ref, ainv_ref, lt_ref, tt_ref, n_inst)

    chains = [(bb, h) for bb in range(bp) for h in heads]
    idx = range(len(chains))

    def scan_chunk(c, carry):
        rows = pl.ds(pl.multiple_of(c * c_len, c_len), c_len)
        ccs = [bb * n_chunk + c for bb in range(bp)]
        gct = [gct_ref[cc] for cc in ccs]
        gcc = [gc_ref[bb, rows, :] for bb in range(bp)]
        bct = [bt_ref[cc] for cc in ccs]
        inst = [ccs[bb] * GDN_HEADS + h for bb, h in chains]
        g_row = [gct[bb][h:h + 1, :] for bb, h in chains]
        g_last = [g[:, c_len - 1:c_len] for g in g_row]
        ks = [kn_ref[bb, rows, lanes[h]] for bb, h in chains]
        a_inv = [ainv_ref[pl.ds(pl.multiple_of(n * LST_PITCH, SUBLANES), c_len), 0:c_len] for n in inst]
        b_row = [bct[bb][h:h + 1, :] for bb, h in chains]
        u = [_dot((a_inv[x] * b_row[x]).astype(BF16), vn_ref[bb, rows, lanes[h]]) for x, (bb, h) in enumerate(chains)]
        w = [_dot((a_inv[x] * (b_row[x] * jnp.exp(g_row[x]))).astype(BF16), ks[x]) for x in idx]
        s_old = [s_ref[bb * GDN_HEADS + h] for bb, h in chains]
        ws = [_dot(jnp.concatenate([w[x].astype(BF16), qn_ref[bb, rows, lanes[h]]], axis=0),
                   s_old[x].astype(BF16)) for x, (bb, h) in enumerate(chains)]
        v_new = [(u[x] - ws[x][0:c_len, :]).astype(BF16) for x in idx]
        kd_t = [(ks[x].astype(F32).T * jnp.exp(g_last[x] - g_row[x])).astype(BF16) for x in idx]
        for x, (bb, h) in enumerate(chains):
            s_ref[bb * GDN_HEADS + h] = s_old[x] * jnp.exp(g_last[x]) + _dot(kd_t[x], v_new[x])
        o = [jnp.exp(gcc[bb][:, h:h + 1]) * ws[x][c_len:2 * c_len, :] + _dot(qk_ref[inst[x]], v_new[x])
             for x, (bb, h) in enumerate(chains)]
        for x, (bb, h) in enumerate(chains):
            on = o[x] * lax.rsqrt(jnp.mean(o[x] * o[x], axis=-1, keepdims=True) + RMS_EPS) * nw_ref[...]
            zc = z_ref[bb, rows, lanes[h]].astype(F32)
            o_ref[bb, rows, lanes[h]] = (on * _silu(zc)).astype(BF16)
        return carry

    lax.fori_loop(0, n_chunk, scan_chunk, 0)


def _gdn(proj3, ab3, abt2, conv_w, a_log, dt_bias, gdn_norm_w):
    b, t, _ = proj3.shape
    tb = GDN_TB
    gw = GDN_WIDTH
    first = 4 * DA_WIDTH // gw
    bp = GDN_BATCH_PAIR
    n_chunk = tb // GDN_CHUNK
    n_inst = bp * n_chunk * GDN_HEADS
    n_tb = t // tb
    assert bp == 2 and b % bp == 0
    assert n_inst == HEAD_LANES, "one (batch row, chunk, head) matrix per lane"
    return pl.pallas_call(
        functools.partial(_gdn_kernel, tb=tb),
        grid=(b // bp, n_tb),
        in_specs=[
            pl.BlockSpec((bp, tb, gw), lambda bi, ti: (bi, ti, first)),
            pl.BlockSpec((bp, tb, gw), lambda bi, ti: (bi, ti, first + 1)),
            pl.BlockSpec((bp, tb, gw), lambda bi, ti: (bi, ti, first + 2)),
            pl.BlockSpec((bp, tb, gw), lambda bi, ti: (bi, ti, first + 3)),
            pl.BlockSpec((bp, tb, AB_COLS), lambda bi, ti: (bi, ti, 0)),
            pl.BlockSpec((AB_COLS, tb), lambda bi, ti: (0, bp * bi * n_tb + ti)),
            pl.BlockSpec((AB_COLS, tb), lambda bi, ti: (0, (bp * bi + 1) * n_tb + ti)),
            pl.BlockSpec((CONV_K, 3 * gw), lambda bi, ti: (0, 0)),
            pl.BlockSpec((1, GDN_HEADS), lambda bi, ti: (0, 0)),
            pl.BlockSpec((1, GDN_HEADS), lambda bi, ti: (0, 0)),
            pl.BlockSpec((GDN_HEADS, 1), lambda bi, ti: (0, 0)),
            pl.BlockSpec((GDN_HEADS, 1), lambda bi, ti: (0, 0)),
            pl.BlockSpec((1, GDN_HEAD_DIM), lambda bi, ti: (0, 0)),
        ],
        out_specs=pl.BlockSpec((bp, tb, gw), lambda bi, ti: (bi, ti, 0)),
        out_shape=jax.ShapeDtypeStruct((b, t, gw), BF16),
        scratch_shapes=[
            pltpu.VMEM((bp * GDN_HEADS, GDN_HEAD_DIM, GDN_HEAD_DIM), F32),
            pltpu.VMEM((bp * 3, BF16_ROWS + GDN_CHUNK, gw), BF16),
            pltpu.VMEM((bp, tb, gw), BF16),
            pltpu.VMEM((bp, tb, gw), BF16),
            pltpu.VMEM((bp, tb, gw), BF16),
            pltpu.VMEM((bp, tb, GDN_HEADS), F32),
            pltpu.VMEM((bp * n_chunk, GDN_HEADS, GDN_CHUNK), F32),
            pltpu.VMEM((bp, tb, GDN_HEADS), F32),
            pltpu.VMEM((bp * n_chunk, GDN_HEADS, GDN_CHUNK), F32),
            pltpu.VMEM((n_inst * LST_PITCH, HEAD_LANES), F32),
            pltpu.VMEM((n_inst * LST_PITCH, HEAD_LANES), F32),
            pltpu.VMEM((GDN_CHUNK, GDN_CHUNK, n_inst), F32),
            pltpu.VMEM((GDN_CHUNK, GDN_CHUNK, n_inst), F32),
            pltpu.VMEM((n_inst, GDN_CHUNK, GDN_CHUNK), BF16),
        ],
        compiler_params=pltpu.CompilerParams(
            dimension_semantics=("parallel", "arbitrary"), vmem_limit_bytes=GDN_VMEM_LIMIT),
        name="gdn",
    )(proj3, proj3, proj3, proj3, ab3, abt2, abt2, conv_w,
      a_log.reshape(1, GDN_HEADS), dt_bias.reshape(1, GDN_HEADS),
      a_log.reshape(GDN_HEADS, 1), dt_bias.reshape(GDN_HEADS, 1),
      gdn_norm_w.reshape(1, GDN_HEAD_DIM))


def _outproj_kernel(oda_ref, ogdn_ref, w_ref, x_ref, fw_ref, o_ref, *, final):
    acc = _dot(oda_ref[...], w_ref[0:DA_WIDTH, :]) + _dot(ogdn_ref[...], w_ref[DA_WIDTH:DA_WIDTH + GDN_WIDTH, :])
    y = x_ref[...] + acc
    if final:
        y = y * lax.rsqrt(jnp.mean(y * y, axis=-1, keepdims=True) + RMS_EPS) * fw_ref[...]
    o_ref[...] = y


def _outproj(o_da2, o_gdn2, w_out_bf16, layer, x2, final_w, final):
    m = x2.shape[0]
    tm = OUTPROJ_TM
    return pl.pallas_call(
        functools.partial(_outproj_kernel, final=final),
        grid=(m // tm,),
        in_specs=[
            pl.BlockSpec((tm, DA_WIDTH), lambda i: (i, 0)),
            pl.BlockSpec((tm, GDN_WIDTH), lambda i: (i, 0)),
            pl.BlockSpec((None, DA_WIDTH + GDN_WIDTH, D_MODEL), lambda i: (layer, 0, 0)),
            pl.BlockSpec((tm, D_MODEL), lambda i: (i, 0)),
            pl.BlockSpec((1, D_MODEL), lambda i: (0, 0)),
        ],
        out_specs=pl.BlockSpec((tm, D_MODEL), lambda i: (i, 0)),
        out_shape=jax.ShapeDtypeStruct((m, D_MODEL), F32),
        compiler_params=pltpu.CompilerParams(
            dimension_semantics=("parallel",), vmem_limit_bytes=VMEM_LIMIT),
        name="outproj",
    )(o_da2, o_gdn2, w_out_bf16, x2, final_w)


def kernel(x, norm_w, w_in, w_out, lambda_q1, lambda_k1, lambda_q2, lambda_k2, da_subln_w, rel_bias,
           conv_w, a_log, dt_bias, gdn_norm_w, final_norm_w):
    b, t, d = x.shape
    m = b * t
    x2 = x.reshape(m, d)
    near_bias = _near_bias(rel_bias)
    final_w = final_norm_w.reshape(1, d)
    w_out_bf16 = w_out.astype(BF16)
    w_in_t = jnp.swapaxes(w_in, 1, 2)
    for l in range(DEPTH):
        lambda_init = 0.8 - 0.6 * math.exp(-0.3 * l)
        proj, ab, abt = _inproj(x2, norm_w[l].reshape(1, d), w_in_t, l)
        proj3 = proj.reshape(b, t, MAIN_COLS)
        lam_params = jnp.stack([lambda_q1[l], lambda_k1[l], lambda_q2[l], lambda_k2[l]])
        o_da = _attention(proj3, near_bias, lam_params, da_subln_w[l].reshape(1, 2 * DA_HEAD_DIM), lambda_init)
        o_gdn = _gdn(proj3, ab.reshape(b, t, AB_COLS), abt, conv_w[l], a_log[l], dt_bias[l], gdn_norm_w[l])
        x2 = _outproj(o_da.reshape(m, DA_WIDTH), o_gdn.reshape(m, GDN_WIDTH), w_out_bf16, l,
                      x2, final_w, final=(l == DEPTH - 1))
    return x2.reshape(b, t, d)
```

```python
import functools
import math

import jax
import jax.numpy as jnp
from jax import lax
from jax.experimental import pallas as pl
from jax.experimental.pallas import tpu as pltpu

F32 = jnp.float32
BF16 = jnp.bfloat16

D_MODEL = 2048
DEPTH = 2
DA_HEADS = 8
DA_HEAD_DIM = 64
DA_WIDTH = DA_HEADS * 2 * DA_HEAD_DIM
GDN_HEADS = 8
GDN_HEAD_DIM = 128
GDN_WIDTH = GDN_HEADS * GDN_HEAD_DIM
CONV_K = 4
GDN_CHUNK = 64
REL_BUCKETS = 32
REL_MAX_DIST = 128
REL_MAX_EXACT = REL_BUCKETS // 2
RMS_EPS = 1e-6
NEG_INF = -1e30
MAIN_COLS = 4 * DA_WIDTH + 4 * GDN_WIDTH
AB_COLS = 2 * GDN_HEADS
HEAD_LANES = 128
BF16_ROWS = 16
LOG2_E = math.log2(math.e)

INPROJ_TM = 1024
INPROJ_TN = 1024
ATTN_TQ = 512
ATTN_TK = 512
ATTN_LANE_TILE = 256
ATTN_HEADS_PER_STEP = 2
BIAS_BLOCK = (64, 128)
GDN_TB = 256
GDN_BATCH_ROWS = 4
OUTPROJ_TM = 512
VMEM_LIMIT = 48 * 1024 * 1024
GDN_VMEM_LIMIT = 56 * 1024 * 1024

NT_DIMS = (((1,), (1,)), ((), ()))


def _sigmoid(x):
    return 1.0 / (1.0 + jnp.exp(-x))


def _silu(x):
    h = 0.5 * x
    return h + h * jnp.tanh(h)


def _softplus(x):
    return jnp.maximum(x, 0.0) + jnp.log(1.0 + jnp.exp(-jnp.abs(x)))


def _dot(a, b):
    return jnp.dot(a, b, preferred_element_type=F32)


def _split_hi_lo(a):
    hi = a.astype(BF16)
    lo = (a - hi.astype(F32)).astype(BF16)
    return hi, lo


def _dot_split(a, b):
    a_hi, a_lo = _split_hi_lo(a)
    b_hi, b_lo = _split_hi_lo(b)
    return _dot(a_hi, b_hi) + _dot(a_lo, b_hi) + _dot(a_hi, b_lo)


def _inproj_kernel(x_ref, nw_ref, w_ref, wab_ref, proj_ref, ab_ref, abt_ref, h_ref):
    @pl.when(pl.program_id(1) == 0)
    def _():
        x = x_ref[...]
        ms = jnp.mean(x * x, axis=-1, keepdims=True)
        h = (x * lax.rsqrt(ms + RMS_EPS) * nw_ref[...]).astype(BF16)
        h_ref[...] = h
        w_ab = jnp.concatenate(
            [wab_ref[...].astype(BF16), jnp.zeros((HEAD_LANES - AB_COLS, D_MODEL), BF16)], axis=0)
        ab = lax.dot_general(h, w_ab, NT_DIMS, preferred_element_type=F32)
        ab_ref[...] = ab[:, 0:AB_COLS]
        abt_ref[...] = ab.T[0:AB_COLS, :]

    proj_ref[...] = lax.dot_general(h_ref[...], w_ref[...].astype(BF16), NT_DIMS,
                                    preferred_element_type=F32).astype(BF16)


def _inproj(x2, norm_w, w_in_t, layer):
    m = x2.shape[0]
    tm, tn = INPROJ_TM, INPROJ_TN
    return pl.pallas_call(
        _inproj_kernel,
        grid=(m // tm, MAIN_COLS // tn),
        in_specs=[
            pl.BlockSpec((tm, D_MODEL), lambda i, j: (i, 0)),
            pl.BlockSpec((1, D_MODEL), lambda i, j: (0, 0)),
            pl.BlockSpec((None, tn, D_MODEL), lambda i, j: (layer, j, 0)),
            pl.BlockSpec((None, AB_COLS, D_MODEL), lambda i, j: (layer, MAIN_COLS // AB_COLS, 0)),
        ],
        out_specs=[
            pl.BlockSpec((tm, tn), lambda i, j: (i, j)),
            pl.BlockSpec((tm, AB_COLS), lambda i, j: (i, 0)),
            pl.BlockSpec((AB_COLS, tm), lambda i, j: (0, i)),
        ],
        out_shape=[
            jax.ShapeDtypeStruct((m, MAIN_COLS), BF16),
            jax.ShapeDtypeStruct((m, AB_COLS), F32),
            jax.ShapeDtypeStruct((AB_COLS, m), F32),
        ],
        scratch_shapes=[pltpu.VMEM((tm, D_MODEL), BF16)],
        compiler_params=pltpu.CompilerParams(
            dimension_semantics=("parallel", "arbitrary"), vmem_limit_bytes=GDN_VMEM_LIMIT),
        name="inproj",
    )(x2, norm_w, w_in_t, w_in_t)


def _bias_kernel(rb_ref, o_ref, *, tk):
    h = pl.program_id(0)
    far = rb_ref[REL_BUCKETS - 1, h]
    br, bc = BIAS_BLOCK
    shape = (br, bc)
    for rb in range(o_ref.shape[0] // br):
        for cb in range(o_ref.shape[1] // bc):
            blk = (slice(rb * br, (rb + 1) * br), slice(cb * bc, (cb + 1) * bc))
            d_min = cb * bc - (rb * br + br - 1) + tk
            d_max = cb * bc + bc - 1 - rb * br + tk
            if d_max < 0:
                o_ref[blk] = jnp.full(shape, NEG_INF, F32)
            elif d_min >= REL_MAX_DIST:
                o_ref[blk] = jnp.zeros(shape, F32)
            else:
                c = lax.broadcasted_iota(jnp.int32, shape, 0) + rb * br
                r = lax.broadcasted_iota(jnp.int32, shape, 1) + cb * bc
                dist = r - c + tk
                n = jnp.maximum(dist, 0)
                nf = jnp.maximum(n, REL_MAX_EXACT).astype(F32)
                large = REL_MAX_EXACT + (jnp.log(nf / REL_MAX_EXACT) / math.log(REL_MAX_DIST / REL_MAX_EXACT)
                                         * (REL_BUCKETS - REL_MAX_EXACT)).astype(jnp.int32)
                large = jnp.minimum(large, REL_BUCKETS - 1)
                bucket = jnp.where(n < REL_MAX_EXACT, n, large)
                val = jnp.zeros(shape, F32)
                for b in range(REL_BUCKETS - 1):
                    val = jnp.where(bucket == b, rb_ref[b, h] - far, val)
                o_ref[blk] = jnp.where(dist >= 0, val * LOG2_E, NEG_INF)


def _near_bias(rel_bias):
    tq, tk = ATTN_TQ, ATTN_TK
    return pl.pallas_call(
        functools.partial(_bias_kernel, tk=tk),
        grid=(DA_HEADS,),
        in_specs=[pl.BlockSpec(memory_space=pltpu.SMEM)],
        out_specs=pl.BlockSpec((None, 2 * tk, tq), lambda h: (h, 0, 0)),
        out_shape=jax.ShapeDtypeStruct((DA_HEADS, 2 * tk, tq), F32),
        name="near_bias",
    )(rel_bias)


def _attn_kernel(q_ref, qnext_ref, k_ref, v_ref, g_ref, bias_ref, lam_ref, sw_ref, o_ref,
                 qz_ref, vt_ref, m_ref, acc_ref, sa_ref, sb_ref, sc_ref, *, tq, tk, lambda_init):
    i = pl.program_id(2)
    hl = HEAD_LANES
    lt = ATTN_LANE_TILE
    hp = ATTN_HEADS_PER_STEP
    n_tiles = 2 * tq // lt
    q_tiles = tq // lt
    head_cols = [slice(hd * hl, (hd + 1) * hl) for hd in range(hp)]

    units = [(hd, t) for hd in range(hp) for t in range(n_tiles)]
    cols = [slice(t * lt, (t + 1) * lt) for t in range(n_tiles)]

    def set_queries(src_ref):
        for hd in range(hp):
            q = src_ref[:, head_cols[hd]] * (DA_HEAD_DIM ** -0.5 * LOG2_E)
            lane = lax.broadcasted_iota(jnp.int32, q.shape, 1)
            zero = jnp.zeros_like(q)
            qz_ref[hd, 0:tq, :] = jnp.where(lane < DA_HEAD_DIM, q, zero)
            qz_ref[hd, tq:2 * tq, :] = jnp.where(lane >= DA_HEAD_DIM, q, zero)

    def keys_used(t, diag):
        return (t % q_tiles + 1) * lt if diag else tk

    def scores(j, s_ref, diag=False):
        row0 = pl.multiple_of(j * tk, tk)
        for u, (hd, t) in enumerate(units):
            nk = keys_used(t, diag)
            kj = k_ref[pl.ds(row0, nk), head_cols[hd]]
            s_ref[u, 0:nk, :] = lax.dot_general(kj, qz_ref[hd, cols[t], :], NT_DIMS, preferred_element_type=F32)

    @pl.when(i == 0)
    def _():
        ones = jnp.ones((vt_ref.shape[2] - hl, tk), BF16)
        for hd in range(hp):
            for j in range(vt_ref.shape[1]):
                vt_ref[hd, j, 0:hl, :] = v_ref[j * tk:(j + 1) * tk, head_cols[hd]].astype(F32).T.astype(BF16)
                vt_ref[hd, j, hl:, :] = ones
        set_queries(q_ref)
        scores(0, sa_ref)

    m_ref[...] = jnp.full(m_ref.shape, NEG_INF, F32)
    acc_ref[...] = jnp.zeros(acc_ref.shape, F32)

    def add_bias(a, hd, t, tile_row0, first_row):
        n_rows = a.shape[0]
        if first_row >= n_rows:
            return a
        b = bias_ref[hd, tile_row0 + first_row:tile_row0 + n_rows, cols[t % q_tiles]]
        if first_row == 0:
            return a + b
        return jnp.concatenate([a[0:first_row, :], a[first_row:, :] + b], axis=0)

    def softmax_pv(j, s_ref, kind="far"):
        diag = kind == "diag"
        nk = [keys_used(t, diag) for _, t in units]
        s = [s_ref[u, 0:nk[u], :] for u in range(len(units))]
        if kind == "prev":
            s = [add_bias(s[u], hd, t, 0, tk - REL_MAX_DIST) if t % q_tiles == 0 else s[u]
                 for u, (hd, t) in enumerate(units)]
        elif diag:
            s = [add_bias(s[u], hd, t, tk, max((t % q_tiles) * lt - REL_MAX_DIST, 0))
                 for u, (hd, t) in enumerate(units)]
        m_prev = [m_ref[hd, :, cols[t]] for hd, t in units]
        m_new = [jnp.maximum(mp, jnp.max(a, axis=0, keepdims=True)) for mp, a in zip(m_prev, s)]
        p = [jnp.exp2(a - mn).astype(BF16) for a, mn in zip(s, m_new)]
        pv = [_dot(vt_ref[hd, j, :, 0:nk[u]], p[u]) for u, (hd, t) in enumerate(units)]
        for u, (hd, t) in enumerate(units):
            acc_ref[hd, :, cols[t]] = jnp.exp2(m_prev[u] - m_new[u]) * acc_ref[hd, :, cols[t]] + pv[u]
            m_ref[hd, :, cols[t]] = m_new[u]

    n_far = jnp.maximum(i - 1, 0)
    odd = n_far % 2

    def far_pair(jj, carry):
        j = 2 * jj
        scores(j + 1, sb_ref)
        softmax_pv(j, sa_ref)
        scores(j + 2, sa_ref)
        softmax_pv(j + 1, sb_ref)
        return carry

    lax.fori_loop(0, n_far // 2, far_pair, 0)

    def finish(prev_ref):
        if prev_ref is None:
            softmax_pv(0, sa_ref, "diag")
        else:
            scores(i, sb_ref, diag=True)
            softmax_pv(n_far, prev_ref, "prev")
        set_queries(qnext_ref)
        scores(0, sa_ref)
        if prev_ref is not None:
            softmax_pv(i, sb_ref, "diag")

        lp = lam_ref[...]
        s1 = jnp.sum(lp[0:1, :] * lp[1:2, :], axis=-1, keepdims=True)
        s2 = jnp.sum(lp[2:3, :] * lp[3:4, :], axis=-1, keepdims=True)
        lam = jnp.exp(s1) - jnp.exp(s2) + lambda_init
        for hd in range(hp):
            inv_l = 1.0 / acc_ref[hd, hl:hl + 1, :]
            o_t = (acc_ref[hd, 0:hl, 0:tq] * inv_l[:, 0:tq]
                   - lam * (acc_ref[hd, 0:hl, tq:2 * tq] * inv_l[:, tq:2 * tq]))
            o = o_t.T
            ms = jnp.mean(o * o, axis=-1, keepdims=True)
            y = o * lax.rsqrt(ms + RMS_EPS) * sw_ref[...] * (1.0 - lambda_init)
            o_ref[:, head_cols[hd]] = (y * _silu(g_ref[:, head_cols[hd]].astype(F32))).astype(BF16)

    @pl.when(i == 0)
    def _():
        finish(None)

    @pl.when(jnp.logical_and(i > 0, odd == 0))
    def _():
        finish(sa_ref)

    @pl.when(odd == 1)
    def _():
        scores(n_far, sc_ref)
        softmax_pv(n_far - 1, sa_ref)
        finish(sc_ref)


def _attention(proj3, near_bias, lam_params, subln_w, lambda_init):
    b, t, _ = proj3.shape
    tq, tk = ATTN_TQ, ATTN_TK
    assert tq == tk and t % tq == 0
    hl = HEAD_LANES
    hp = ATTN_HEADS_PER_STEP
    wide = hp * hl
    sect = DA_WIDTH // wide
    n_units = hp * (2 * tq // ATTN_LANE_TILE)
    return pl.pallas_call(
        functools.partial(_attn_kernel, tq=tq, tk=tk, lambda_init=lambda_init),
        grid=(b, DA_HEADS // hp, t // tq),
        in_specs=[
            pl.BlockSpec((None, tq, wide), lambda bi, h, i: (bi, i, h)),
            pl.BlockSpec((None, tq, wide), lambda bi, h, i: (bi, jnp.minimum(i + 1, t // tq - 1), h)),
            pl.BlockSpec((None, t, wide), lambda bi, h, i: (bi, 0, sect + h)),
            pl.BlockSpec((None, t, wide), lambda bi, h, i: (bi, 0, 2 * sect + h)),
            pl.BlockSpec((None, tq, wide), lambda bi, h, i: (bi, i, 3 * sect + h)),
            pl.BlockSpec((hp, 2 * tk, tq), lambda bi, h, i: (h, 0, 0)),
            pl.BlockSpec((4, DA_HEAD_DIM), lambda bi, h, i: (0, 0)),
            pl.BlockSpec((1, hl), lambda bi, h, i: (0, 0)),
        ],
        out_specs=pl.BlockSpec((None, tq, wide), lambda bi, h, i: (bi, i, h)),
        out_shape=jax.ShapeDtypeStruct((b, t, DA_WIDTH), BF16),
        scratch_shapes=[
            pltpu.VMEM((hp, 2 * tq, hl), BF16),
            pltpu.VMEM((hp, t // tk, hl + BF16_ROWS, tk), BF16),
            pltpu.VMEM((hp, 1, 2 * tq), F32),
            pltpu.VMEM((hp, hl + BF16_ROWS, 2 * tq), F32),
            pltpu.VMEM((n_units, tk, ATTN_LANE_TILE), F32),
            pltpu.VMEM((n_units, tk, ATTN_LANE_TILE), F32),
            pltpu.VMEM((n_units, tk, ATTN_LANE_TILE), F32),
        ],
        compiler_params=pltpu.CompilerParams(
            dimension_semantics=("parallel", "parallel", "arbitrary"), vmem_limit_bytes=VMEM_LIMIT),
        name="diff_attn",
    )(proj3, proj3, proj3, proj3, proj3, near_bias, lam_params, subln_w)


LST_PITCH = 72
SUBLANES = 8
XPOSE_LEAD = 8


def _unit_lower_inverse_on_lanes(lst_ref, ainv_ref, lt_ref, tt_ref, n_inst):
    c_len = GDN_CHUNK
    sub = lax.broadcasted_iota(jnp.int32, (SUBLANES, n_inst), 0)
    zero = jnp.zeros((SUBLANES, n_inst), F32)

    def gather(r):
        lt_ref[r] = lst_ref[pl.ds(r, n_inst, stride=LST_PITCH), :].T[0:c_len, :]

    def scatter(r):
        t = tt_ref[r]
        ainv_ref[pl.ds(r, n_inst, stride=LST_PITCH), :] = jnp.concatenate([t, jnp.zeros_like(t)], axis=0).T

    for r in range(XPOSE_LEAD):
        gather(r)
    for i in range(c_len):
        if i + XPOSE_LEAD < c_len:
            gather(i + XPOSE_LEAD)
        if i >= XPOSE_LEAD:
            scatter(i - XPOSE_LEAD)
        n_blk = (i - 1) // SUBLANES + 1 if i > 0 else 0
        acc = [None] * n_blk
        for k in range(i):
            lik = jnp.broadcast_to(lt_ref[i, k:k + 1, :], (SUBLANES, n_inst))
            for jb in range(k // SUBLANES + 1):
                term = lik * tt_ref[k, jb * SUBLANES:(jb + 1) * SUBLANES, :]
                acc[jb] = term if acc[jb] is None else acc[jb] + term
        for jb in range(c_len // SUBLANES):
            val = -acc[jb] if jb < n_blk else zero
            if jb == i // SUBLANES:
                val = jnp.where(sub == i % SUBLANES, 1.0, val)
            tt_ref[i, jb * SUBLANES:(jb + 1) * SUBLANES, :] = val
    for r in range(c_len - XPOSE_LEAD, c_len):
        scatter(r)


def _gdn_kernel(qp_ref, kp_ref, vp_ref, z_ref, ab_ref, *refs, tb):
    c_len = GDN_CHUNK
    hl = HEAD_LANES
    bp = GDN_BATCH_ROWS
    n_chunk = tb // c_len
    n_inst = bp * n_chunk * GDN_HEADS
    halo = BF16_ROWS
    abt_refs = refs[0:bp]
    (cw_ref, alog_ref, dtb_ref, alogt_ref, dtbt_ref, nw_ref, o_ref,
     s_ref, cbuf_ref, qn_ref, kn_ref, vn_ref, gc_ref, gct_ref, beta_ref, bt_ref,
     lst_ref, ainv_ref, lt_ref, tt_ref, qk_ref) = refs[bp:]

    @pl.when(pl.program_id(1) == 0)
    def _():
        s_ref[...] = jnp.zeros(s_ref.shape, F32)
        cbuf_ref[:, 0:halo, :] = jnp.zeros((bp * 3, halo, GDN_WIDTH), BF16)

    sel_r = lax.broadcasted_iota(jnp.int32, ((CONV_K - 1) * c_len, halo + c_len), 0)
    sel_c = lax.broadcasted_iota(jnp.int32, ((CONV_K - 1) * c_len, halo + c_len), 1)
    tap = sel_r // c_len
    shift_mat = (sel_c == sel_r - tap * c_len + halo - (CONV_K - 1) + tap).astype(BF16)

    def conv_chunk(c, carry):
        rows = pl.ds(pl.multiple_of(c * c_len, c_len), c_len)
        groups = ((qp_ref, qn_ref), (kp_ref, kn_ref), (vp_ref, vn_ref))
        for bb, (g, (src, dst)) in [(bb, gs) for bb in range(bp) for gs in enumerate(groups)]:
            stage = bb * 3 + g
            xb = src[bb, rows, :]
            cbuf_ref[stage, halo:halo + c_len, :] = xb
            w = 0.5 * cw_ref[:, g * GDN_WIDTH:(g + 1) * GDN_WIDTH]
            taps = _dot(shift_mat, cbuf_ref[stage])
            hy = w[0:1, :] * taps[0:c_len, :]
            hy = hy + w[1:2, :] * taps[c_len:2 * c_len, :]
            hy = hy + w[2:3, :] * taps[2 * c_len:3 * c_len, :]
            hy = hy + w[3:4, :] * xb.astype(F32)
            cbuf_ref[stage, 0:halo, :] = xb[c_len - halo:c_len, :]
            y = hy + hy * jnp.tanh(hy)
            for h in range(GDN_HEADS):
                cs = slice(h * hl, (h + 1) * hl)
                yh = y[:, cs]
                if g < 2:
                    inv_norm = lax.rsqrt(jnp.sum(yh * yh, axis=-1, keepdims=True) + RMS_EPS)
                    if g == 0:
                        inv_norm = inv_norm * (GDN_HEAD_DIM ** -0.5)
                    yh = yh * inv_norm
                dst[bb, rows, cs] = yh.astype(BF16)
        return carry

    lax.fori_loop(0, n_chunk, conv_chunk, 0)

    row = lax.broadcasted_iota(jnp.int32, (c_len, c_len), 0)
    col = lax.broadcasted_iota(jnp.int32, (c_len, c_len), 1)
    tril = row >= col
    strict = row > col
    tril_f = tril.astype(F32)
    triu_f = (row <= col).astype(F32)
    for bb in range(bp):
        ab = ab_ref[bb]
        beta_ref[bb] = _sigmoid(ab[:, 0:GDN_HEADS])
        g_raw = -jnp.exp(alog_ref[...]) * _softplus(ab[:, GDN_HEADS:AB_COLS] + dtb_ref[...])
        abt = abt_refs[bb][...]
        beta_t = _sigmoid(abt[0:GDN_HEADS, :])
        g_raw_t = -jnp.exp(alogt_ref[...]) * _softplus(abt[GDN_HEADS:AB_COLS, :] + dtbt_ref[...])
        for c in range(n_chunk):
            cc = bb * n_chunk + c
            gc_ref[bb, c * c_len:(c + 1) * c_len, :] = _dot_split(tril_f, g_raw[c * c_len:(c + 1) * c_len, :])
            gct_ref[cc] = _dot_split(g_raw_t[:, c * c_len:(c + 1) * c_len], triu_f)
            bt_ref[cc] = beta_t[:, c * c_len:(c + 1) * c_len]

    heads = range(GDN_HEADS)
    lanes = [slice(h * hl, (h + 1) * hl) for h in heads]

    def intra_chunk(c, carry):
        rows = pl.ds(pl.multiple_of(c * c_len, c_len), c_len)
        pad = jnp.zeros((c_len, hl - c_len), F32)
        for bb in range(bp):
            cc = bb * n_chunk + c
            gct = gct_ref[cc]
            gcc = gc_ref[bb, rows, :]
            betac = beta_ref[bb, rows, :]
            ks = [kn_ref[bb, rows, lanes[h]] for h in heads]
            kq = [jnp.concatenate([ks[h], qn_ref[bb, rows, lanes[h]]], axis=0) for h in heads]
            kk = [lax.dot_general(kq[h], ks[h], NT_DIMS, preferred_element_type=F32) for h in heads]
            decay = [jnp.where(tril, jnp.exp(jnp.where(tril, gcc[:, h:h + 1] - gct[h:h + 1, :], 0.0)), 0.0)
                     for h in heads]
            for h in heads:
                l_strict = jnp.where(strict, kk[h][0:c_len, :] * decay[h] * betac[:, h:h + 1], 0.0)
                base = pl.multiple_of((cc * GDN_HEADS + h) * LST_PITCH, SUBLANES)
                lst_ref[pl.ds(base, c_len), :] = jnp.concatenate([l_strict, pad], axis=1)
                qk_ref[cc * GDN_HEADS + h] = (kk[h][c_len:2 * c_len, :] * decay[h]).astype(BF16)
        return carry

    lax.fori_loop(0, n_chunk, intra_chunk, 0)

    _unit_lower_inverse_on_lanes(lst_ref, ainv_ref, lt_ref, tt_ref, n_inst)

    chains = [(bb, h) for bb in range(bp) for h in heads]
    idx = range(len(chains))

    def scan_chunk(c, carry):
        rows = pl.ds(pl.multiple_of(c * c_len, c_len), c_len)
        ccs = [bb * n_chunk + c for bb in range(bp)]
        gct = [gct_ref[cc] for cc in ccs]
        gcc = [gc_ref[bb, rows, :] for bb in range(bp)]
        bct = [bt_ref[cc] for cc in ccs]
        inst = [ccs[bb] * GDN_HEADS + h for bb, h in chains]
        g_row = [gct[bb][h:h + 1, :] for bb, h in chains]
        g_last = [g[:, c_len - 1:c_len] for g in g_row]
        ks = [kn_ref[bb, rows, lanes[h]] for bb, h in chains]
        a_inv = [ainv_ref[pl.ds(pl.multiple_of(n * LST_PITCH, SUBLANES), c_len), 0:c_len] for n in inst]
        b_row = [bct[bb][h:h + 1, :] for bb, h in chains]
        u = [_dot((a_inv[x] * b_row[x]).astype(BF16), vn_ref[bb, rows, lanes[h]]) for x, (bb, h) in enumerate(chains)]
        w = [_dot((a_inv[x] * (b_row[x] * jnp.exp(g_row[x]))).astype(BF16), ks[x]) for x in idx]
        s_old = [s_ref[bb * GDN_HEADS + h] for bb, h in chains]
        ws = [_dot(jnp.concatenate([w[x].astype(BF16), qn_ref[bb, rows, lanes[h]]], axis=0),
                   s_old[x].astype(BF16)) for x, (bb, h) in enumerate(chains)]
        v_new = [(u[x] - ws[x][0:c_len, :]).astype(BF16) for x in idx]
        kd_t = [(ks[x].astype(F32).T * jnp.exp(g_last[x] - g_row[x])).astype(BF16) for x in idx]
        for x, (bb, h) in enumerate(chains):
            s_ref[bb * GDN_HEADS + h] = s_old[x] * jnp.exp(g_last[x]) + _dot(kd_t[x], v_new[x])
        o = [jnp.exp(gcc[bb][:, h:h + 1]) * ws[x][c_len:2 * c_len, :] + _dot(qk_ref[inst[x]], v_new[x])
             for x, (bb, h) in enumerate(chains)]
        for x, (bb, h) in enumerate(chains):
            on = o[x] * lax.rsqrt(jnp.mean(o[x] * o[x], axis=-1, keepdims=True) + RMS_EPS) * nw_ref[...]
            zc = z_ref[bb, rows, lanes[h]].astype(F32)
            o_ref[bb, rows, lanes[h]] = (on * _silu(zc)).astype(BF16)
        return carry

    lax.fori_loop(0, n_chunk, scan_chunk, 0)


def _gdn(proj3, ab3, abt2, conv_w, a_log, dt_bias, gdn_norm_w):
    b, t, _ = proj3.shape
    tb = GDN_TB
    gw = GDN_WIDTH
    first = 4 * DA_WIDTH // gw
    bp = GDN_BATCH_ROWS
    n_chunk = tb // GDN_CHUNK
    n_inst = bp * n_chunk * GDN_HEADS
    n_tb = t // tb
    assert b % bp == 0
    assert n_inst == HEAD_LANES, "one (batch row, chunk, head) matrix per lane"

    def abt_spec(row):
        return pl.BlockSpec((AB_COLS, tb), lambda bi, ti: (0, (bp * bi + row) * n_tb + ti))

    return pl.pallas_call(
        functools.partial(_gdn_kernel, tb=tb),
        grid=(b // bp, n_tb),
        in_specs=[
            pl.BlockSpec((bp, tb, gw), lambda bi, ti: (bi, ti, first)),
            pl.BlockSpec((bp, tb, gw), lambda bi, ti: (bi, ti, first + 1)),
            pl.BlockSpec((bp, tb, gw), lambda bi, ti: (bi, ti, first + 2)),
            pl.BlockSpec((bp, tb, gw), lambda bi, ti: (bi, ti, first + 3)),
            pl.BlockSpec((bp, tb, AB_COLS), lambda bi, ti: (bi, ti, 0)),
            *[abt_spec(row) for row in range(bp)],
            pl.BlockSpec((CONV_K, 3 * gw), lambda bi, ti: (0, 0)),
            pl.BlockSpec((1, GDN_HEADS), lambda bi, ti: (0, 0)),
            pl.BlockSpec((1, GDN_HEADS), lambda bi, ti: (0, 0)),
            pl.BlockSpec((GDN_HEADS, 1), lambda bi, ti: (0, 0)),
            pl.BlockSpec((GDN_HEADS, 1), lambda bi, ti: (0, 0)),
            pl.BlockSpec((1, GDN_HEAD_DIM), lambda bi, ti: (0, 0)),
        ],
        out_specs=pl.BlockSpec((bp, tb, gw), lambda bi, ti: (bi, ti, 0)),
        out_shape=jax.ShapeDtypeStruct((b, t, gw), BF16),
        scratch_shapes=[
            pltpu.VMEM((bp * GDN_HEADS, GDN_HEAD_DIM, GDN_HEAD_DIM), F32),
            pltpu.VMEM((bp * 3, BF16_ROWS + GDN_CHUNK, gw), BF16),
            pltpu.VMEM((bp, tb, gw), BF16),
            pltpu.VMEM((bp, tb, gw), BF16),
            pltpu.VMEM((bp, tb, gw), BF16),
            pltpu.VMEM((bp, tb, GDN_HEADS), F32),
            pltpu.VMEM((bp * n_chunk, GDN_HEADS, GDN_CHUNK), F32),
            pltpu.VMEM((bp, tb, GDN_HEADS), F32),
            pltpu.VMEM((bp * n_chunk, GDN_HEADS, GDN_CHUNK), F32),
            pltpu.VMEM((n_inst * LST_PITCH, HEAD_LANES), F32),
            pltpu.VMEM((n_inst * LST_PITCH, HEAD_LANES), F32),
            pltpu.VMEM((GDN_CHUNK, GDN_CHUNK, n_inst), F32),
            pltpu.VMEM((GDN_CHUNK, GDN_CHUNK, n_inst), F32),
            pltpu.VMEM((n_inst, GDN_CHUNK, GDN_CHUNK), BF16),
        ],
        compiler_params=pltpu.CompilerParams(
            dimension_semantics=("parallel", "arbitrary"), vmem_limit_bytes=GDN_VMEM_LIMIT),
        name="gdn",
    )(proj3, proj3, proj3, proj3, ab3, *([abt2] * bp), conv_w,
      a_log.reshape(1, GDN_HEADS), dt_bias.reshape(1, GDN_HEADS),
      a_log.reshape(GDN_HEADS, 1), dt_bias.reshape(GDN_HEADS, 1),
      gdn_norm_w.reshape(1, GDN_HEAD_DIM))


def _outproj_kernel(oda_ref, ogdn_ref, w_ref, x_ref, fw_ref, o_ref, *, final):
    acc = _dot(oda_ref[...], w_ref[0:DA_WIDTH, :]) + _dot(ogdn_ref[...], w_ref[DA_WIDTH:DA_WIDTH + GDN_WIDTH, :])
    y = x_ref[...] + acc
    if final:
        y = y * lax.rsqrt(jnp.mean(y * y, axis=-1, keepdims=True) + RMS_EPS) * fw_ref[...]
    o_ref[...] = y


def _outproj(o_da2, o_gdn2, w_out_bf16, layer, x2, final_w, final):
    m = x2.shape[0]
    tm = OUTPROJ_TM
    return pl.pallas_call(
        functools.partial(_outproj_kernel, final=final),
        grid=(m // tm,),
        in_specs=[
            pl.BlockSpec((tm, DA_WIDTH), lambda i: (i, 0)),
            pl.BlockSpec((tm, GDN_WIDTH), lambda i: (i, 0)),
            pl.BlockSpec((None, DA_WIDTH + GDN_WIDTH, D_MODEL), lambda i: (layer, 0, 0)),
            pl.BlockSpec((tm, D_MODEL), lambda i: (i, 0)),
            pl.BlockSpec((1, D_MODEL), lambda i: (0, 0)),
        ],
        out_specs=pl.BlockSpec((tm, D_MODEL), lambda i: (i, 0)),
        out_shape=jax.ShapeDtypeStruct((m, D_MODEL), F32),
        compiler_params=pltpu.CompilerParams(
            dimension_semantics=("parallel",), vmem_limit_bytes=VMEM_LIMIT),
        name="outproj",
    )(o_da2, o_gdn2, w_out_bf16, x2, final_w)


def kernel(x, norm_w, w_in, w_out, lambda_q1, lambda_k1, lambda_q2, lambda_k2, da_subln_w, rel_bias,
           conv_w, a_log, dt_bias, gdn_norm_w, final_norm_w):
    b, t, d = x.shape
    m = b * t
    x2 = x.reshape(m, d)
    near_bias = _near_bias(rel_bias)
    final_w = final_norm_w.reshape(1, d)
    w_out_bf16 = w_out.astype(BF16)
    w_in_t = jnp.swapaxes(w_in, 1, 2)
    for l in range(DEPTH):
        lambda_init = 0.8 - 0.6 * math.exp(-0.3 * l)
        proj, ab, abt = _inproj(x2, norm_w[l].reshape(1, d), w_in_t, l)
        proj3 = proj.reshape(b, t, MAIN_COLS)
        lam_params = jnp.stack([lambda_q1[l], lambda_k1[l], lambda_q2[l], lambda_k2[l]])
        o_da = _attention(proj3, near_bias, lam_params, da_subln_w[l].reshape(1, 2 * DA_HEAD_DIM), lambda_init)
        o_gdn = _gdn(proj3, ab.reshape(b, t, AB_COLS), abt, conv_w[l], a_log[l], dt_bias[l], gdn_norm_w[l])
        x2 = _outproj(o_da.reshape(m, DA_WIDTH), o_gdn.reshape(m, GDN_WIDTH), w_out_bf16, l,
                      x2, final_w, final=(l == DEPTH - 1))
    return x2.reshape(b, t, d)
```
